```python
import math
import jax, jax.numpy as jnp
from jax import lax
import numpy as np

D_MODEL = 1024
BATCH = 1
SEQ = 16384
DEPTH = 2
DEC_BATCH = 16
DEC_SEQ = 4096
PAST_LEN = 128

MIX_WIDTH = D_MODEL
GROUP_WIDTH = MIX_WIDTH // 4
CONV_WIDTH = 3
CONV_CH = GROUP_WIDTH
DIFF_HEADS = 4
DIFF_HEAD_DIM = GROUP_WIDTH // DIFF_HEADS
DIFF_QK_DIM = DIFF_HEAD_DIM // 2
MLA_HEADS = 4
MLA_NOPE = 64
MLA_ROPE = 32
MLA_V = GROUP_WIDTH // MLA_HEADS
MLA_Q_RANK = 256
MLA_KV_RANK = 128
ROPE_THETA = 10000.0
FOURIER_CH = GROUP_WIDTH
FOURIER_GROUPS = 4
D_FF = 2816
Q_BLOCK = 128
NORM_EPS = 1e-6
N_MOD = 9

_SEG = (CONV_CH, CONV_CH, CONV_CH,
        GROUP_WIDTH, GROUP_WIDTH, GROUP_WIDTH,
        MLA_Q_RANK, MLA_KV_RANK, MLA_ROPE,
        FOURIER_CH)
IN_COLS = sum(_SEG)
SPLIT_POINTS = tuple(int(v) for v in np.cumsum(_SEG)[:-1])

kernel_name = "hybrid_parallel_encoder"


def _rms(x, g):
    xf = x.astype(jnp.float32)
    y = xf * lax.rsqrt(jnp.mean(xf * xf, axis=-1, keepdims=True) + NORM_EPS)
    return (y * g.astype(jnp.float32)).astype(x.dtype)


def _alibi_slopes(n):
    return jnp.asarray(2.0 ** (-8.0 * np.arange(1, n + 1) / n), dtype=jnp.float32)


def _query_blocks(t):
    b, s, h, d = t.shape
    return jnp.moveaxis(t.reshape(b, s // Q_BLOCK, Q_BLOCK, h, d), 1, 0)


def _unblock(t):
    n, b, q, h, d = t.shape
    return jnp.moveaxis(t, 0, 1).reshape(b, n * q, h, d)


def _swiglu(h, w_gu, w_down):
    gu = jnp.einsum('bsd,df->bsf', h, w_gu)
    g, u = jnp.split(gu, 2, axis=-1)
    return jnp.einsum('bsf,fd->bsd', jax.nn.silu(g) * u, w_down)


def _rope(t):
    s = t.shape[1]
    inv = ROPE_THETA ** (-jnp.arange(0, MLA_ROPE, 2, dtype=jnp.float32) / MLA_ROPE)
    ang = jnp.arange(s, dtype=jnp.float32)[:, None] * inv[None, :]
    cos = jnp.cos(ang)[None, :, None, :]
    sin = jnp.sin(ang)[None, :, None, :]
    t1, t2 = jnp.split(t.astype(jnp.float32), 2, axis=-1)
    return jnp.concatenate([t1 * cos - t2 * sin, t1 * sin + t2 * cos], axis=-1).astype(t.dtype)


def _diff_attention(q1, q2, k1, k2, v, lam, slopes):
    b, s, h, dq = q1.shape
    scale = dq ** -0.5
    k_pos = jnp.arange(s, dtype=jnp.float32)
    vf = v.astype(jnp.float32)
    n_blk = s // Q_BLOCK

    def block(args):
        i, q1b, q2b = args
        q_pos = (i * Q_BLOCK + jnp.arange(Q_BLOCK)).astype(jnp.float32)
        bias = -slopes[:, None, None] * jnp.abs(q_pos[:, None] - k_pos[None, :])[None]
        s1 = jnp.einsum('bqhd,bkhd->bhqk', q1b, k1, preferred_element_type=jnp.float32) * scale + bias
        s2 = jnp.einsum('bqhd,bkhd->bhqk', q2b, k2, preferred_element_type=jnp.float32) * scale + bias
        a = jax.nn.softmax(s1, axis=-1) - lam * jax.nn.softmax(s2, axis=-1)
        return jnp.einsum('bhqk,bkhd->bqhd', a, vf)

    out = lax.map(block, (jnp.arange(n_blk), _query_blocks(q1), _query_blocks(q2)))
    return _unblock(out)


def _mla_attention(q, k, v):
    scale = q.shape[-1] ** -0.5
    vf = v.astype(jnp.float32)

    def block(qb):
        sc = jnp.einsum('bqhd,bkhd->bhqk', qb, k, preferred_element_type=jnp.float32) * scale
        p = jax.nn.softmax(sc, axis=-1)
        return jnp.einsum('bhqk,bkhd->bqhd', p, vf)

    return _unblock(lax.map(block, _query_blocks(q)))


def _token_mixer(h, p, l):
    b, s, _ = h.shape
    dt = h.dtype
    proj = jnp.einsum('bsd,de->bse', h, p['w_in'][l])
    (a_b, a_c, a_x, d_q, d_k, d_v, m_cq, m_ckv, m_kpe, f_u) = jnp.split(proj, SPLIT_POINTS, axis=-1)

    conv_w = p['conv_w'][l]
    u = a_c * a_x
    up = jnp.pad(u, ((0, 0), (1, 1), (0, 0)))
    conv = up[:, :-2] * conv_w[0] + up[:, 1:-1] * conv_w[1] + up[:, 2:] * conv_w[2]
    y_a = a_b * conv

    lambda_init = 0.8 - 0.6 * math.exp(-0.3 * l)
    q = _rms(d_q.reshape(b, s, DIFF_HEADS, 2, DIFF_QK_DIM), p['diff_q_g'][l])
    k = _rms(d_k.reshape(b, s, DIFF_HEADS, 2, DIFF_QK_DIM), p['diff_k_g'][l])
    v = d_v.reshape(b, s, DIFF_HEADS, DIFF_HEAD_DIM)
    lv = p['diff_lambda'][l].astype(jnp.float32)
    lam = jnp.exp(jnp.sum(lv[0] * lv[1])) - jnp.exp(jnp.sum(lv[2] * lv[3])) + lambda_init
    o = _diff_attention(q[..., 0, :], q[..., 1, :], k[..., 0, :], k[..., 1, :], v, lam,
                        _alibi_slopes(DIFF_HEADS))
    o = _rms(o, p['diff_subln_g'][l]) * (1.0 - lambda_init)
    y_b = o.reshape(b, s, GROUP_WIDTH).astype(dt)

    cq = _rms(m_cq, p['mla_q_a_g'][l])
    qm = jnp.einsum('bsr,re->bse', cq, p['mla_w_uq'][l]).reshape(b, s, MLA_HEADS, MLA_NOPE + MLA_ROPE)
    ckv = _rms(m_ckv, p['mla_kv_a_g'][l])
    kv = jnp.einsum('bsr,re->bse', ckv, p['mla_w_ukv'][l]).reshape(b, s, MLA_HEADS, MLA_NOPE + MLA_V)
    k_nope, vm = jnp.split(kv, [MLA_NOPE], axis=-1)
    k_pe = jnp.broadcast_to(m_kpe[:, :, None, :], (b, s, MLA_HEADS, MLA_ROPE))
    km = jnp.concatenate([k_nope, k_pe], axis=-1)
    qm = _rms(qm, p['mla_q_g'][l])
    km = _rms(km, p['mla_k_g'][l])
    qm = jnp.concatenate([qm[..., :MLA_NOPE], _rope(qm[..., MLA_NOPE:])], axis=-1)
    km = jnp.concatenate([km[..., :MLA_NOPE], _rope(km[..., MLA_NOPE:])], axis=-1)
    y_c = _mla_attention(qm, km, vm).reshape(b, s, GROUP_WIDTH).astype(dt)

    fu = f_u.reshape(b, s, FOURIER_GROUPS, FOURIER_CH // FOURIER_GROUPS).astype(jnp.float32)
    y_d = jnp.fft.fftn(fu, axes=(1, 3), norm='ortho').real.reshape(b, s, FOURIER_CH).astype(dt)

    y = jnp.concatenate([y_a, y_b, y_c, y_d], axis=-1)
    return jnp.einsum('bse,ed->bsd', y, p['w_out'][l])


def _layer(x, c, p, l):
    b = x.shape[0]
    m = (jnp.einsum('bd,de->be', jax.nn.silu(c), p['w_mod'][l]) + p['b_mod'][l]).reshape(b, N_MOD, D_MODEL)
    ng = p['norm_g'][l]

    def modulate(t, i):
        return _rms(t, ng[i]) * (1.0 + m[:, 3 * i + 1, None, :]) + m[:, 3 * i, None, :]

    x = x + 0.5 * m[:, 2, None, :] * _swiglu(modulate(x, 0), p['ffn1_w_gu'][l], p['ffn1_w_down'][l])
    x = x + m[:, 5, None, :] * _token_mixer(modulate(x, 1), p, l)
    x = x + 0.5 * m[:, 8, None, :] * _swiglu(modulate(x, 2), p['ffn2_w_gu'][l], p['ffn2_w_down'][l])
    return x


def _trunk(x, c, p):
    for l in range(DEPTH):
        x = _layer(x, c, p, l)
    return x


def setup_inputs(seed: int = 0) -> dict:
    key = jax.random.key(seed)
    ks = jax.random.split(key, 32)
    f32 = jnp.float32

    def nrm(k, shape, scale):
        return jax.random.normal(k, shape, f32) * scale

    def gain(k, shape):
        return 1.0 + 0.02 * jax.random.normal(k, shape, f32)

    return {
        'x_prompt': nrm(ks[0], (BATCH, SEQ, D_MODEL), 1.0),
        'x_sample': nrm(ks[1], (DEC_BATCH, DEC_SEQ, D_MODEL), 1.0),
        'c_prompt': nrm(ks[2], (BATCH, D_MODEL), 1.0),
        'c_sample': nrm(ks[3], (DEC_BATCH, D_MODEL), 1.0),
        'w_mod': nrm(ks[4], (DEPTH, D_MODEL, N_MOD * D_MODEL), 0.5 * D_MODEL ** -0.5),
        'b_mod': nrm(ks[5], (DEPTH, N_MOD * D_MODEL), 0.02),
        'norm_g': gain(ks[6], (DEPTH, 3, D_MODEL)),
        'ffn1_w_gu': nrm(ks[7], (DEPTH, D_MODEL, 2 * D_FF), D_MODEL ** -0.5),
        'ffn1_w_down': nrm(ks[8], (DEPTH, D_FF, D_MODEL), D_FF ** -0.5),
        'w_in': nrm(ks[9], (DEPTH, D_MODEL, IN_COLS), D_MODEL ** -0.5),
        'conv_w': nrm(ks[10], (DEPTH, CONV_WIDTH, CONV_CH), CONV_WIDTH ** -0.5),
        'diff_lambda': nrm(ks[11], (DEPTH, 4, DIFF_QK_DIM), 0.1),
        'diff_q_g': gain(ks[12], (DEPTH, DIFF_QK_DIM)),
        'diff_k_g': gain(ks[13], (DEPTH, DIFF_QK_DIM)),
        'diff_subln_g': gain(ks[14], (DEPTH, DIFF_HEAD_DIM)),
        'mla_q_a_g': gain(ks[15], (DEPTH, MLA_Q_RANK)),
        'mla_w_uq': nrm(ks[16], (DEPTH, MLA_Q_RANK, MLA_HEADS * (MLA_NOPE + MLA_ROPE)), MLA_Q_RANK ** -0.5),
        'mla_kv_a_g': gain(ks[17], (DEPTH, MLA_KV_RANK)),
        'mla_w_ukv': nrm(ks[18], (DEPTH, MLA_KV_RANK, MLA_HEADS * (MLA_NOPE + MLA_V)), MLA_KV_RANK ** -0.5),
        'mla_q_g': gain(ks[19], (DEPTH, MLA_NOPE + MLA_ROPE)),
        'mla_k_g': gain(ks[20], (DEPTH, MLA_NOPE + MLA_ROPE)),
        'w_out': nrm(ks[21], (DEPTH, MIX_WIDTH, D_MODEL), MIX_WIDTH ** -0.5),
        'ffn2_w_gu': nrm(ks[22], (DEPTH, D_MODEL, 2 * D_FF), D_MODEL ** -0.5),
        'ffn2_w_down': nrm(ks[23], (DEPTH, D_FF, D_MODEL), D_FF ** -0.5),
    }


def reference(x_prompt, x_sample, c_prompt, c_sample, w_mod, b_mod, norm_g, ffn1_w_gu, ffn1_w_down,
              w_in, conv_w, diff_lambda, diff_q_g, diff_k_g, diff_subln_g, mla_q_a_g, mla_w_uq,
              mla_kv_a_g, mla_w_ukv, mla_q_g, mla_k_g, w_out, ffn2_w_gu, ffn2_w_down):
    p = dict(w_mod=w_mod, b_mod=b_mod, norm_g=norm_g, ffn1_w_gu=ffn1_w_gu, ffn1_w_down=ffn1_w_down,
             w_in=w_in, conv_w=conv_w, diff_lambda=diff_lambda, diff_q_g=diff_q_g, diff_k_g=diff_k_g,
             diff_subln_g=diff_subln_g, mla_q_a_g=mla_q_a_g, mla_w_uq=mla_w_uq, mla_kv_a_g=mla_kv_a_g,
             mla_w_ukv=mla_w_ukv, mla_q_g=mla_q_g, mla_k_g=mla_k_g, w_out=w_out,
             ffn2_w_gu=ffn2_w_gu, ffn2_w_down=ffn2_w_down)
    y_prompt = _trunk(x_prompt, c_prompt, p)
    y_sample = _trunk(x_sample, c_sample, p)
    return (y_prompt, y_sample)
```

```python
import functools
import math

import numpy as np
import jax
import jax.numpy as jnp
from jax import lax
from jax.experimental import pallas as pl
from jax.experimental.pallas import tpu as pltpu

F32 = jnp.float32
BF16 = jnp.bfloat16

HEADS = 4
GROUP = 256
DIFF_QK = 32
MLA_NOPE = 64
MLA_ROPE = 32
MLA_QK = MLA_NOPE + MLA_ROPE
MLA_Q_RANK = 256
MLA_KV_RANK = 128
HEAD_V = 64
FOURIER_GROUPS = 4
ROPE_THETA = 10000.0
EPS = 1e-6
N_MOD = 9
ATT_ROWS = 3 * GROUP + MLA_Q_RANK + MLA_KV_RANK + MLA_ROPE
NAT_LO = 3 * GROUP
ATT_HI = NAT_LO + ATT_ROWS

LANES = 128
BF16_SUBLANES = 16
KPAD = 128
V_ROWS = HEAD_V + BF16_SUBLANES
AUG_ROWS = BF16_SUBLANES
FF_CHUNK = 256
VMEM_LIMIT = 52 * 1024 * 1024
NEG_BIG = -1e30


def _cparams(sem):
    return pltpu.CompilerParams(dimension_semantics=sem, vmem_limit_bytes=VMEM_LIMIT)


def _tiles(S):
    ts = min(512, S)
    tq = min(256, S)
    tk = min(512, S)
    return ts, tq, tk


def _fourier_split(S):
    n1 = 1 << ((int(math.log2(S)) + 1) // 2)
    return n1, S // n1


def _mod_kernel(c_ref, w_ref, b_ref, o_ref):
    c = c_ref[...]
    s = (c * jax.nn.sigmoid(c)).astype(BF16)
    o_ref[0] = jnp.dot(s, w_ref[0].astype(BF16), preferred_element_type=F32) + b_ref[0]


def _modulation(c_all, w_mod, b_mod):
    L, D, N = w_mod.shape
    bp = c_all.shape[0]
    tn = 1536
    return pl.pallas_call(
        _mod_kernel,
        grid=(L, N // tn),
        in_specs=[pl.BlockSpec((bp, D), lambda l, j: (0, 0)),
                  pl.BlockSpec((1, D, tn), lambda l, j: (l, 0, j)),
                  pl.BlockSpec((1, 1, tn), lambda l, j: (l, 0, j))],
        out_specs=pl.BlockSpec((1, bp, tn), lambda l, j: (l, 0, j)),
        out_shape=jax.ShapeDtypeStruct((L, bp, N), F32),
        compiler_params=_cparams(("arbitrary", "arbitrary")),
        name="modulation",
    )(c_all, w_mod, b_mod.reshape(L, 1, N))


def _modulated_norm(x, m_ref, g_ref, sub):
    shift = m_ref[0, 3 * sub:3 * sub + 1, :]
    scale = m_ref[0, 3 * sub + 1:3 * sub + 2, :]
    ms = jnp.mean(x * x, axis=-1, keepdims=True)
    y = x * lax.rsqrt(ms + EPS) * g_ref[...]
    return y * (1.0 + scale) + shift


def _ffn_kernel(x_ref, m_ref, g_ref, wgu_ref, wd_ref, o_ref, h_scr, acc_scr, *, sub, nch):
    x = x_ref[0]
    h_scr[...] = _modulated_norm(x, m_ref, g_ref, sub).astype(BF16)
    acc_scr[...] = jnp.zeros_like(acc_scr)

    def body(c, carry):
        gu = jnp.dot(h_scr[...], wgu_ref[c], preferred_element_type=F32)
        g = gu[:, :FF_CHUNK]
        u = gu[:, FF_CHUNK:]
        a = (g * jax.nn.sigmoid(g) * u).astype(BF16)
        acc_scr[...] += jnp.dot(a, wd_ref[c], preferred_element_type=F32)
        return carry

    lax.fori_loop(0, nch, body, 0)
    gate = m_ref[0, 3 * sub + 2:3 * sub + 3, :]
    o_ref[0] = x + 0.5 * gate * acc_scr[...]


def _ffn(x, m, g, wgu, wd, sub):
    B, S, D = x.shape
    ts, _, _ = _tiles(S)
    nch = wgu.shape[0]
    const3 = lambda b, t: (0, 0, 0)
    return pl.pallas_call(
        functools.partial(_ffn_kernel, sub=sub, nch=nch),
        grid=(B, S // ts),
        in_specs=[pl.BlockSpec((1, ts, D), lambda b, t: (b, t, 0)),
                  pl.BlockSpec((1, N_MOD, D), lambda b, t: (b, 0, 0)),
                  pl.BlockSpec((1, D), lambda b, t: (0, 0)),
                  pl.BlockSpec(wgu.shape, const3, pipeline_mode=pl.Buffered(1)),
                  pl.BlockSpec(wd.shape, const3, pipeline_mode=pl.Buffered(1))],
        out_specs=pl.BlockSpec((1, ts, D), lambda b, t: (b, t, 0)),
        out_shape=jax.ShapeDtypeStruct((B, S, D), F32),
        scratch_shapes=[pltpu.VMEM((ts, D), BF16), pltpu.VMEM((ts, D), F32)],
        compiler_params=_cparams(("arbitrary", "arbitrary")),
        name="ffn",
    )(x, m, g, wgu, wd)


def _rms_rows(v, gain):
    inv = lax.rsqrt(jnp.mean(v * v, axis=0, keepdims=True) + EPS)
    return v * inv * gain


def _rope_rows(t1, t2, cos, sin):
    return t1 * cos - t2 * sin, t1 * sin + t2 * cos


def _ones_row_tile(tk):
    rid = lax.broadcasted_iota(jnp.int32, (V_ROWS - HEAD_V, tk), 0)
    return jnp.where(rid == 0, 1.0, 0.0).astype(BF16)


def _inproj_kernel(x_ref, m_ref, g_ref, wnat_ref, wT_ref, wuqT_ref, wukvT_ref,
                   gdq_ref, gdk_ref, gqa_ref, gkva_ref, gmq_ref, gmk_ref, cos_ref, sin_ref,
                   ab_ref, u_ref, fu_ref, qd_ref, kd_ref, vd_ref, qm_ref, km_ref, vm_ref,
                   *, ts, tk, slopes):
    t = pl.program_id(1)
    hb = _modulated_norm(x_ref[0], m_ref, g_ref, 1).astype(BF16)

    nat = jnp.dot(hb, wnat_ref[...], preferred_element_type=F32)
    ab_ref[0] = nat[:, 0:GROUP]
    u_ref[0] = nat[:, GROUP:2 * GROUP] * nat[:, 2 * GROUP:3 * GROUP]
    fu_ref[0] = nat[:, 3 * GROUP:4 * GROUP].astype(BF16)

    pT = lax.dot_general(wT_ref[...], hb, (((1,), (1,)), ((), ())),
                         preferred_element_type=F32)

    ones_tile = _ones_row_tile(tk)
    pos = t * ts + lax.broadcasted_iota(jnp.int32, (1, ts), 1)
    c = pos & (tk - 1)
    c_hi = ((c >> 8) << 8).astype(F32)
    c_lo = (c & 255).astype(F32)
    rid = lax.broadcasted_iota(jnp.int32, (AUG_ROWS, ts), 0)
    pad_d = jnp.zeros((KPAD - 2 * DIFF_QK - AUG_ROWS, ts), F32)
    pad_m = jnp.zeros((KPAD - MLA_QK, ts), F32)
    q_scale = DIFF_QK ** -0.5
    gdq = gdq_ref[...] * q_scale
    gdk = gdk_ref[...]

    for h in range(HEADS):
        base = 2 * DIFF_QK * h
        ks = []
        for j in range(2):
            r0 = base + DIFF_QK * j
            qd_ref[0, h, DIFF_QK * j:DIFF_QK * (j + 1), :] = _rms_rows(pT[r0:r0 + DIFF_QK], gdq).astype(BF16)
            ks.append(_rms_rows(pT[GROUP + r0:GROUP + r0 + DIFF_QK], gdk))
        aug = jnp.where(rid < 2, 1.0,
                        jnp.where(rid == 2, slopes[h] * c_hi,
                                  jnp.where(rid == 3, slopes[h] * c_lo, 0.0)))
        kext = jnp.concatenate(ks + [aug, pad_d], axis=0)
        kd_ref[0, h] = kext.T.astype(BF16)
        v = pT[2 * GROUP + HEAD_V * h:2 * GROUP + HEAD_V * (h + 1)].astype(BF16)
        for kb in range(ts // tk):
            vd_ref[0, h, kb, 0:HEAD_V, :] = v[:, kb * tk:(kb + 1) * tk]
            vd_ref[0, h, kb, HEAD_V:V_ROWS, :] = ones_tile

    o_cq = 3 * GROUP
    o_ckv = o_cq + MLA_Q_RANK
    o_kpe = o_ckv + MLA_KV_RANK
    cqn = _rms_rows(pT[o_cq:o_ckv], gqa_ref[...]).astype(BF16)
    qmT = jnp.dot(wuqT_ref[...], cqn, preferred_element_type=F32)
    ckvn = _rms_rows(pT[o_ckv:o_kpe], gkva_ref[...]).astype(BF16)
    kvT = jnp.dot(wukvT_ref[...], ckvn, preferred_element_type=F32)
    kpe = pT[o_kpe:o_kpe + MLA_ROPE]
    cos = cos_ref[...]
    sin = sin_ref[...]
    m_scale = MLA_QK ** -0.5
    gmq = gmq_ref[...] * m_scale
    gmk = gmk_ref[...]
    half = MLA_ROPE // 2
    for h in range(HEADS):
        qn = _rms_rows(qmT[MLA_QK * h:MLA_QK * (h + 1)], gmq)
        q1, q2 = _rope_rows(qn[MLA_NOPE:MLA_NOPE + half], qn[MLA_NOPE + half:MLA_QK], cos, sin)
        qm_ref[0, h, 0:MLA_NOPE, :] = qn[0:MLA_NOPE].astype(BF16)
        qm_ref[0, h, MLA_NOPE:MLA_NOPE + half, :] = q1.astype(BF16)
        qm_ref[0, h, MLA_NOPE + half:MLA_QK, :] = q2.astype(BF16)
        kv0 = (MLA_NOPE + HEAD_V) * h
        kn = _rms_rows(jnp.concatenate([kvT[kv0:kv0 + MLA_NOPE], kpe], axis=0), gmk)
        k1, k2 = _rope_rows(kn[MLA_NOPE:MLA_NOPE + half], kn[MLA_NOPE + half:MLA_QK], cos, sin)
        kext = jnp.concatenate([kn[0:MLA_NOPE], k1, k2, pad_m], axis=0)
        km_ref[0, h] = kext.T.astype(BF16)
        v = kvT[kv0 + MLA_NOPE:kv0 + MLA_NOPE + HEAD_V].astype(BF16)
        for kb in range(ts // tk):
            vm_ref[0, h, kb, 0:HEAD_V, :] = v[:, kb * tk:(kb + 1) * tk]
            vm_ref[0, h, kb, HEAD_V:V_ROWS, :] = ones_tile


def _inproj(x, m, g, w, rope_cos, rope_sin, slopes):
    B, S, D = x.shape
    ts, _, tk = _tiles(S)
    nkb = S // tk
    c2 = lambda b, t: (0, 0)
    tok = lambda b, t: (b, t, 0)
    head_T = lambda b, t: (b, 0, 0, t)
    head_N = lambda b, t: (b, 0, t, 0)
    head_V = lambda b, t: (b, 0, t, 0, 0)
    small = [w['gdq'], w['gdk'], w['gqa'], w['gkva'], w['gmq'], w['gmk']]
    in_specs = ([pl.BlockSpec((1, ts, D), tok),
                 pl.BlockSpec((1, N_MOD, D), lambda b, t: (b, 0, 0)),
                 pl.BlockSpec((1, D), c2),
                 pl.BlockSpec(w['w_nat'].shape, c2, pipeline_mode=pl.Buffered(1)),
                 pl.BlockSpec(w['w_T'].shape, c2, pipeline_mode=pl.Buffered(1)),
                 pl.BlockSpec(w['wuqT'].shape, c2, pipeline_mode=pl.Buffered(1)),
                 pl.BlockSpec(w['wukvT'].shape, c2, pipeline_mode=pl.Buffered(1))]
                + [pl.BlockSpec(a.shape, c2) for a in small]
                + [pl.BlockSpec((MLA_ROPE // 2, ts), lambda b, t: (0, t))] * 2)
    out_shape = [jax.ShapeDtypeStruct((B, S, GROUP), F32),
                 jax.ShapeDtypeStruct((B, S, GROUP), F32),
                 jax.ShapeDtypeStruct((B, S, GROUP), BF16),
                 jax.ShapeDtypeStruct((B, HEADS, 2 * DIFF_QK, S), BF16),
                 jax.ShapeDtypeStruct((B, HEADS, S, KPAD), BF16),
                 jax.ShapeDtypeStruct((B, HEADS, nkb, V_ROWS, tk), BF16),
                 jax.ShapeDtypeStruct((B, HEADS, MLA_QK, S), BF16),
                 jax.ShapeDtypeStruct((B, HEADS, S, KPAD), BF16),
                 jax.ShapeDtypeStruct((B, HEADS, nkb, V_ROWS, tk), BF16)]
    out_specs = [pl.BlockSpec((1, ts, GROUP), tok),
                 pl.BlockSpec((1, ts, GROUP), tok),
                 pl.BlockSpec((1, ts, GROUP), tok),
                 pl.BlockSpec((1, HEADS, 2 * DIFF_QK, ts), head_T),
                 pl.BlockSpec((1, HEADS, ts, KPAD), head_N),
                 pl.BlockSpec((1, HEADS, ts // tk, V_ROWS, tk), head_V),
                 pl.BlockSpec((1, HEADS, MLA_QK, ts), head_T),
                 pl.BlockSpec((1, HEADS, ts, KPAD), head_N),
                 pl.BlockSpec((1, HEADS, ts // tk, V_ROWS, tk), head_V)]
    return pl.pallas_call(
        functools.partial(_inproj_kernel, ts=ts, tk=tk, slopes=slopes),
        grid=(B, S // ts),
        in_specs=in_specs,
        out_specs=out_specs,
        out_shape=out_shape,
        compiler_params=_cparams(("arbitrary", "arbitrary")),
        name="inproj",
    )(x, m, g, w['w_nat'], w['w_T'], w['wuqT'], w['wukvT'], *small, rope_cos, rope_sin)


def _attn_kernel(slopes_ref, qT_ref, k_ref, vT_ref, gsub_ref, lv_ref, o_ref,
                 rhs_scr, acc_scr, m_scr, *, n_maps, alibi, tq, tk, nkb, lambda_init):
    h = pl.program_id(1)
    q0 = pl.program_id(2) * tq
    q = qT_ref[0, 0]

    if alibi:
        slope = slopes_ref[h]
        lane = lax.broadcasted_iota(jnp.int32, (1, tq), 1)
        r_hi = ((lane >> 8) << 8).astype(F32)
        r_lo = (lane & 255).astype(F32)
        rid = lax.broadcasted_iota(jnp.int32, (AUG_ROWS, tq), 0)
        aug_below = jnp.where(rid == 0, -slope * r_hi,
                              jnp.where(rid == 1, -slope * r_lo,
                                        jnp.where(rid < 4, 1.0, 0.0)))
        zq = jnp.zeros((DIFF_QK, tq), BF16)
        zp = jnp.zeros((KPAD - 2 * DIFF_QK - AUG_ROWS, tq), BF16)
        for v, aug in enumerate((aug_below, -aug_below, jnp.zeros_like(aug_below))):
            a = aug.astype(BF16)
            rhs_scr[2 * v + 0] = jnp.concatenate([q[0:DIFF_QK], zq, a, zp], axis=0)
            rhs_scr[2 * v + 1] = jnp.concatenate([zq, q[DIFF_QK:2 * DIFF_QK], a, zp], axis=0)
    else:
        rhs_scr[0] = jnp.concatenate([q, jnp.zeros((KPAD - q.shape[0], tq), BF16)], axis=0)

    m_scr[...] = jnp.full_like(m_scr, NEG_BIG)
    acc_scr[...] = jnp.zeros_like(acc_scr)

    def step(kb, variant, diag):
        k0 = kb * tk
        kblk = k_ref[0, 0, pl.ds(pl.multiple_of(k0, tk), tk), :]
        vblk = vT_ref[0, 0, kb]
        cb = 0.0
        if alibi and not diag:
            cb = -slope * jnp.abs(q0 - k0).astype(F32)
        for j in range(n_maps):
            s = jnp.dot(kblk, rhs_scr[n_maps * variant + j], preferred_element_type=F32)
            if diag:
                qi = q0 + lax.broadcasted_iota(jnp.int32, (tk, tq), 1)
                kj = k0 + lax.broadcasted_iota(jnp.int32, (tk, tq), 0)
                s = s - slope * jnp.abs(qi - kj).astype(F32)
            m_old = m_scr[j]
            m_new = jnp.maximum(m_old, jnp.max(s, axis=0, keepdims=True) + cb)
            alpha = jnp.exp(m_old - m_new)
            p = jnp.exp(s + (cb - m_new)).astype(BF16)
            acc_scr[j] = alpha * acc_scr[j] + jnp.dot(vblk, p, preferred_element_type=F32)
            m_scr[j] = m_new

    def loop(lo, hi, variant, diag):
        def body(kb, carry):
            step(kb, variant, diag)
            return carry
        lax.fori_loop(lo, hi, body, 0)

    if alibi:
        n_below = q0 // tk
        n_diag = max(1, tq // tk)
        loop(0, n_below, 0, False)
        loop(n_below, n_below + n_diag, 2, True)
        loop(n_below + n_diag, nkb, 1, False)
    else:
        loop(0, nkb, 0, False)

    a0 = acc_scr[0]
    o = a0[0:HEAD_V] / a0[HEAD_V:HEAD_V + 1]
    if n_maps == 2:
        a1 = acc_scr[1]
        lv = lv_ref[...]
        lam = (jnp.exp(jnp.sum(lv[0:1] * lv[1:2], axis=1, keepdims=True))
               - jnp.exp(jnp.sum(lv[2:3] * lv[3:4], axis=1, keepdims=True)) + lambda_init)
        o = o - lam * (a1[0:HEAD_V] / a1[HEAD_V:HEAD_V + 1])
        o = _rms_rows(o, gsub_ref[...]) * (1.0 - lambda_init)
    o_ref[0] = o.astype(BF16)


def _attention(qT, k, vT, slopes, gsub, lv, *, n_maps, alibi, lambda_init):
    B, H, kq, S = qT.shape
    _, tq, tk = _tiles(S)
    nkb = S // tk
    assert tk % tq == 0 or tq % tk == 0
    kern = functools.partial(_attn_kernel, n_maps=n_maps, alibi=alibi, tq=tq, tk=tk, nkb=nkb,
                             lambda_init=lambda_init)
    n_rhs = 3 * n_maps if alibi else n_maps
    return pl.pallas_call(
        kern,
        grid=(B, H, S // tq),
        in_specs=[pl.BlockSpec(memory_space=pltpu.SMEM),
                  pl.BlockSpec((1, 1, kq, tq), lambda b, h, i: (b, h, 0, i)),
                  pl.BlockSpec((1, 1, S, KPAD), lambda b, h, i: (b, h, 0, 0)),
                  pl.BlockSpec((1, 1, nkb, V_ROWS, tk), lambda b, h, i: (b, h, 0, 0, 0)),
                  pl.BlockSpec(gsub.shape, lambda b, h, i: (0, 0)),
                  pl.BlockSpec(lv.shape, lambda b, h, i: (0, 0))],
        out_specs=pl.BlockSpec((1, HEAD_V, tq), lambda b, h, i: (b, h, i)),
        out_shape=jax.ShapeDtypeStruct((B, H * HEAD_V, S), BF16),
        scratch_shapes=[pltpu.VMEM((n_rhs, KPAD, tq), BF16),
                        pltpu.VMEM((n_maps, V_ROWS, tq), F32),
                        pltpu.VMEM((n_maps, 1, tq), F32)],
        compiler_params=_cparams(("arbitrary", "arbitrary", "arbitrary")),
        name="attn_diff" if alibi else "attn_mla",
    )(slopes, qT, k, vT, gsub, lv)


def _f1_kernel(t_ref, x_ref, o_ref):
    o_ref[0] = jnp.dot(t_ref[...], x_ref[0], preferred_element_type=F32).astype(BF16)


def _f2_kernel(g_ref, a_ref, cs_ref, o_ref, *, tk1, n2, ch, norm):
    for j in range(tk1):
        z = jnp.concatenate([a_ref[0, 0, j], a_ref[0, 1, j]], axis=0)
        x = jnp.dot(g_ref[j], z, preferred_element_type=F32)
        xr = x[:n2].astype(BF16)
        xi = x[n2:].astype(BF16)
        y = (jnp.dot(xr, cs_ref[0:ch], preferred_element_type=F32)
             + jnp.dot(xi, cs_ref[ch:2 * ch], preferred_element_type=F32))
        o_ref[0, :, j * ch:(j + 1) * ch] = (y * norm).astype(BF16)


def _fourier_tables(S):
    n1, n2 = _fourier_split(S)
    a = jnp.arange(n1, dtype=jnp.int32)
    ang1 = (2.0 * math.pi / n1) * ((a[:, None] * a[None, :]) % n1).astype(F32)
    t1 = jnp.concatenate([jnp.cos(ang1), -jnp.sin(ang1)], axis=0).astype(BF16)
    k = a[:, None, None] + n1 * jnp.arange(n2, dtype=jnp.int32)[None, :, None]
    b = jnp.arange(n2, dtype=jnp.int32)[None, None, :]
    ang = (2.0 * math.pi / S) * ((k * b) % S).astype(F32)
    cg, sg = jnp.cos(ang), jnp.sin(ang)
    g = jnp.concatenate([jnp.concatenate([cg, sg], axis=2),
                         jnp.concatenate([-sg, cg], axis=2)], axis=1).astype(BF16)
    gw = GROUP // FOURIER_GROUPS
    c = jnp.arange(GROUP, dtype=jnp.int32)
    same = (c[:, None] // gw) == (c[None, :] // gw)
    angc = (2.0 * math.pi / gw) * (((c[:, None] % gw) * (c[None, :] % gw)) % gw).astype(F32)
    cs = jnp.concatenate([jnp.where(same, jnp.cos(angc), 0.0),
                          jnp.where(same, jnp.sin(angc), 0.0)], axis=0).astype(BF16)
    return t1, g, cs


def _fourier(fu, tables):
    B, S, C = fu.shape
    n1, n2 = _fourier_split(S)
    t1, g, cs = tables
    tn = min(4096, n2 * C)
    a = pl.pallas_call(
        _f1_kernel,
        grid=(B, n2 * C // tn),
        in_specs=[pl.BlockSpec(t1.shape, lambda b, j: (0, 0)),
                  pl.BlockSpec((1, n1, tn), lambda b, j: (b, 0, j))],
        out_specs=pl.BlockSpec((1, 2 * n1, tn), lambda b, j: (b, 0, j)),
        out_shape=jax.ShapeDtypeStruct((B, 2 * n1, n2 * C), BF16),
        compiler_params=_cparams(("arbitrary", "arbitrary")),
        name="fourier_stage1",
    )(t1, fu.reshape(B, n1, n2 * C))
    tk1 = 8
    norm = 1.0 / math.sqrt(S * (GROUP // FOURIER_GROUPS))
    y = pl.pallas_call(
        functools.partial(_f2_kernel, tk1=tk1, n2=n2, ch=C, norm=norm),
        grid=(B, n1 // tk1),
        in_specs=[pl.BlockSpec((tk1, 2 * n2, 2 * n2), lambda b, j: (j, 0, 0)),
                  pl.BlockSpec((1, 2, tk1, n2, C), lambda b, j: (b, 0, j, 0, 0)),
                  pl.BlockSpec(cs.shape, lambda b, j: (0, 0))],
        out_specs=pl.BlockSpec((1, n2, tk1 * C), lambda b, j: (b, 0, j)),
        out_shape=jax.ShapeDtypeStruct((B, n2, n1 * C), BF16),
        compiler_params=_cparams(("arbitrary", "arbitrary")),
        name="fourier_stage2",
    )(g, a.reshape(B, 2, n1, n2, C), cs)
    return y.reshape(B, S, C)


def _outproj_kernel(x_ref, m_ref, ab_ref, u_ref, up_ref, un_ref, cw_ref, yb_ref, yc_ref, yd_ref,
                    w_ref, o_ref, *, ts, nt):
    t = pl.program_id(1)
    u = u_ref[0]
    prev_row = jnp.where(t == 0, 0.0, up_ref[0, 7:8, :])
    next_row = jnp.where(t == nt - 1, 0.0, un_ref[0, 0:1, :])
    rid = lax.broadcasted_iota(jnp.int32, u.shape, 0)
    u_m1 = jnp.where(rid == 0, prev_row, pltpu.roll(u, 1, 0))
    u_p1 = jnp.where(rid == ts - 1, next_row, pltpu.roll(u, ts - 1, 0))
    conv = u_m1 * cw_ref[0:1, :] + u * cw_ref[1:2, :] + u_p1 * cw_ref[2:3, :]
    ya = (ab_ref[0] * conv).astype(BF16)
    tn_dims = (((0,), (0,)), ((), ()))
    acc = jnp.dot(ya, w_ref[0:GROUP], preferred_element_type=F32)
    acc += lax.dot_general(yb_ref[0], w_ref[GROUP:2 * GROUP], tn_dims, preferred_element_type=F32)
    acc += lax.dot_general(yc_ref[0], w_ref[2 * GROUP:3 * GROUP], tn_dims, preferred_element_type=F32)
    acc += jnp.dot(yd_ref[0], w_ref[3 * GROUP:4 * GROUP], preferred_element_type=F32)
    o_ref[0] = x_ref[0] + m_ref[0, 5:6, :] * acc


def _outproj(x, m, ab, u, conv_w, ybT, ycT, yd, w_out):
    B, S, D = x.shape
    ts, _, _ = _tiles(S)
    nt = S // ts
    r8 = ts // 8
    tok = lambda b, t: (b, t, 0)
    return pl.pallas_call(
        functools.partial(_outproj_kernel, ts=ts, nt=nt),
        grid=(B, nt),
        in_specs=[pl.BlockSpec((1, ts, D), tok),
                  pl.BlockSpec((1, N_MOD, D), lambda b, t: (b, 0, 0)),
                  pl.BlockSpec((1, ts, GROUP), tok),
                  pl.BlockSpec((1, ts, GROUP), tok),
                  pl.BlockSpec((1, 8, GROUP), lambda b, t: (b, jnp.maximum(t * r8 - 1, 0), 0)),
                  pl.BlockSpec((1, 8, GROUP), lambda b, t: (b, jnp.minimum((t + 1) * r8, S // 8 - 1), 0)),
                  pl.BlockSpec(conv_w.shape, lambda b, t: (0, 0)),
                  pl.BlockSpec((1, GROUP, ts), lambda b, t: (b, 0, t)),
                  pl.BlockSpec((1, GROUP, ts), lambda b, t: (b, 0, t)),
                  pl.BlockSpec((1, ts, GROUP), tok),
                  pl.BlockSpec(w_out.shape, lambda b, t: (0, 0), pipeline_mode=pl.Buffered(1))],
        out_specs=pl.BlockSpec((1, ts, D), tok),
        out_shape=jax.ShapeDtypeStruct((B, S, D), F32),
        compiler_params=_cparams(("arbitrary", "arbitrary")),
        name="outproj",
    )(x, m, ab, u, u, u, conv_w, ybT, ycT, yd, w_out)


def _rope_tables(S):
    inv = ROPE_THETA ** (-jnp.arange(0, MLA_ROPE, 2, dtype=F32) / MLA_ROPE)
    ang = inv[:, None] * jnp.arange(S, dtype=F32)[None, :]
    return jnp.cos(ang), jnp.sin(ang)


def _alibi_slopes():
    return tuple(float(2.0 ** (-8.0 * (i + 1) / HEADS)) for i in range(HEADS))


def _prep_layer(p, l):
    col = lambda v: v.astype(F32).reshape(-1, 1)
    w_in = p['w_in'][l].astype(BF16)
    nch = p['ffn1_w_gu'].shape[-1] // 2 // FF_CHUNK

    def ffn_w(wgu, wd):
        D = wgu.shape[0]
        g = wgu[:, :nch * FF_CHUNK].reshape(D, nch, FF_CHUNK)
        u = wgu[:, nch * FF_CHUNK:].reshape(D, nch, FF_CHUNK)
        wgu_c = jnp.transpose(jnp.concatenate([g, u], axis=2), (1, 0, 2)).astype(BF16)
        return wgu_c, wd.reshape(nch, FF_CHUNK, D).astype(BF16)

    return dict(
        ffn1=ffn_w(p['ffn1_w_gu'][l], p['ffn1_w_down'][l]),
        ffn2=ffn_w(p['ffn2_w_gu'][l], p['ffn2_w_down'][l]),
        w_nat=jnp.concatenate([w_in[:, :NAT_LO], w_in[:, ATT_HI:]], axis=1),
        w_T=w_in[:, NAT_LO:ATT_HI].T,
        wuqT=p['mla_w_uq'][l].astype(BF16).T,
        wukvT=p['mla_w_ukv'][l].astype(BF16).T,
        gdq=col(p['diff_q_g'][l]), gdk=col(p['diff_k_g'][l]),
        gqa=col(p['mla_q_a_g'][l]), gkva=col(p['mla_kv_a_g'][l]),
        gmq=col(p['mla_q_g'][l]), gmk=col(p['mla_k_g'][l]),
        gsub=col(p['diff_subln_g'][l]),
        lv=p['diff_lambda'][l].astype(F32),
        conv_w=p['conv_w'][l].astype(F32),
        w_out=p['w_out'][l].astype(BF16),
        norm_g=p['norm_g'][l].astype(F32),
    )


def _layer(x, m, w, l, consts):
    rope_cos, rope_sin, ftables, slopes = consts
    ng = w['norm_g']
    x = _ffn(x, m, ng[0:1], *w['ffn1'], 0)
    ab, u, fu, qd, kd, vd, qm, km, vm = _inproj(x, m, ng[1:2], w, rope_cos, rope_sin, slopes)
    slopes_arr = jnp.asarray(slopes, F32)
    lambda_init = 0.8 - 0.6 * math.exp(-0.3 * l)
    ybT = _attention(qd, kd, vd, slopes_arr, w['gsub'], w['lv'],
                     n_maps=2, alibi=True, lambda_init=lambda_init)
    ycT = _attention(qm, km, vm, slopes_arr, w['gsub'], w['lv'],
                     n_maps=1, alibi=False, lambda_init=0.0)
    yd = _fourier(fu, ftables)
    x = _outproj(x, m, ab, u, w['conv_w'], ybT, ycT, yd, w['w_out'])
    return _ffn(x, m, ng[2:3], *w['ffn2'], 2)


def _trunk(groups, c_all, p):
    depth = p['w_mod'].shape[0]
    mods = _modulation(c_all, p['w_mod'], p['b_mod'])
    outs = []
    consts = []
    for x, _ in groups:
        S = x.shape[1]
        consts.append(_rope_tables(S) + (_fourier_tables(S), _alibi_slopes()))
    xs = [x for x, _ in groups]
    for l in range(depth):
        w = _prep_layer(p, l)
        for gi, (x0, off) in enumerate(groups):
            B, _, D = x0.shape
            m = mods[l, off:off + B].reshape(B, N_MOD, D)
            xs[gi] = _layer(xs[gi], m, w, l, consts[gi])
    return xs


def kernel(x_prompt, x_sample, c_prompt, c_sample, w_mod, b_mod, norm_g, ffn1_w_gu, ffn1_w_down,
           w_in, conv_w, diff_lambda, diff_q_g, diff_k_g, diff_subln_g, mla_q_a_g, mla_w_uq,
           mla_kv_a_g, mla_w_ukv, mla_q_g, mla_k_g, w_out, ffn2_w_gu, ffn2_w_down):
    p = dict(w_mod=w_mod, b_mod=b_mod, norm_g=norm_g, ffn1_w_gu=ffn1_w_gu, ffn1_w_down=ffn1_w_down,
             w_in=w_in, conv_w=conv_w, diff_lambda=diff_lambda, diff_q_g=diff_q_g, diff_k_g=diff_k_g,
             diff_subln_g=diff_subln_g, mla_q_a_g=mla_q_a_g, mla_w_uq=mla_w_uq, mla_kv_a_g=mla_kv_a_g,
             mla_w_ukv=mla_w_ukv, mla_q_g=mla_q_g, mla_k_g=mla_k_g, w_out=w_out,
             ffn2_w_gu=ffn2_w_gu, ffn2_w_down=ffn2_w_down)
    nb = c_prompt.shape[0] + c_sample.shape[0]
    pad = (-nb) % 8
    c_all = jnp.concatenate([c_prompt, c_sample, jnp.zeros((pad, c_prompt.shape[1]), c_prompt.dtype)])
    y_prompt, y_sample = _trunk([(x_prompt, 0), (x_sample, c_prompt.shape[0])], c_all, p)
    return (y_prompt, y_sample)
```

```python
import functools
import math

import numpy as np
import jax
import jax.numpy as jnp
from jax import lax
from jax.experimental import pallas as pl
from jax.experimental.pallas import tpu as pltpu

F32 = jnp.float32
BF16 = jnp.bfloat16

HEADS = 4
GROUP = 256
DIFF_QK = 32
MLA_NOPE = 64
MLA_ROPE = 32
MLA_QK = MLA_NOPE + MLA_ROPE
MLA_Q_RANK = 256
MLA_KV_RANK = 128
HEAD_V = 64
FOURIER_GROUPS = 4
ROPE_THETA = 10000.0
EPS = 1e-6
N_MOD = 9
ATT_ROWS = 3 * GROUP + MLA_Q_RANK + MLA_KV_RANK + MLA_ROPE
NAT_LO = 3 * GROUP
ATT_HI = NAT_LO + ATT_ROWS

LANES = 128
BF16_SUBLANES = 16
KPAD = 128
V_ROWS = HEAD_V + BF16_SUBLANES
AUG_ROWS = BF16_SUBLANES
FF_CHUNK = 256
ATTN_UNROLL = 4
VMEM_LIMIT = 52 * 1024 * 1024
NEG_BIG = -1e30
LOG2E = 1.4426950408889634
FAST_SOFTMAX_BOUND = 25.0
BOUND_MARGIN = 1.02


def _mod3(i):
    three = lambda c: jnp.where(c, 3, 0)
    return i - three(i >= 3) - three(i >= 6) - three(i >= 9)


def _bf16_parts(x):
    parts, r = [], np.float32(x)
    for _ in range(3):
        p = np.float32(np.asarray(r, dtype=jnp.bfloat16))
        parts.append(float(p))
        r = np.float32(r - p)
    return tuple(parts)


def _cparams(sem):
    return pltpu.CompilerParams(dimension_semantics=sem, vmem_limit_bytes=VMEM_LIMIT)


def _tiles(S):
    ts = min(512, S)
    tq = min(512, S)
    tk = min(512, S)
    return ts, tq, tk


def _fourier_split(S):
    n1 = 1 << ((int(math.log2(S)) + 1) // 2)
    return n1, S // n1


def _mod_kernel(c_ref, w_ref, b_ref, o_ref):
    c = c_ref[...]
    s = (c * jax.nn.sigmoid(c)).astype(BF16)
    o_ref[0] = jnp.dot(s, w_ref[0].astype(BF16), preferred_element_type=F32) + b_ref[0]


def _modulation(c_all, w_mod, b_mod):
    L, D, N = w_mod.shape
    bp = c_all.shape[0]
    tn = 1536
    return pl.pallas_call(
        _mod_kernel,
        grid=(L, N // tn),
        in_specs=[pl.BlockSpec((bp, D), lambda l, j: (0, 0)),
                  pl.BlockSpec((1, D, tn), lambda l, j: (l, 0, j)),
                  pl.BlockSpec((1, 1, tn), lambda l, j: (l, 0, j))],
        out_specs=pl.BlockSpec((1, bp, tn), lambda l, j: (l, 0, j)),
        out_shape=jax.ShapeDtypeStruct((L, bp, N), F32),
        compiler_params=_cparams(("arbitrary", "arbitrary")),
        name="modulation",
    )(c_all, w_mod, b_mod.reshape(L, 1, N))


def _modulated_norm(x, m_ref, g_ref, sub):
    shift = m_ref[0, 3 * sub:3 * sub + 1, :]
    scale = m_ref[0, 3 * sub + 1:3 * sub + 2, :]
    ms = jnp.mean(x * x, axis=-1, keepdims=True)
    y = x * lax.rsqrt(ms + EPS) * g_ref[...]
    return y * (1.0 + scale) + shift


def _ffn_kernel(x_ref, m_ref, g_ref, wgu_ref, wd_ref, o_ref, h_scr, acc_scr, *, sub, nch):
    x = x_ref[0]
    h_scr[...] = _modulated_norm(x, m_ref, g_ref, sub).astype(BF16)
    acc_scr[...] = jnp.zeros_like(acc_scr)

    def body(c, carry):
        gu = jnp.dot(h_scr[...], wgu_ref[c], preferred_element_type=F32)
        g = gu[:, :FF_CHUNK]
        u = gu[:, FF_CHUNK:]
        a = (g * jax.nn.sigmoid(g) * u).astype(BF16)
        acc_scr[...] += jnp.dot(a, wd_ref[c], preferred_element_type=F32)
        return carry

    lax.fori_loop(0, nch, body, 0)
    gate = m_ref[0, 3 * sub + 2:3 * sub + 3, :]
    o_ref[0] = x + 0.5 * gate * acc_scr[...]


def _ffn(x, m, g, wgu, wd, sub):
    B, S, D = x.shape
    ts, _, _ = _tiles(S)
    nch = wgu.shape[0]
    const3 = lambda b, t: (0, 0, 0)
    return pl.pallas_call(
        functools.partial(_ffn_kernel, sub=sub, nch=nch),
        grid=(B, S // ts),
        in_specs=[pl.BlockSpec((1, ts, D), lambda b, t: (b, t, 0)),
                  pl.BlockSpec((1, N_MOD, D), lambda b, t: (b, 0, 0)),
                  pl.BlockSpec((1, D), lambda b, t: (0, 0)),
                  pl.BlockSpec(wgu.shape, const3, pipeline_mode=pl.Buffered(1)),
                  pl.BlockSpec(wd.shape, const3, pipeline_mode=pl.Buffered(1))],
        out_specs=pl.BlockSpec((1, ts, D), lambda b, t: (b, t, 0)),
        out_shape=jax.ShapeDtypeStruct((B, S, D), F32),
        scratch_shapes=[pltpu.VMEM((ts, D), BF16), pltpu.VMEM((ts, D), F32)],
        compiler_params=_cparams(("arbitrary", "arbitrary")),
        name="ffn",
    )(x, m, g, wgu, wd)


def _rms_rows(v, gain):
    inv = lax.rsqrt(jnp.mean(v * v, axis=0, keepdims=True) + EPS)
    return v * inv * gain


def _rope_rows(t1, t2, cos, sin):
    return t1 * cos - t2 * sin, t1 * sin + t2 * cos


def _ones_row_tile(tk):
    rid = lax.broadcasted_iota(jnp.int32, (V_ROWS - HEAD_V, tk), 0)
    return jnp.where(rid == 0, 1.0, 0.0).astype(BF16)


def _inproj_kernel(x_ref, m_ref, g_ref, wnat_ref, wT_ref, wuqT_ref, wukvT_ref,
                   gdq_ref, gdk_ref, gqa_ref, gkva_ref, gmq_ref, gmk_ref, cos_ref, sin_ref,
                   ab_ref, u_ref, fu_ref, qd_ref, kd_ref, vd_ref, qm_ref, km_ref, vm_ref,
                   *, ts, tk, slope_parts):
    t = pl.program_id(1)
    hb = _modulated_norm(x_ref[0], m_ref, g_ref, 1).astype(BF16)

    nat = jnp.dot(hb, wnat_ref[...], preferred_element_type=F32)
    ab_ref[0] = nat[:, 0:GROUP]
    u_ref[0] = nat[:, GROUP:2 * GROUP] * nat[:, 2 * GROUP:3 * GROUP]
    fu_ref[0] = nat[:, 3 * GROUP:4 * GROUP].astype(BF16)

    pT = lax.dot_general(wT_ref[...], hb, (((1,), (1,)), ((), ())),
                         preferred_element_type=F32)

    ones_tile = _ones_row_tile(tk)
    pos = t * ts + lax.broadcasted_iota(jnp.int32, (1, ts), 1)
    c = pos & (tk - 1)
    c_hi = ((c >> 8) << 8).astype(F32)
    c_lo = (c & 255).astype(F32)
    rid = lax.broadcasted_iota(jnp.int32, (AUG_ROWS, ts), 0)
    pad_d = jnp.zeros((KPAD - 2 * DIFF_QK - AUG_ROWS, ts), F32)
    pad_m = jnp.zeros((KPAD - MLA_QK, ts), F32)
    gdq = gdq_ref[...] * (DIFF_QK ** -0.5 * LOG2E)
    third = _mod3(rid)
    gdk = gdk_ref[...]

    for h in range(HEADS):
        base = 2 * DIFF_QK * h
        ks = []
        for j in range(2):
            r0 = base + DIFF_QK * j
            qd_ref[0, h, DIFF_QK * j:DIFF_QK * (j + 1), :] = _rms_rows(pT[r0:r0 + DIFF_QK], gdq).astype(BF16)
            ks.append(_rms_rows(pT[GROUP + r0:GROUP + r0 + DIFF_QK], gdk))
        s0, s1, s2 = slope_parts[h]
        part = jnp.where(third == 0, s0, jnp.where(third == 1, s1, s2))
        aug = jnp.where(rid < 6, part, jnp.where(rid < 9, c_hi, jnp.where(rid < 12, c_lo, 0.0)))
        kext = jnp.concatenate(ks + [aug, pad_d], axis=0)
        kd_ref[0, h] = kext.T.astype(BF16)
        v = pT[2 * GROUP + HEAD_V * h:2 * GROUP + HEAD_V * (h + 1)].astype(BF16)
        for kb in range(ts // tk):
            vd_ref[0, h, kb, 0:HEAD_V, :] = v[:, kb * tk:(kb + 1) * tk]
            vd_ref[0, h, kb, HEAD_V:V_ROWS, :] = ones_tile

    o_cq = 3 * GROUP
    o_ckv = o_cq + MLA_Q_RANK
    o_kpe = o_ckv + MLA_KV_RANK
    cqn = _rms_rows(pT[o_cq:o_ckv], gqa_ref[...]).astype(BF16)
    qmT = jnp.dot(wuqT_ref[...], cqn, preferred_element_type=F32)
    ckvn = _rms_rows(pT[o_ckv:o_kpe], gkva_ref[...]).astype(BF16)
    kvT = jnp.dot(wukvT_ref[...], ckvn, preferred_element_type=F32)
    kpe = pT[o_kpe:o_kpe + MLA_ROPE]
    cos = cos_ref[...]
    sin = sin_ref[...]
    gmq = gmq_ref[...] * (MLA_QK ** -0.5 * LOG2E)
    gmk = gmk_ref[...]
    half = MLA_ROPE // 2
    for h in range(HEADS):
        qn = _rms_rows(qmT[MLA_QK * h:MLA_QK * (h + 1)], gmq)
        q1, q2 = _rope_rows(qn[MLA_NOPE:MLA_NOPE + half], qn[MLA_NOPE + half:MLA_QK], cos, sin)
        qm_ref[0, h, 0:MLA_NOPE, :] = qn[0:MLA_NOPE].astype(BF16)
        qm_ref[0, h, MLA_NOPE:MLA_NOPE + half, :] = q1.astype(BF16)
        qm_ref[0, h, MLA_NOPE + half:MLA_QK, :] = q2.astype(BF16)
        kv0 = (MLA_NOPE + HEAD_V) * h
        kn = _rms_rows(jnp.concatenate([kvT[kv0:kv0 + MLA_NOPE], kpe], axis=0), gmk)
        k1, k2 = _rope_rows(kn[MLA_NOPE:MLA_NOPE + half], kn[MLA_NOPE + half:MLA_QK], cos, sin)
        kext = jnp.concatenate([kn[0:MLA_NOPE], k1, k2, pad_m], axis=0)
        km_ref[0, h] = kext.T.astype(BF16)
        v = kvT[kv0 + MLA_NOPE:kv0 + MLA_NOPE + HEAD_V].astype(BF16)
        for kb in range(ts // tk):
            vm_ref[0, h, kb, 0:HEAD_V, :] = v[:, kb * tk:(kb + 1) * tk]
            vm_ref[0, h, kb, HEAD_V:V_ROWS, :] = ones_tile


def _inproj(x, m, g, w, rope_cos, rope_sin, slope_parts):
    B, S, D = x.shape
    ts, _, tk = _tiles(S)
    nkb = S // tk
    c2 = lambda b, t: (0, 0)
    tok = lambda b, t: (b, t, 0)
    head_T = lambda b, t: (b, 0, 0, t)
    head_N = lambda b, t: (b, 0, t, 0)
    head_V = lambda b, t: (b, 0, t, 0, 0)
    small = [w['gdq'], w['gdk'], w['gqa'], w['gkva'], w['gmq'], w['gmk']]
    in_specs = ([pl.BlockSpec((1, ts, D), tok),
                 pl.BlockSpec((1, N_MOD, D), lambda b, t: (b, 0, 0)),
                 pl.BlockSpec((1, D), c2),
                 pl.BlockSpec(w['w_nat'].shape, c2, pipeline_mode=pl.Buffered(1)),
                 pl.BlockSpec(w['w_T'].shape, c2, pipeline_mode=pl.Buffered(1)),
                 pl.BlockSpec(w['wuqT'].shape, c2, pipeline_mode=pl.Buffered(1)),
                 pl.BlockSpec(w['wukvT'].shape, c2, pipeline_mode=pl.Buffered(1))]
                + [pl.BlockSpec(a.shape, c2) for a in small]
                + [pl.BlockSpec((MLA_ROPE // 2, ts), lambda b, t: (0, t))] * 2)
    out_shape = [jax.ShapeDtypeStruct((B, S, GROUP), F32),
                 jax.ShapeDtypeStruct((B, S, GROUP), F32),
                 jax.ShapeDtypeStruct((B, S, GROUP), BF16),
                 jax.ShapeDtypeStruct((B, HEADS, 2 * DIFF_QK, S), BF16),
                 jax.ShapeDtypeStruct((B, HEADS, S, KPAD), BF16),
                 jax.ShapeDtypeStruct((B, HEADS, nkb, V_ROWS, tk), BF16),
                 jax.ShapeDtypeStruct((B, HEADS, MLA_QK, S), BF16),
                 jax.ShapeDtypeStruct((B, HEADS, S, KPAD), BF16),
                 jax.ShapeDtypeStruct((B, HEADS, nkb, V_ROWS, tk), BF16)]
    out_specs = [pl.BlockSpec((1, ts, GROUP), tok),
                 pl.BlockSpec((1, ts, GROUP), tok),
                 pl.BlockSpec((1, ts, GROUP), tok),
                 pl.BlockSpec((1, HEADS, 2 * DIFF_QK, ts), head_T),
                 pl.BlockSpec((1, HEADS, ts, KPAD), head_N),
                 pl.BlockSpec((1, HEADS, ts // tk, V_ROWS, tk), head_V),
                 pl.BlockSpec((1, HEADS, MLA_QK, ts), head_T),
                 pl.BlockSpec((1, HEADS, ts, KPAD), head_N),
                 pl.BlockSpec((1, HEADS, ts // tk, V_ROWS, tk), head_V)]
    return pl.pallas_call(
        functools.partial(_inproj_kernel, ts=ts, tk=tk, slope_parts=slope_parts),
        grid=(B, S // ts),
        in_specs=in_specs,
        out_specs=out_specs,
        out_shape=out_shape,
        compiler_params=_cparams(("arbitrary", "arbitrary")),
        name="inproj",
    )(x, m, g, w['w_nat'], w['w_T'], w['wuqT'], w['wukvT'], *small, rope_cos, rope_sin)


def _attn_kernel(sc_ref, qT_ref, k_ref, vT_ref, gsub_ref, lv_ref, o_ref, rhs_scr,
                 *, n_maps, alibi, online, tq, tk, nkb, unroll, lambda_init):
    h = pl.program_id(1)
    q0 = pl.program_id(2) * tq
    q = qT_ref[0, 0]
    shift = sc_ref[h, 4]

    if alibi:
        slope2 = sc_ref[h, 0]
        lane = lax.broadcasted_iota(jnp.int32, (1, tq), 1)
        r_hi = ((lane >> 8) << 8).astype(F32)
        r_lo = (lane & 255).astype(F32)
        rid = lax.broadcasted_iota(jnp.int32, (AUG_ROWS, tq), 0)
        third = _mod3(rid)
        part = jnp.where(third == 0, sc_ref[h, 1], jnp.where(third == 1, sc_ref[h, 2], sc_ref[h, 3]))
        aug_below = jnp.where(rid < 3, -r_hi, jnp.where(rid < 6, -r_lo, jnp.where(rid < 12, part, 0.0)))
        zq = jnp.zeros((DIFF_QK, tq), BF16)
        zp = jnp.zeros((KPAD - 2 * DIFF_QK - AUG_ROWS, tq), BF16)
        for v, aug in enumerate((aug_below, -aug_below)):
            a = aug.astype(BF16)
            rhs_scr[2 * v + 0] = jnp.concatenate([q[0:DIFF_QK], zq, a, zp], axis=0)
            rhs_scr[2 * v + 1] = jnp.concatenate([zq, q[DIFF_QK:2 * DIFF_QK], a, zp], axis=0)
    else:
        rhs_scr[0] = jnp.concatenate([q, jnp.zeros((KPAD - q.shape[0], tq), BF16)], axis=0)

    def scores(kb, mode):
        k0 = kb * tk
        kblk = k_ref[0, 0, pl.ds(pl.multiple_of(k0, tk), tk), :]
        dist = slope2 * (q0 - k0).astype(F32) if alibi else 0.0
        out = []
        for j in range(n_maps):
            score = lambda v: jnp.dot(kblk, rhs_scr[n_maps * v + j], preferred_element_type=F32)
            if mode == "below":
                out.append((score(0), -dist))
            elif mode == "above":
                out.append((score(1), dist))
            elif mode == "diag":
                out.append((jnp.minimum(score(0) - dist, score(1) + dist), 0.0))
            else:
                out.append((score(0), 0.0))
        return out

    def consume(kb, sc, carry):
        accs, ms = carry
        vblk = vT_ref[0, 0, kb]
        new_accs, new_ms = [], []
        for j, (s, cb) in enumerate(sc):
            if online:
                m_new = jnp.maximum(ms[j], jnp.max(s, axis=0, keepdims=True) + cb)
                p = jnp.exp2(s + (cb - m_new)).astype(BF16)
                acc = jnp.exp2(ms[j] - m_new) * accs[j] + jnp.dot(vblk, p, preferred_element_type=F32)
            else:
                m_new = ms[j]
                p = jnp.exp2(s + (cb - shift)).astype(BF16)
                acc = accs[j] + jnp.dot(vblk, p, preferred_element_type=F32)
            new_accs.append(acc)
            new_ms.append(m_new)
        return tuple(new_accs), tuple(new_ms)

    def block(kb, carry, mode):
        return consume(kb, scores(kb, mode), carry)

    def run(lo, hi, mode, carry):
        if unroll > 1:
            n = hi - lo
            n_groups = n // unroll if isinstance(n, int) else lax.shift_right_logical(n, int(math.log2(unroll)))

            def group(i, c):
                for u in range(unroll):
                    c = block(lo + i * unroll + u, c, mode)
                return c

            carry = lax.fori_loop(0, n_groups, group, carry)
            lo = lo + n_groups * unroll
        return lax.fori_loop(lo, hi, lambda kb, c: block(kb, c, mode), carry)

    carry = (tuple(jnp.zeros((V_ROWS, tq), F32) for _ in range(n_maps)),
             tuple(jnp.full((1, tq), NEG_BIG, F32) for _ in range(n_maps)))
    if alibi:
        n_below = lax.shift_right_logical(q0, int(math.log2(tk)))
        n_diag = max(1, tq // tk)
        carry = run(0, n_below, "below", carry)
        for d in range(n_diag):
            carry = block(n_below + d, carry, "diag")
        carry = run(n_below + n_diag, nkb, "above", carry)
    else:
        carry = run(0, nkb, "plain", carry)
    accs, _ = carry

    a0 = accs[0]
    o = a0[0:HEAD_V] / a0[HEAD_V:HEAD_V + 1]
    if n_maps == 2:
        a1 = accs[1]
        lv = lv_ref[...]
        lam = (jnp.exp(jnp.sum(lv[0:1] * lv[1:2], axis=1, keepdims=True))
               - jnp.exp(jnp.sum(lv[2:3] * lv[3:4], axis=1, keepdims=True)) + lambda_init)
        o = o - lam * (a1[0:HEAD_V] / a1[HEAD_V:HEAD_V + 1])
        o = _rms_rows(o, gsub_ref[...]) * (1.0 - lambda_init)
    o_ref[0] = o.astype(BF16)


def _attention_call(sc, qT, k, vT, gsub, lv, *, n_maps, alibi, online, lambda_init):
    B, H, kq, S = qT.shape
    _, tq, tk = _tiles(S)
    nkb = S // tk
    assert tk % tq == 0 or tq % tk == 0
    unroll = 1 if online else min(ATTN_UNROLL, nkb)
    kern = functools.partial(_attn_kernel, n_maps=n_maps, alibi=alibi, online=online, tq=tq, tk=tk,
                             nkb=nkb, unroll=unroll, lambda_init=lambda_init)
    n_rhs = 2 * n_maps if alibi else n_maps
    name = ("attn_diff" if alibi else "attn_mla") + ("_online" if online else "")
    return pl.pallas_call(
        kern,
        grid=(B, H, S // tq),
        in_specs=[pl.BlockSpec(memory_space=pltpu.SMEM),
                  pl.BlockSpec((1, 1, kq, tq), lambda b, h, i: (b, h, 0, i)),
                  pl.BlockSpec((1, 1, S, KPAD), lambda b, h, i: (b, h, 0, 0)),
                  pl.BlockSpec((1, 1, nkb, V_ROWS, tk), lambda b, h, i: (b, h, 0, 0, 0)),
                  pl.BlockSpec(gsub.shape, lambda b, h, i: (0, 0)),
                  pl.BlockSpec(lv.shape, lambda b, h, i: (0, 0))],
        out_specs=pl.BlockSpec((1, HEAD_V, tq), lambda b, h, i: (b, h, i)),
        out_shape=jax.ShapeDtypeStruct((B, H * HEAD_V, S), BF16),
        scratch_shapes=[pltpu.VMEM((n_rhs, KPAD, tq), BF16)],
        compiler_params=_cparams(("arbitrary", "arbitrary", "arbitrary")),
        name=name,
    )(sc, qT, k, vT, gsub, lv)


def _attention(qT, k, vT, gq, gk, gsub, lv, *, width, n_maps, alibi, lambda_init):
    bound = math.sqrt(width) * BOUND_MARGIN * jnp.max(jnp.abs(gq)) * jnp.max(jnp.abs(gk))
    slopes2 = [np.float32(s * LOG2E) for s in _alibi_slopes()]
    static = jnp.asarray([[s, *_bf16_parts(s)] for s in slopes2], F32)
    sc = jnp.concatenate([static, jnp.full((HEADS, 1), LOG2E, F32) * bound,
                          jnp.zeros((HEADS, 3), F32)], axis=1)
    call = functools.partial(_attention_call, n_maps=n_maps, alibi=alibi, lambda_init=lambda_init)
    return lax.cond(bound <= FAST_SOFTMAX_BOUND,
                    functools.partial(call, online=False),
                    functools.partial(call, online=True),
                    sc, qT, k, vT, gsub, lv)


def _f1_kernel(t_ref, x_ref, o_ref):
    o_ref[0] = jnp.dot(t_ref[...], x_ref[0], preferred_element_type=F32).astype(BF16)


def _f2_kernel(g_ref, a_ref, cs_ref, o_ref, *, tk1, n2, ch, norm):
    for j in range(tk1):
        z = jnp.concatenate([a_ref[0, 0, j], a_ref[0, 1, j]], axis=0)
        x = jnp.dot(g_ref[j], z, preferred_element_type=F32)
        xr = x[:n2].astype(BF16)
        xi = x[n2:].astype(BF16)
        y = (jnp.dot(xr, cs_ref[0:ch], preferred_element_type=F32)
             + jnp.dot(xi, cs_ref[ch:2 * ch], preferred_element_type=F32))
        o_ref[0, :, j * ch:(j + 1) * ch] = (y * norm).astype(BF16)


def _fourier_tables(S):
    n1, n2 = _fourier_split(S)
    a = jnp.arange(n1, dtype=jnp.int32)
    ang1 = (2.0 * math.pi / n1) * ((a[:, None] * a[None, :]) % n1).astype(F32)
    t1 = jnp.concatenate([jnp.cos(ang1), -jnp.sin(ang1)], axis=0).astype(BF16)
    k = a[:, None, None] + n1 * jnp.arange(n2, dtype=jnp.int32)[None, :, None]
    b = jnp.arange(n2, dtype=jnp.int32)[None, None, :]
    ang = (2.0 * math.pi / S) * ((k * b) % S).astype(F32)
    cg, sg = jnp.cos(ang), jnp.sin(ang)
    g = jnp.concatenate([jnp.concatenate([cg, sg], axis=2),
                         jnp.concatenate([-sg, cg], axis=2)], axis=1).astype(BF16)
    gw = GROUP // FOURIER_GROUPS
    c = jnp.arange(GROUP, dtype=jnp.int32)
    same = (c[:, None] // gw) == (c[None, :] // gw)
    angc = (2.0 * math.pi / gw) * (((c[:, None] % gw) * (c[None, :] % gw)) % gw).astype(F32)
    cs = jnp.concatenate([jnp.where(same, jnp.cos(angc), 0.0),
                          jnp.where(same, jnp.sin(angc), 0.0)], axis=0).astype(BF16)
    return t1, g, cs


def _fourier(fu, tables):
    B, S, C = fu.shape
    n1, n2 = _fourier_split(S)
    t1, g, cs = tables
    tn = min(4096, n2 * C)
    a = pl.pallas_call(
        _f1_kernel,
        grid=(B, n2 * C // tn),
        in_specs=[pl.BlockSpec(t1.shape, lambda b, j: (0, 0)),
                  pl.BlockSpec((1, n1, tn), lambda b, j: (b, 0, j))],
        out_specs=pl.BlockSpec((1, 2 * n1, tn), lambda b, j: (b, 0, j)),
        out_shape=jax.ShapeDtypeStruct((B, 2 * n1, n2 * C), BF16),
        compiler_params=_cparams(("arbitrary", "arbitrary")),
        name="fourier_stage1",
    )(t1, fu.reshape(B, n1, n2 * C))
    tk1 = 8
    norm = 1.0 / math.sqrt(S * (GROUP // FOURIER_GROUPS))
    y = pl.pallas_call(
        functools.partial(_f2_kernel, tk1=tk1, n2=n2, ch=C, norm=norm),
        grid=(B, n1 // tk1),
        in_specs=[pl.BlockSpec((tk1, 2 * n2, 2 * n2), lambda b, j: (j, 0, 0)),
                  pl.BlockSpec((1, 2, tk1, n2, C), lambda b, j: (b, 0, j, 0, 0)),
                  pl.BlockSpec(cs.shape, lambda b, j: (0, 0))],
        out_specs=pl.BlockSpec((1, n2, tk1 * C), lambda b, j: (b, 0, j)),
        out_shape=jax.ShapeDtypeStruct((B, n2, n1 * C), BF16),
        compiler_params=_cparams(("arbitrary", "arbitrary")),
        name="fourier_stage2",
    )(g, a.reshape(B, 2, n1, n2, C), cs)
    return y.reshape(B, S, C)


def _outproj_kernel(x_ref, m_ref, ab_ref, u_ref, up_ref, un_ref, cw_ref, yb_ref, yc_ref, yd_ref,
                    w_ref, o_ref, *, ts, nt):
    t = pl.program_id(1)
    u = u_ref[0]
    prev_row = jnp.where(t == 0, 0.0, up_ref[0, 7:8, :])
    next_row = jnp.where(t == nt - 1, 0.0, un_ref[0, 0:1, :])
    rid = lax.broadcasted_iota(jnp.int32, u.shape, 0)
    u_m1 = jnp.where(rid == 0, prev_row, pltpu.roll(u, 1, 0))
    u_p1 = jnp.where(rid == ts - 1, next_row, pltpu.roll(u, ts - 1, 0))
    conv = u_m1 * cw_ref[0:1, :] + u * cw_ref[1:2, :] + u_p1 * cw_ref[2:3, :]
    ya = (ab_ref[0] * conv).astype(BF16)
    tn_dims = (((0,), (0,)), ((), ()))
    acc = jnp.dot(ya, w_ref[0:GROUP], preferred_element_type=F32)
    acc += lax.dot_general(yb_ref[0], w_ref[GROUP:2 * GROUP], tn_dims, preferred_element_type=F32)
    acc += lax.dot_general(yc_ref[0], w_ref[2 * GROUP:3 * GROUP], tn_dims, preferred_element_type=F32)
    acc += jnp.dot(yd_ref[0], w_ref[3 * GROUP:4 * GROUP], preferred_element_type=F32)
    o_ref[0] = x_ref[0] + m_ref[0, 5:6, :] * acc


def _outproj(x, m, ab, u, conv_w, ybT, ycT, yd, w_out):
    B, S, D = x.shape
    ts, _, _ = _tiles(S)
    nt = S // ts
    r8 = ts // 8
    tok = lambda b, t: (b, t, 0)
    return pl.pallas_call(
        functools.partial(_outproj_kernel, ts=ts, nt=nt),
        grid=(B, nt),
        in_specs=[pl.BlockSpec((1, ts, D), tok),
                  pl.BlockSpec((1, N_MOD, D), lambda b, t: (b, 0, 0)),
                  pl.BlockSpec((1, ts, GROUP), tok),
                  pl.BlockSpec((1, ts, GROUP), tok),
                  pl.BlockSpec((1, 8, GROUP), lambda b, t: (b, jnp.maximum(t * r8 - 1, 0), 0)),
                  pl.BlockSpec((1, 8, GROUP), lambda b, t: (b, jnp.minimum((t + 1) * r8, S // 8 - 1), 0)),
                  pl.BlockSpec(conv_w.shape, lambda b, t: (0, 0)),
                  pl.BlockSpec((1, GROUP, ts), lambda b, t: (b, 0, t)),
                  pl.BlockSpec((1, GROUP, ts), lambda b, t: (b, 0, t)),
                  pl.BlockSpec((1, ts, GROUP), tok),
                  pl.BlockSpec(w_out.shape, lambda b, t: (0, 0), pipeline_mode=pl.Buffered(1))],
        out_specs=pl.BlockSpec((1, ts, D), tok),
        out_shape=jax.ShapeDtypeStruct((B, S, D), F32),
        compiler_params=_cparams(("arbitrary", "arbitrary")),
        name="outproj",
    )(x, m, ab, u, u, u, conv_w, ybT, ycT, yd, w_out)


def _rope_tables(S):
    inv = ROPE_THETA ** (-jnp.arange(0, MLA_ROPE, 2, dtype=F32) / MLA_ROPE)
    ang = inv[:, None] * jnp.arange(S, dtype=F32)[None, :]
    return jnp.cos(ang), jnp.sin(ang)


def _alibi_slopes():
    return tuple(float(2.0 ** (-8.0 * (i + 1) / HEADS)) for i in range(HEADS))


def _prep_layer(p, l):
    col = lambda v: v.astype(F32).reshape(-1, 1)
    w_in = p['w_in'][l].astype(BF16)
    nch = p['ffn1_w_gu'].shape[-1] // 2 // FF_CHUNK

    def ffn_w(wgu, wd):
        D = wgu.shape[0]
        g = wgu[:, :nch * FF_CHUNK].reshape(D, nch, FF_CHUNK)
        u = wgu[:, nch * FF_CHUNK:].reshape(D, nch, FF_CHUNK)
        wgu_c = jnp.transpose(jnp.concatenate([g, u], axis=2), (1, 0, 2)).astype(BF16)
        return wgu_c, wd.reshape(nch, FF_CHUNK, D).astype(BF16)

    return dict(
        ffn1=ffn_w(p['ffn1_w_gu'][l], p['ffn1_w_down'][l]),
        ffn2=ffn_w(p['ffn2_w_gu'][l], p['ffn2_w_down'][l]),
        w_nat=jnp.concatenate([w_in[:, :NAT_LO], w_in[:, ATT_HI:]], axis=1),
        w_T=w_in[:, NAT_LO:ATT_HI].T,
        wuqT=p['mla_w_uq'][l].astype(BF16).T,
        wukvT=p['mla_w_ukv'][l].astype(BF16).T,
        gdq=col(p['diff_q_g'][l]), gdk=col(p['diff_k_g'][l]),
        gqa=col(p['mla_q_a_g'][l]), gkva=col(p['mla_kv_a_g'][l]),
        gmq=col(p['mla_q_g'][l]), gmk=col(p['mla_k_g'][l]),
        gsub=col(p['diff_subln_g'][l]),
        lv=p['diff_lambda'][l].astype(F32),
        conv_w=p['conv_w'][l].astype(F32),
        w_out=p['w_out'][l].astype(BF16),
        norm_g=p['norm_g'][l].astype(F32),
    )


def _layer(x, m, w, l, consts):
    rope_cos, rope_sin, ftables = consts
    ng = w['norm_g']
    x = _ffn(x, m, ng[0:1], *w['ffn1'], 0)
    slope_parts = tuple(_bf16_parts(np.float32(s * LOG2E)) for s in _alibi_slopes())
    ab, u, fu, qd, kd, vd, qm, km, vm = _inproj(x, m, ng[1:2], w, rope_cos, rope_sin, slope_parts)
    lambda_init = 0.8 - 0.6 * math.exp(-0.3 * l)
    ybT = _attention(qd, kd, vd, w['gdq'], w['gdk'], w['gsub'], w['lv'],
                     width=DIFF_QK, n_maps=2, alibi=True, lambda_init=lambda_init)
    ycT = _attention(qm, km, vm, w['gmq'], w['gmk'], w['gsub'], w['lv'],
                     width=MLA_QK, n_maps=1, alibi=False, lambda_init=0.0)
    yd = _fourier(fu, ftables)
    x = _outproj(x, m, ab, u, w['conv_w'], ybT, ycT, yd, w['w_out'])
    return _ffn(x, m, ng[2:3], *w['ffn2'], 2)


def _trunk(groups, c_all, p):
    depth = p['w_mod'].shape[0]
    mods = _modulation(c_all, p['w_mod'], p['b_mod'])
    outs = []
    consts = []
    for x, _ in groups:
        S = x.shape[1]
        consts.append(_rope_tables(S) + (_fourier_tables(S),))
    xs = [x for x, _ in groups]
    for l in range(depth):
        w = _prep_layer(p, l)
        for gi, (x0, off) in enumerate(groups):
            B, _, D = x0.shape
            m = mods[l, off:off + B].reshape(B, N_MOD, D)
            xs[gi] = _layer(xs[gi], m, w, l, consts[gi])
    return xs


def kernel(x_prompt, x_sample, c_prompt, c_sample, w_mod, b_mod, norm_g, ffn1_w_gu, ffn1_w_down,
           w_in, conv_w, diff_lambda, diff_q_g, diff_k_g, diff_subln_g, mla_q_a_g, mla_w_uq,
           mla_kv_a_g, mla_w_ukv, mla_q_g, mla_k_g, w_out, ffn2_w_gu, ffn2_w_down):
    p = dict(w_mod=w_mod, b_mod=b_mod, norm_g=norm_g, ffn1_w_gu=ffn1_w_gu, ffn1_w_down=ffn1_w_down,
             w_in=w_in, conv_w=conv_w, diff_lambda=diff_lambda, diff_q_g=diff_q_g, diff_k_g=diff_k_g,
             diff_subln_g=diff_subln_g, mla_q_a_g=mla_q_a_g, mla_w_uq=mla_w_uq, mla_kv_a_g=mla_kv_a_g,
             mla_w_ukv=mla_w_ukv, mla_q_g=mla_q_g, mla_k_g=mla_k_g, w_out=w_out,
             ffn2_w_gu=ffn2_w_gu, ffn2_w_down=ffn2_w_down)
    nb = c_prompt.shape[0] + c_sample.shape[0]
    pad = (-nb) % 8
    c_all = jnp.concatenate([c_prompt, c_sample, jnp.zeros((pad, c_prompt.shape[1]), c_prompt.dtype)])
    y_prompt, y_sample = _trunk([(x_prompt, 0), (x_sample, c_prompt.shape[0])], c_all, p)
    return (y_prompt, y_sample)
```

```python
import functools
import math

import numpy as np
import jax
import jax.numpy as jnp
from jax import lax
from jax.experimental import pallas as pl
from jax.experimental.pallas import tpu as pltpu

F32 = jnp.float32
BF16 = jnp.bfloat16

HEADS = 4
GROUP = 256
DIFF_QK = 32
MLA_NOPE = 64
MLA_ROPE = 32
MLA_QK = MLA_NOPE + MLA_ROPE
MLA_Q_RANK = 256
MLA_KV_RANK = 128
HEAD_V = 64
FOURIER_GROUPS = 4
ROPE_THETA = 10000.0
EPS = 1e-6
N_MOD = 9
ATT_ROWS = 3 * GROUP + MLA_Q_RANK + MLA_KV_RANK + MLA_ROPE
NAT_LO = 3 * GROUP
ATT_HI = NAT_LO + ATT_ROWS

LANES = 128
BF16_SUBLANES = 16
KPAD = 128
V_ROWS = HEAD_V + BF16_SUBLANES
AUG_ROWS = BF16_SUBLANES
FF_CHUNK = 256
ATTN_GROUPS = (8, 4, 2, 1)
VMEM_LIMIT = 52 * 1024 * 1024
NEG_BIG = -1e30
LOG2E = 1.4426950408889634
FAST_SOFTMAX_BOUND = 25.0
BOUND_MARGIN = 1.02


def _mod3(i):
    three = lambda c: jnp.where(c, 3, 0)
    return i - three(i >= 3) - three(i >= 6) - three(i >= 9)


def _bf16_parts(x):
    parts, r = [], np.float32(x)
    for _ in range(3):
        p = np.float32(np.asarray(r, dtype=jnp.bfloat16))
        parts.append(float(p))
        r = np.float32(r - p)
    return tuple(parts)


def _cparams(sem):
    return pltpu.CompilerParams(dimension_semantics=sem, vmem_limit_bytes=VMEM_LIMIT)


def _tiles(S):
    ts = min(512, S)
    tq = min(512, S)
    tk = min(512, S)
    return ts, tq, tk


def _fourier_split(S):
    n1 = 1 << ((int(math.log2(S)) + 1) // 2)
    return n1, S // n1


def _mod_kernel(c_ref, w_ref, b_ref, o_ref):
    c = c_ref[...]
    s = (c * jax.nn.sigmoid(c)).astype(BF16)
    o_ref[0] = jnp.dot(s, w_ref[0].astype(BF16), preferred_element_type=F32) + b_ref[0]


def _modulation(c_all, w_mod, b_mod):
    L, D, N = w_mod.shape
    bp = c_all.shape[0]
    tn = 1536
    return pl.pallas_call(
        _mod_kernel,
        grid=(L, N // tn),
        in_specs=[pl.BlockSpec((bp, D), lambda l, j: (0, 0)),
                  pl.BlockSpec((1, D, tn), lambda l, j: (l, 0, j)),
                  pl.BlockSpec((1, 1, tn), lambda l, j: (l, 0, j))],
        out_specs=pl.BlockSpec((1, bp, tn), lambda l, j: (l, 0, j)),
        out_shape=jax.ShapeDtypeStruct((L, bp, N), F32),
        compiler_params=_cparams(("arbitrary", "arbitrary")),
        name="modulation",
    )(c_all, w_mod, b_mod.reshape(L, 1, N))


def _modulated_norm(x, m_ref, g_ref, sub):
    shift = m_ref[0, 3 * sub:3 * sub + 1, :]
    scale = m_ref[0, 3 * sub + 1:3 * sub + 2, :]
    ms = jnp.mean(x * x, axis=-1, keepdims=True)
    y = x * lax.rsqrt(ms + EPS) * g_ref[...]
    return y * (1.0 + scale) + shift


def _ffn_kernel(x_ref, m_ref, g_ref, wgu_ref, wd_ref, o_ref, h_scr, acc_scr, *, sub, nch):
    x = x_ref[0]
    h_scr[...] = _modulated_norm(x, m_ref, g_ref, sub).astype(BF16)
    acc_scr[...] = jnp.zeros_like(acc_scr)

    def body(c, carry):
        gu = jnp.dot(h_scr[...], wgu_ref[c], preferred_element_type=F32)
        g = gu[:, :FF_CHUNK]
        u = gu[:, FF_CHUNK:]
        a = (g * jax.nn.sigmoid(g) * u).astype(BF16)
        acc_scr[...] += jnp.dot(a, wd_ref[c], preferred_element_type=F32)
        return carry

    lax.fori_loop(0, nch, body, 0, unroll=True)
    gate = m_ref[0, 3 * sub + 2:3 * sub + 3, :]
    o_ref[0] = x + 0.5 * gate * acc_scr[...]


def _ffn(x, m, g, wgu, wd, sub):
    B, S, D = x.shape
    ts, _, _ = _tiles(S)
    nch = wgu.shape[0]
    const3 = lambda b, t: (0, 0, 0)
    return pl.pallas_call(
        functools.partial(_ffn_kernel, sub=sub, nch=nch),
        grid=(B, S // ts),
        in_specs=[pl.BlockSpec((1, ts, D), lambda b, t: (b, t, 0)),
                  pl.BlockSpec((1, N_MOD, D), lambda b, t: (b, 0, 0)),
                  pl.BlockSpec((1, D), lambda b, t: (0, 0)),
                  pl.BlockSpec(wgu.shape, const3, pipeline_mode=pl.Buffered(1)),
                  pl.BlockSpec(wd.shape, const3, pipeline_mode=pl.Buffered(1))],
        out_specs=pl.BlockSpec((1, ts, D), lambda b, t: (b, t, 0)),
        out_shape=jax.ShapeDtypeStruct((B, S, D), F32),
        scratch_shapes=[pltpu.VMEM((ts, D), BF16), pltpu.VMEM((ts, D), F32)],
        compiler_params=_cparams(("arbitrary", "arbitrary")),
        name="ffn",
    )(x, m, g, wgu, wd)


def _rms_rows(v, gain):
    inv = lax.rsqrt(jnp.mean(v * v, axis=0, keepdims=True) + EPS)
    return v * inv * gain


def _rope_rows(t1, t2, cos, sin):
    return t1 * cos - t2 * sin, t1 * sin + t2 * cos


def _ones_row_tile(tk):
    rid = lax.broadcasted_iota(jnp.int32, (V_ROWS - HEAD_V, tk), 0)
    return jnp.where(rid == 0, 1.0, 0.0).astype(BF16)


def _inproj_kernel(x_ref, m_ref, g_ref, wnat_ref, wT_ref, wuqT_ref, wukvT_ref,
                   gdq_ref, gdk_ref, gqa_ref, gkva_ref, gmq_ref, gmk_ref, cos_ref, sin_ref,
                   ab_ref, u_ref, fu_ref, qd_ref, kd_ref, vd_ref, qm_ref, km_ref, vm_ref,
                   *, ts, tk, slope_parts):
    t = pl.program_id(1)
    hb = _modulated_norm(x_ref[0], m_ref, g_ref, 1).astype(BF16)

    nat = jnp.dot(hb, wnat_ref[...], preferred_element_type=F32)
    ab_ref[0] = nat[:, 0:GROUP]
    u_ref[0] = nat[:, GROUP:2 * GROUP] * nat[:, 2 * GROUP:3 * GROUP]
    fu_ref[0] = nat[:, 3 * GROUP:4 * GROUP].astype(BF16)

    pT = lax.dot_general(wT_ref[...], hb, (((1,), (1,)), ((), ())),
                         preferred_element_type=F32)

    ones_tile = _ones_row_tile(tk)
    pos = t * ts + lax.broadcasted_iota(jnp.int32, (1, ts), 1)
    c = pos & (tk - 1)
    c_hi = ((c >> 8) << 8).astype(F32)
    c_lo = (c & 255).astype(F32)
    rid = lax.broadcasted_iota(jnp.int32, (AUG_ROWS, ts), 0)
    pad_d = jnp.zeros((KPAD - 2 * DIFF_QK - AUG_ROWS, ts), F32)
    pad_m = jnp.zeros((KPAD - MLA_QK, ts), F32)
    gdq = gdq_ref[...] * (DIFF_QK ** -0.5 * LOG2E)
    third = _mod3(rid)
    gdk = gdk_ref[...]

    for h in range(HEADS):
        base = 2 * DIFF_QK * h
        ks = []
        for j in range(2):
            r0 = base + DIFF_QK * j
            qd_ref[0, h, DIFF_QK * j:DIFF_QK * (j + 1), :] = _rms_rows(pT[r0:r0 + DIFF_QK], gdq).astype(BF16)
            ks.append(_rms_rows(pT[GROUP + r0:GROUP + r0 + DIFF_QK], gdk))
        s0, s1, s2 = slope_parts[h]
        part = jnp.where(third == 0, s0, jnp.where(third == 1, s1, s2))
        aug = jnp.where(rid < 6, part, jnp.where(rid < 9, c_hi, jnp.where(rid < 12, c_lo, 0.0)))
        kext = jnp.concatenate(ks + [aug, pad_d], axis=0)
        kd_ref[0, h] = kext.T.astype(BF16)
        v = pT[2 * GROUP + HEAD_V * h:2 * GROUP + HEAD_V * (h + 1)].astype(BF16)
        for kb in range(ts // tk):
            vd_ref[0, h, kb, 0:HEAD_V, :] = v[:, kb * tk:(kb + 1) * tk]
            vd_ref[0, h, kb, HEAD_V:V_ROWS, :] = ones_tile

    o_cq = 3 * GROUP
    o_ckv = o_cq + MLA_Q_RANK
    o_kpe = o_ckv + MLA_KV_RANK
    cqn = _rms_rows(pT[o_cq:o_ckv], gqa_ref[...]).astype(BF16)
    qmT = jnp.dot(wuqT_ref[...], cqn, preferred_element_type=F32)
    ckvn = _rms_rows(pT[o_ckv:o_kpe], gkva_ref[...]).astype(BF16)
    kvT = jnp.dot(wukvT_ref[...], ckvn, preferred_element_type=F32)
    kpe = pT[o_kpe:o_kpe + MLA_ROPE]
    cos = cos_ref[...]
    sin = sin_ref[...]
    gmq = gmq_ref[...] * (MLA_QK ** -0.5 * LOG2E)
    gmk = gmk_ref[...]
    half = MLA_ROPE // 2
    for h in range(HEADS):
        qn = _rms_rows(qmT[MLA_QK * h:MLA_QK * (h + 1)], gmq)
        q1, q2 = _rope_rows(qn[MLA_NOPE:MLA_NOPE + half], qn[MLA_NOPE + half:MLA_QK], cos, sin)
        qm_ref[0, h, 0:MLA_NOPE, :] = qn[0:MLA_NOPE].astype(BF16)
        qm_ref[0, h, MLA_NOPE:MLA_NOPE + half, :] = q1.astype(BF16)
        qm_ref[0, h, MLA_NOPE + half:MLA_QK, :] = q2.astype(BF16)
        kv0 = (MLA_NOPE + HEAD_V) * h
        kn = _rms_rows(jnp.concatenate([kvT[kv0:kv0 + MLA_NOPE], kpe], axis=0), gmk)
        k1, k2 = _rope_rows(kn[MLA_NOPE:MLA_NOPE + half], kn[MLA_NOPE + half:MLA_QK], cos, sin)
        kext = jnp.concatenate([kn[0:MLA_NOPE], k1, k2, pad_m], axis=0)
        km_ref[0, h] = kext.T.astype(BF16)
        v = kvT[kv0 + MLA_NOPE:kv0 + MLA_NOPE + HEAD_V].astype(BF16)
        for kb in range(ts // tk):
            vm_ref[0, h, kb, 0:HEAD_V, :] = v[:, kb * tk:(kb + 1) * tk]
            vm_ref[0, h, kb, HEAD_V:V_ROWS, :] = ones_tile


def _inproj(x, m, g, w, rope_cos, rope_sin, slope_parts):
    B, S, D = x.shape
    ts, _, tk = _tiles(S)
    nkb = S // tk
    c2 = lambda b, t: (0, 0)
    tok = lambda b, t: (b, t, 0)
    head_T = lambda b, t: (b, 0, 0, t)
    head_N = lambda b, t: (b, 0, t, 0)
    head_V = lambda b, t: (b, 0, t, 0, 0)
    small = [w['gdq'], w['gdk'], w['gqa'], w['gkva'], w['gmq'], w['gmk']]
    in_specs = ([pl.BlockSpec((1, ts, D), tok),
                 pl.BlockSpec((1, N_MOD, D), lambda b, t: (b, 0, 0)),
                 pl.BlockSpec((1, D), c2),
                 pl.BlockSpec(w['w_nat'].shape, c2, pipeline_mode=pl.Buffered(1)),
                 pl.BlockSpec(w['w_T'].shape, c2, pipeline_mode=pl.Buffered(1)),
                 pl.BlockSpec(w['wuqT'].shape, c2, pipeline_mode=pl.Buffered(1)),
                 pl.BlockSpec(w['wukvT'].shape, c2, pipeline_mode=pl.Buffered(1))]
                + [pl.BlockSpec(a.shape, c2) for a in small]
                + [pl.BlockSpec((MLA_ROPE // 2, ts), lambda b, t: (0, t))] * 2)
    out_shape = [jax.ShapeDtypeStruct((B, S, GROUP), F32),
                 jax.ShapeDtypeStruct((B, S, GROUP), F32),
                 jax.ShapeDtypeStruct((B, S, GROUP), BF16),
                 jax.ShapeDtypeStruct((B, HEADS, 2 * DIFF_QK, S), BF16),
                 jax.ShapeDtypeStruct((B, HEADS, S, KPAD), BF16),
                 jax.ShapeDtypeStruct((B, HEADS, nkb, V_ROWS, tk), BF16),
                 jax.ShapeDtypeStruct((B, HEADS, MLA_QK, S), BF16),
                 jax.ShapeDtypeStruct((B, HEADS, S, KPAD), BF16),
                 jax.ShapeDtypeStruct((B, HEADS, nkb, V_ROWS, tk), BF16)]
    out_specs = [pl.BlockSpec((1, ts, GROUP), tok),
                 pl.BlockSpec((1, ts, GROUP), tok),
                 pl.BlockSpec((1, ts, GROUP), tok),
                 pl.BlockSpec((1, HEADS, 2 * DIFF_QK, ts), head_T),
                 pl.BlockSpec((1, HEADS, ts, KPAD), head_N),
                 pl.BlockSpec((1, HEADS, ts // tk, V_ROWS, tk), head_V),
                 pl.BlockSpec((1, HEADS, MLA_QK, ts), head_T),
                 pl.BlockSpec((1, HEADS, ts, KPAD), head_N),
                 pl.BlockSpec((1, HEADS, ts // tk, V_ROWS, tk), head_V)]
    return pl.pallas_call(
        functools.partial(_inproj_kernel, ts=ts, tk=tk, slope_parts=slope_parts),
        grid=(B, S // ts),
        in_specs=in_specs,
        out_specs=out_specs,
        out_shape=out_shape,
        compiler_params=_cparams(("arbitrary", "arbitrary")),
        name="inproj",
    )(x, m, g, w['w_nat'], w['w_T'], w['wuqT'], w['wukvT'], *small, rope_cos, rope_sin)


def _attn_kernel(sc_ref, qT_ref, k_ref, vT_ref, gsub_ref, lv_ref, o_ref, rhs_scr,
                 *, n_maps, alibi, online, tq, tk, nkb, group_sizes, lambda_init):
    h = pl.program_id(1)
    q0 = pl.program_id(2) * tq
    q = qT_ref[0, 0]
    shift = sc_ref[h, 4]

    if alibi:
        slope2 = sc_ref[h, 0]
        lane = lax.broadcasted_iota(jnp.int32, (1, tq), 1)
        r_hi = ((lane >> 8) << 8).astype(F32)
        r_lo = (lane & 255).astype(F32)
        rid = lax.broadcasted_iota(jnp.int32, (AUG_ROWS, tq), 0)
        third = _mod3(rid)
        part = jnp.where(third == 0, sc_ref[h, 1], jnp.where(third == 1, sc_ref[h, 2], sc_ref[h, 3]))
        aug_below = jnp.where(rid < 3, -r_hi, jnp.where(rid < 6, -r_lo, jnp.where(rid < 12, part, 0.0)))
        zq = jnp.zeros((DIFF_QK, tq), BF16)
        zp = jnp.zeros((KPAD - 2 * DIFF_QK - AUG_ROWS, tq), BF16)
        for v, aug in enumerate((aug_below, -aug_below)):
            a = aug.astype(BF16)
            rhs_scr[2 * v + 0] = jnp.concatenate([q[0:DIFF_QK], zq, a, zp], axis=0)
            rhs_scr[2 * v + 1] = jnp.concatenate([zq, q[DIFF_QK:2 * DIFF_QK], a, zp], axis=0)
    else:
        rhs_scr[0] = jnp.concatenate([q, jnp.zeros((KPAD - q.shape[0], tq), BF16)], axis=0)

    def key_rows(kb, n):
        return k_ref[0, 0, pl.ds(pl.multiple_of(kb * tk, tk), n * tk), :]

    def accumulate(j, kb, s, cb, accs, ms):
        vblk = vT_ref[0, 0, kb]
        if online:
            m_new = jnp.maximum(ms[j], jnp.max(s, axis=0, keepdims=True) + cb)
            p = jnp.exp2(s + (cb - m_new)).astype(BF16)
            return (jnp.exp2(ms[j] - m_new) * accs[j] + jnp.dot(vblk, p, preferred_element_type=F32), m_new)
        p = jnp.exp2(s + (cb - shift)).astype(BF16)
        return accs[j] + jnp.dot(vblk, p, preferred_element_type=F32), ms[j]

    def group(base, n, side, carry):
        accs, ms = carry
        accs, ms = list(accs), list(ms)
        kgrp = key_rows(base, n)
        sign = lax.convert_element_type(2 * side - 1, F32) if alibi else 0.0
        for j in range(n_maps):
            s = jnp.dot(kgrp, rhs_scr[n_maps * side + j], preferred_element_type=F32)
            for u in range(n):
                cb = sign * slope2 * (q0 - (base + u) * tk).astype(F32) if alibi else 0.0
                accs[j], ms[j] = accumulate(j, base + u, s[u * tk:(u + 1) * tk], cb, accs, ms)
        return tuple(accs), tuple(ms)

    def diagonal(kb, carry):
        accs, ms = carry
        accs, ms = list(accs), list(ms)
        kblk = key_rows(kb, 1)
        dist = slope2 * (q0 - kb * tk).astype(F32)
        for j in range(n_maps):
            below = jnp.dot(kblk, rhs_scr[j], preferred_element_type=F32) - dist
            above = jnp.dot(kblk, rhs_scr[n_maps + j], preferred_element_type=F32) + dist
            accs[j], ms[j] = accumulate(j, kb, jnp.minimum(below, above), 0.0, accs, ms)
        return tuple(accs), tuple(ms)

    def span(lo, hi, side, carry):
        for gi, n in enumerate(group_sizes):
            left = hi - lo
            if isinstance(left, int):
                count = left // n
            else:
                count = lax.shift_right_logical(left, int(math.log2(n)))
                count = count if gi == 0 else count & 1
            if isinstance(count, int) and count == 0:
                continue
            carry = lax.fori_loop(0, count, lambda i, c, lo=lo, n=n: group(lo + i * n, n, side, c), carry)
            lo = lo + count * n
        return carry

    carry = (tuple(jnp.zeros((V_ROWS, tq), F32) for _ in range(n_maps)),
             tuple(jnp.full((1, tq), NEG_BIG, F32) for _ in range(n_maps)))
    if alibi:
        n_below = lax.shift_right_logical(q0, int(math.log2(tk)))
        n_diag = max(1, tq // tk)

        def region(side, c):
            lo = side * (n_below + n_diag)
            hi = n_below + side * (nkb - n_below)
            return span(lo, hi, side, c)

        carry = lax.fori_loop(0, 2, region, carry)
        for d in range(n_diag):
            carry = diagonal(n_below + d, carry)
    else:
        carry = span(0, nkb, 0, carry)
    accs, _ = carry

    a0 = accs[0]
    o = a0[0:HEAD_V] / a0[HEAD_V:HEAD_V + 1]
    if n_maps == 2:
        a1 = accs[1]
        lv = lv_ref[...]
        lam = (jnp.exp(jnp.sum(lv[0:1] * lv[1:2], axis=1, keepdims=True))
               - jnp.exp(jnp.sum(lv[2:3] * lv[3:4], axis=1, keepdims=True)) + lambda_init)
        o = o - lam * (a1[0:HEAD_V] / a1[HEAD_V:HEAD_V + 1])
        o = _rms_rows(o, gsub_ref[...]) * (1.0 - lambda_init)
    o_ref[0] = o.astype(BF16)


def _attention_call(sc, qT, k, vT, gsub, lv, *, n_maps, alibi, online, lambda_init):
    B, H, kq, S = qT.shape
    _, tq, tk = _tiles(S)
    nkb = S // tk
    assert tk % tq == 0 or tq % tk == 0
    group_sizes = (1,) if online else tuple(n for n in ATTN_GROUPS if n <= nkb)
    kern = functools.partial(_attn_kernel, n_maps=n_maps, alibi=alibi, online=online, tq=tq, tk=tk,
                             nkb=nkb, group_sizes=group_sizes, lambda_init=lambda_init)
    n_rhs = 2 * n_maps if alibi else n_maps
    name = ("attn_diff" if alibi else "attn_mla") + ("_online" if online else "")
    return pl.pallas_call(
        kern,
        grid=(B, H, S // tq),
        in_specs=[pl.BlockSpec(memory_space=pltpu.SMEM),
                  pl.BlockSpec((1, 1, kq, tq), lambda b, h, i: (b, h, 0, i)),
                  pl.BlockSpec((1, 1, S, KPAD), lambda b, h, i: (b, h, 0, 0)),
                  pl.BlockSpec((1, 1, nkb, V_ROWS, tk), lambda b, h, i: (b, h, 0, 0, 0)),
                  pl.BlockSpec(gsub.shape, lambda b, h, i: (0, 0)),
                  pl.BlockSpec(lv.shape, lambda b, h, i: (0, 0))],
        out_specs=pl.BlockSpec((1, HEAD_V, tq), lambda b, h, i: (b, h, i)),
        out_shape=jax.ShapeDtypeStruct((B, H * HEAD_V, S), BF16),
        scratch_shapes=[pltpu.VMEM((n_rhs, KPAD, tq), BF16)],
        compiler_params=_cparams(("arbitrary", "arbitrary", "arbitrary")),
        name=name,
    )(sc, qT, k, vT, gsub, lv)


def _attention(qT, k, vT, gq, gk, gsub, lv, *, width, n_maps, alibi, lambda_init):
    bound = math.sqrt(width) * BOUND_MARGIN * jnp.max(jnp.abs(gq)) * jnp.max(jnp.abs(gk))
    slopes2 = [np.float32(s * LOG2E) for s in _alibi_slopes()]
    static = jnp.asarray([[s, *_bf16_parts(s)] for s in slopes2], F32)
    sc = jnp.concatenate([static, jnp.full((HEADS, 1), LOG2E, F32) * bound,
                          jnp.zeros((HEADS, 3), F32)], axis=1)
    call = functools.partial(_attention_call, n_maps=n_maps, alibi=alibi, lambda_init=lambda_init)
    return lax.cond(bound <= FAST_SOFTMAX_BOUND,
                    functools.partial(call, online=False),
                    functools.partial(call, online=True),
                    sc, qT, k, vT, gsub, lv)


def _f1_kernel(t_ref, x_ref, o_ref):
    o_ref[0] = jnp.dot(t_ref[...], x_ref[0], preferred_element_type=F32).astype(BF16)


def _f2_kernel(g_ref, a_ref, cs_ref, o_ref, *, tk1, n2, ch, norm):
    for j in range(tk1):
        z = jnp.concatenate([a_ref[0, 0, j], a_ref[0, 1, j]], axis=0)
        x = jnp.dot(g_ref[j], z, preferred_element_type=F32)
        xr = x[:n2].astype(BF16)
        xi = x[n2:].astype(BF16)
        y = (jnp.dot(xr, cs_ref[0:ch], preferred_element_type=F32)
             + jnp.dot(xi, cs_ref[ch:2 * ch], preferred_element_type=F32))
        o_ref[0, :, j * ch:(j + 1) * ch] = (y * norm).astype(BF16)


def _fourier_tables(S):
    n1, n2 = _fourier_split(S)
    a = jnp.arange(n1, dtype=jnp.int32)
    ang1 = (2.0 * math.pi / n1) * ((a[:, None] * a[None, :]) % n1).astype(F32)
    t1 = jnp.concatenate([jnp.cos(ang1), -jnp.sin(ang1)], axis=0).astype(BF16)
    k = a[:, None, None] + n1 * jnp.arange(n2, dtype=jnp.int32)[None, :, None]
    b = jnp.arange(n2, dtype=jnp.int32)[None, None, :]
    ang = (2.0 * math.pi / S) * ((k * b) % S).astype(F32)
    cg, sg = jnp.cos(ang), jnp.sin(ang)
    g = jnp.concatenate([jnp.concatenate([cg, sg], axis=2),
                         jnp.concatenate([-sg, cg], axis=2)], axis=1).astype(BF16)
    gw = GROUP // FOURIER_GROUPS
    c = jnp.arange(GROUP, dtype=jnp.int32)
    same = (c[:, None] // gw) == (c[None, :] // gw)
    angc = (2.0 * math.pi / gw) * (((c[:, None] % gw) * (c[None, :] % gw)) % gw).astype(F32)
    cs = jnp.concatenate([jnp.where(same, jnp.cos(angc), 0.0),
                          jnp.where(same, jnp.sin(angc), 0.0)], axis=0).astype(BF16)
    return t1, g, cs


def _fourier(fu, tables):
    B, S, C = fu.shape
    n1, n2 = _fourier_split(S)
    t1, g, cs = tables
    tn = min(4096, n2 * C)
    a = pl.pallas_call(
        _f1_kernel,
        grid=(B, n2 * C // tn),
        in_specs=[pl.BlockSpec(t1.shape, lambda b, j: (0, 0)),
                  pl.BlockSpec((1, n1, tn), lambda b, j: (b, 0, j))],
        out_specs=pl.BlockSpec((1, 2 * n1, tn), lambda b, j: (b, 0, j)),
        out_shape=jax.ShapeDtypeStruct((B, 2 * n1, n2 * C), BF16),
        compiler_params=_cparams(("arbitrary", "arbitrary")),
        name="fourier_stage1",
    )(t1, fu.reshape(B, n1, n2 * C))
    tk1 = 8
    norm = 1.0 / math.sqrt(S * (GROUP // FOURIER_GROUPS))
    y = pl.pallas_call(
        functools.partial(_f2_kernel, tk1=tk1, n2=n2, ch=C, norm=norm),
        grid=(B, n1 // tk1),
        in_specs=[pl.BlockSpec((tk1, 2 * n2, 2 * n2), lambda b, j: (j, 0, 0)),
                  pl.BlockSpec((1, 2, tk1, n2, C), lambda b, j: (b, 0, j, 0, 0)),
                  pl.BlockSpec(cs.shape, lambda b, j: (0, 0))],
        out_specs=pl.BlockSpec((1, n2, tk1 * C), lambda b, j: (b, 0, j)),
        out_shape=jax.ShapeDtypeStruct((B, n2, n1 * C), BF16),
        compiler_params=_cparams(("arbitrary", "arbitrary")),
        name="fourier_stage2",
    )(g, a.reshape(B, 2, n1, n2, C), cs)
    return y.reshape(B, S, C)


def _outproj_kernel(x_ref, m_ref, ab_ref, u_ref, up_ref, un_ref, cw_ref, yb_ref, yc_ref, yd_ref,
                    w_ref, o_ref, *, ts, nt):
    t = pl.program_id(1)
    u = u_ref[0]
    prev_row = jnp.where(t == 0, 0.0, up_ref[0, 7:8, :])
    next_row = jnp.where(t == nt - 1, 0.0, un_ref[0, 0:1, :])
    rid = lax.broadcasted_iota(jnp.int32, u.shape, 0)
    u_m1 = jnp.where(rid == 0, prev_row, pltpu.roll(u, 1, 0))
    u_p1 = jnp.where(rid == ts - 1, next_row, pltpu.roll(u, ts - 1, 0))
    conv = u_m1 * cw_ref[0:1, :] + u * cw_ref[1:2, :] + u_p1 * cw_ref[2:3, :]
    ya = (ab_ref[0] * conv).astype(BF16)
    tn_dims = (((0,), (0,)), ((), ()))
    acc = jnp.dot(ya, w_ref[0:GROUP], preferred_element_type=F32)
    acc += lax.dot_general(yb_ref[0], w_ref[GROUP:2 * GROUP], tn_dims, preferred_element_type=F32)
    acc += lax.dot_general(yc_ref[0], w_ref[2 * GROUP:3 * GROUP], tn_dims, preferred_element_type=F32)
    acc += jnp.dot(yd_ref[0], w_ref[3 * GROUP:4 * GROUP], preferred_element_type=F32)
    o_ref[0] = x_ref[0] + m_ref[0, 5:6, :] * acc


def _outproj(x, m, ab, u, conv_w, ybT, ycT, yd, w_out):
    B, S, D = x.shape
    ts, _, _ = _tiles(S)
    nt = S // ts
    r8 = ts // 8
    tok = lambda b, t: (b, t, 0)
    return pl.pallas_call(
        functools.partial(_outproj_kernel, ts=ts, nt=nt),
        grid=(B, nt),
        in_specs=[pl.BlockSpec((1, ts, D), tok),
                  pl.BlockSpec((1, N_MOD, D), lambda b, t: (b, 0, 0)),
                  pl.BlockSpec((1, ts, GROUP), tok),
                  pl.BlockSpec((1, ts, GROUP), tok),
                  pl.BlockSpec((1, 8, GROUP), lambda b, t: (b, jnp.maximum(t * r8 - 1, 0), 0)),
                  pl.BlockSpec((1, 8, GROUP), lambda b, t: (b, jnp.minimum((t + 1) * r8, S // 8 - 1), 0)),
                  pl.BlockSpec(conv_w.shape, lambda b, t: (0, 0)),
                  pl.BlockSpec((1, GROUP, ts), lambda b, t: (b, 0, t)),
                  pl.BlockSpec((1, GROUP, ts), lambda b, t: (b, 0, t)),
                  pl.BlockSpec((1, ts, GROUP), tok),
                  pl.BlockSpec(w_out.shape, lambda b, t: (0, 0), pipeline_mode=pl.Buffered(1))],
        out_specs=pl.BlockSpec((1, ts, D), tok),
        out_shape=jax.ShapeDtypeStruct((B, S, D), F32),
        compiler_params=_cparams(("arbitrary", "arbitrary")),
        name="outproj",
    )(x, m, ab, u, u, u, conv_w, ybT, ycT, yd, w_out)


def _rope_tables(S):
    inv = ROPE_THETA ** (-jnp.arange(0, MLA_ROPE, 2, dtype=F32) / MLA_ROPE)
    ang = inv[:, None] * jnp.arange(S, dtype=F32)[None, :]
    return jnp.cos(ang), jnp.sin(ang)


def _alibi_slopes():
    return tuple(float(2.0 ** (-8.0 * (i + 1) / HEADS)) for i in range(HEADS))


def _prep_layer(p, l):
    col = lambda v: v.astype(F32).reshape(-1, 1)
    w_in = p['w_in'][l].astype(BF16)
    nch = p['ffn1_w_gu'].shape[-1] // 2 // FF_CHUNK

    def ffn_w(wgu, wd):
        D = wgu.shape[0]
        g = wgu[:, :nch * FF_CHUNK].reshape(D, nch, FF_CHUNK)
        u = wgu[:, nch * FF_CHUNK:].reshape(D, nch, FF_CHUNK)
        wgu_c = jnp.transpose(jnp.concatenate([g, u], axis=2), (1, 0, 2)).astype(BF16)
        return wgu_c, wd.reshape(nch, FF_CHUNK, D).astype(BF16)

    return dict(
        ffn1=ffn_w(p['ffn1_w_gu'][l], p['ffn1_w_down'][l]),
        ffn2=ffn_w(p['ffn2_w_gu'][l], p['ffn2_w_down'][l]),
        w_nat=jnp.concatenate([w_in[:, :NAT_LO], w_in[:, ATT_HI:]], axis=1),
        w_T=w_in[:, NAT_LO:ATT_HI].T,
        wuqT=p['mla_w_uq'][l].astype(BF16).T,
        wukvT=p['mla_w_ukv'][l].astype(BF16).T,
        gdq=col(p['diff_q_g'][l]), gdk=col(p['diff_k_g'][l]),
        gqa=col(p['mla_q_a_g'][l]), gkva=col(p['mla_kv_a_g'][l]),
        gmq=col(p['mla_q_g'][l]), gmk=col(p['mla_k_g'][l]),
        gsub=col(p['diff_subln_g'][l]),
        lv=p['diff_lambda'][l].astype(F32),
        conv_w=p['conv_w'][l].astype(F32),
        w_out=p['w_out'][l].astype(BF16),
        norm_g=p['norm_g'][l].astype(F32),
    )


def _layer(x, m, w, l, consts):
    rope_cos, rope_sin, ftables = consts
    ng = w['norm_g']
    x = _ffn(x, m, ng[0:1], *w['ffn1'], 0)
    slope_parts = tuple(_bf16_parts(np.float32(s * LOG2E)) for s in _alibi_slopes())
    ab, u, fu, qd, kd, vd, qm, km, vm = _inproj(x, m, ng[1:2], w, rope_cos, rope_sin, slope_parts)
    lambda_init = 0.8 - 0.6 * math.exp(-0.3 * l)
    ybT = _attention(qd, kd, vd, w['gdq'], w['gdk'], w['gsub'], w['lv'],
                     width=DIFF_QK, n_maps=2, alibi=True, lambda_init=lambda_init)
    ycT = _attention(qm, km, vm, w['gmq'], w['gmk'], w['gsub'], w['lv'],
                     width=MLA_QK, n_maps=1, alibi=False, lambda_init=0.0)
    yd = _fourier(fu, ftables)
    x = _outproj(x, m, ab, u, w['conv_w'], ybT, ycT, yd, w['w_out'])
    return _ffn(x, m, ng[2:3], *w['ffn2'], 2)


def _trunk(groups, c_all, p):
    depth = p['w_mod'].shape[0]
    mods = _modulation(c_all, p['w_mod'], p['b_mod'])
    outs = []
    consts = []
    for x, _ in groups:
        S = x.shape[1]
        consts.append(_rope_tables(S) + (_fourier_tables(S),))
    xs = [x for x, _ in groups]
    for l in range(depth):
        w = _prep_layer(p, l)
        for gi, (x0, off) in enumerate(groups):
            B, _, D = x0.shape
            m = mods[l, off:off + B].reshape(B, N_MOD, D)
            xs[gi] = _layer(xs[gi], m, w, l, consts[gi])
    return xs


def kernel(x_prompt, x_sample, c_prompt, c_sample, w_mod, b_mod, norm_g, ffn1_w_gu, ffn1_w_down,
           w_in, conv_w, diff_lambda, diff_q_g, diff_k_g, diff_subln_g, mla_q_a_g, mla_w_uq,
           mla_kv_a_g, mla_w_ukv, mla_q_g, mla_k_g, w_out, ffn2_w_gu, ffn2_w_down):
    p = dict(w_mod=w_mod, b_mod=b_mod, norm_g=norm_g, ffn1_w_gu=ffn1_w_gu, ffn1_w_down=ffn1_w_down,
             w_in=w_in, conv_w=conv_w, diff_lambda=diff_lambda, diff_q_g=diff_q_g, diff_k_g=diff_k_g,
             diff_subln_g=diff_subln_g, mla_q_a_g=mla_q_a_g, mla_w_uq=mla_w_uq, mla_kv_a_g=mla_kv_a_g,
             mla_w_ukv=mla_w_ukv, mla_q_g=mla_q_g, mla_k_g=mla_k_g, w_out=w_out,
             ffn2_w_gu=ffn2_w_gu, ffn2_w_down=ffn2_w_down)
    nb = c_prompt.shape[0] + c_sample.shape[0]
    pad = (-nb) % 8
    c_all = jnp.concatenate([c_prompt, c_sample, jnp.zeros((pad, c_prompt.shape[1]), c_prompt.dtype)])
    y_prompt, y_sample = _trunk([(x_prompt, 0), (x_sample, c_prompt.shape[0])], c_all, p)
    return (y_prompt, y_sample)
```

```python
import functools
import math

import numpy as np
import jax
import jax.numpy as jnp
from jax import lax
from jax.experimental import pallas as pl
from jax.experimental.pallas import tpu as pltpu

F32 = jnp.float32
BF16 = jnp.bfloat16

HEADS = 4
GROUP = 256
DIFF_QK = 32
MLA_NOPE = 64
MLA_ROPE = 32
MLA_QK = MLA_NOPE + MLA_ROPE
MLA_Q_RANK = 256
MLA_KV_RANK = 128
HEAD_V = 64
FOURIER_GROUPS = 4
ROPE_THETA = 10000.0
EPS = 1e-6
N_MOD = 9
ATT_ROWS = 3 * GROUP + MLA_Q_RANK + MLA_KV_RANK + MLA_ROPE
NAT_LO = 3 * GROUP
ATT_HI = NAT_LO + ATT_ROWS

LANES = 128
BF16_SUBLANES = 16
KPAD = 128
V_ROWS = HEAD_V + BF16_SUBLANES
AUG_ROWS = BF16_SUBLANES
FF_CHUNK = 256
ATTN_GROUP = 8
VMEM_LIMIT = 52 * 1024 * 1024
NEG_BIG = -1e30
LOG2E = 1.4426950408889634
FAST_SOFTMAX_BOUND = 25.0
BOUND_MARGIN = 1.02


def _mod3(i):
    three = lambda c: jnp.where(c, 3, 0)
    return i - three(i >= 3) - three(i >= 6) - three(i >= 9)


def _bf16_parts(x):
    parts, r = [], np.float32(x)
    for _ in range(3):
        p = np.float32(np.asarray(r, dtype=jnp.bfloat16))
        parts.append(float(p))
        r = np.float32(r - p)
    return tuple(parts)


def _cparams(sem):
    return pltpu.CompilerParams(dimension_semantics=sem, vmem_limit_bytes=VMEM_LIMIT)


def _tiles(S):
    ts = min(512, S)
    tq = min(512, S)
    tk = min(512, S)
    return ts, tq, tk


def _fourier_split(S):
    n1 = 1 << ((int(math.log2(S)) + 1) // 2)
    return n1, S // n1


def _mod_kernel(c_ref, w_ref, b_ref, o_ref):
    c = c_ref[...]
    s = (c * jax.nn.sigmoid(c)).astype(BF16)
    o_ref[0] = jnp.dot(s, w_ref[0].astype(BF16), preferred_element_type=F32) + b_ref[0]


def _modulation(c_all, w_mod, b_mod):
    L, D, N = w_mod.shape
    bp = c_all.shape[0]
    tn = 1536
    return pl.pallas_call(
        _mod_kernel,
        grid=(L, N // tn),
        in_specs=[pl.BlockSpec((bp, D), lambda l, j: (0, 0)),
                  pl.BlockSpec((1, D, tn), lambda l, j: (l, 0, j)),
                  pl.BlockSpec((1, 1, tn), lambda l, j: (l, 0, j))],
        out_specs=pl.BlockSpec((1, bp, tn), lambda l, j: (l, 0, j)),
        out_shape=jax.ShapeDtypeStruct((L, bp, N), F32),
        compiler_params=_cparams(("arbitrary", "arbitrary")),
        name="modulation",
    )(c_all, w_mod, b_mod.reshape(L, 1, N))


def _modulated_norm(x, m_ref, g_ref, sub):
    shift = m_ref[0, 3 * sub:3 * sub + 1, :]
    scale = m_ref[0, 3 * sub + 1:3 * sub + 2, :]
    ms = jnp.mean(x * x, axis=-1, keepdims=True)
    y = x * lax.rsqrt(ms + EPS) * g_ref[...]
    return y * (1.0 + scale) + shift


def _ffn_kernel(x_ref, m_ref, g_ref, wgu_ref, wd_ref, o_ref, h_scr, acc_scr, *, sub, nch):
    x = x_ref[0]
    h_scr[...] = _modulated_norm(x, m_ref, g_ref, sub).astype(BF16)
    acc_scr[...] = jnp.zeros_like(acc_scr)

    def body(c, carry):
        gu = jnp.dot(h_scr[...], wgu_ref[c], preferred_element_type=F32)
        g = gu[:, :FF_CHUNK]
        u = gu[:, FF_CHUNK:]
        a = (g * jax.nn.sigmoid(g) * u).astype(BF16)
        acc_scr[...] += jnp.dot(a, wd_ref[c], preferred_element_type=F32)
        return carry

    lax.fori_loop(0, nch, body, 0, unroll=True)
    gate = m_ref[0, 3 * sub + 2:3 * sub + 3, :]
    o_ref[0] = x + 0.5 * gate * acc_scr[...]


def _ffn(x, m, g, wgu, wd, sub):
    B, S, D = x.shape
    ts, _, _ = _tiles(S)
    nch = wgu.shape[0]
    const3 = lambda b, t: (0, 0, 0)
    return pl.pallas_call(
        functools.partial(_ffn_kernel, sub=sub, nch=nch),
        grid=(B, S // ts),
        in_specs=[pl.BlockSpec((1, ts, D), lambda b, t: (b, t, 0)),
                  pl.BlockSpec((1, N_MOD, D), lambda b, t: (b, 0, 0)),
                  pl.BlockSpec((1, D), lambda b, t: (0, 0)),
                  pl.BlockSpec(wgu.shape, const3, pipeline_mode=pl.Buffered(1)),
                  pl.BlockSpec(wd.shape, const3, pipeline_mode=pl.Buffered(1))],
        out_specs=pl.BlockSpec((1, ts, D), lambda b, t: (b, t, 0)),
        out_shape=jax.ShapeDtypeStruct((B, S, D), F32),
        scratch_shapes=[pltpu.VMEM((ts, D), BF16), pltpu.VMEM((ts, D), F32)],
        compiler_params=_cparams(("arbitrary", "arbitrary")),
        name="ffn",
    )(x, m, g, wgu, wd)


def _rms_rows(v, gain):
    inv = lax.rsqrt(jnp.mean(v * v, axis=0, keepdims=True) + EPS)
    return v * inv * gain


def _rope_rows(t1, t2, cos, sin):
    return t1 * cos - t2 * sin, t1 * sin + t2 * cos


def _ones_row_tile(tk):
    rid = lax.broadcasted_iota(jnp.int32, (V_ROWS - HEAD_V, tk), 0)
    return jnp.where(rid == 0, 1.0, 0.0).astype(BF16)


def _inproj_kernel(x_ref, m_ref, g_ref, wnat_ref, wT_ref, wuqT_ref, wukvT_ref,
                   gdq_ref, gdk_ref, gqa_ref, gkva_ref, gmq_ref, gmk_ref, cos_ref, sin_ref,
                   ab_ref, u_ref, fu_ref, qd_ref, kd_ref, vd_ref, qm_ref, km_ref, vm_ref,
                   *, ts, tk, slope_parts):
    t = pl.program_id(1)
    hb = _modulated_norm(x_ref[0], m_ref, g_ref, 1).astype(BF16)

    nat = jnp.dot(hb, wnat_ref[...], preferred_element_type=F32)
    ab_ref[0] = nat[:, 0:GROUP]
    u_ref[0] = nat[:, GROUP:2 * GROUP] * nat[:, 2 * GROUP:3 * GROUP]
    fu_ref[0] = nat[:, 3 * GROUP:4 * GROUP].astype(BF16)

    pT = lax.dot_general(wT_ref[...], hb, (((1,), (1,)), ((), ())),
                         preferred_element_type=F32)

    ones_tile = _ones_row_tile(tk)
    pos = t * ts + lax.broadcasted_iota(jnp.int32, (1, ts), 1)
    c = pos & (tk - 1)
    c_hi = ((c >> 8) << 8).astype(F32)
    c_lo = (c & 255).astype(F32)
    rid = lax.broadcasted_iota(jnp.int32, (AUG_ROWS, ts), 0)
    pad_d = jnp.zeros((KPAD - 2 * DIFF_QK - AUG_ROWS, ts), F32)
    pad_m = jnp.zeros((KPAD - MLA_QK, ts), F32)
    gdq = gdq_ref[...] * (DIFF_QK ** -0.5 * LOG2E)
    third = _mod3(rid)
    gdk = gdk_ref[...]

    for h in range(HEADS):
        base = 2 * DIFF_QK * h
        ks = []
        for j in range(2):
            r0 = base + DIFF_QK * j
            qd_ref[0, h, DIFF_QK * j:DIFF_QK * (j + 1), :] = _rms_rows(pT[r0:r0 + DIFF_QK], gdq).astype(BF16)
            ks.append(_rms_rows(pT[GROUP + r0:GROUP + r0 + DIFF_QK], gdk))
        s0, s1, s2 = slope_parts[h]
        part = jnp.where(third == 0, s0, jnp.where(third == 1, s1, s2))
        aug = jnp.where(rid < 6, part, jnp.where(rid < 9, c_hi, jnp.where(rid < 12, c_lo, 0.0)))
        kext = jnp.concatenate(ks + [aug, pad_d], axis=0)
        kd_ref[0, h] = kext.T.astype(BF16)
        v = pT[2 * GROUP + HEAD_V * h:2 * GROUP + HEAD_V * (h + 1)].astype(BF16)
        for kb in range(ts // tk):
            vd_ref[0, h, kb, 0:HEAD_V, :] = v[:, kb * tk:(kb + 1) * tk]
            vd_ref[0, h, kb, HEAD_V:V_ROWS, :] = ones_tile

    o_cq = 3 * GROUP
    o_ckv = o_cq + MLA_Q_RANK
    o_kpe = o_ckv + MLA_KV_RANK
    cqn = _rms_rows(pT[o_cq:o_ckv], gqa_ref[...]).astype(BF16)
    qmT = jnp.dot(wuqT_ref[...], cqn, preferred_element_type=F32)
    ckvn = _rms_rows(pT[o_ckv:o_kpe], gkva_ref[...]).astype(BF16)
    kvT = jnp.dot(wukvT_ref[...], ckvn, preferred_element_type=F32)
    kpe = pT[o_kpe:o_kpe + MLA_ROPE]
    cos = cos_ref[...]
    sin = sin_ref[...]
    gmq = gmq_ref[...] * (MLA_QK ** -0.5 * LOG2E)
    gmk = gmk_ref[...]
    half = MLA_ROPE // 2
    for h in range(HEADS):
        qn = _rms_rows(qmT[MLA_QK * h:MLA_QK * (h + 1)], gmq)
        q1, q2 = _rope_rows(qn[MLA_NOPE:MLA_NOPE + half], qn[MLA_NOPE + half:MLA_QK], cos, sin)
        qm_ref[0, h, 0:MLA_NOPE, :] = qn[0:MLA_NOPE].astype(BF16)
        qm_ref[0, h, MLA_NOPE:MLA_NOPE + half, :] = q1.astype(BF16)
        qm_ref[0, h, MLA_NOPE + half:MLA_QK, :] = q2.astype(BF16)
        kv0 = (MLA_NOPE + HEAD_V) * h
        kn = _rms_rows(jnp.concatenate([kvT[kv0:kv0 + MLA_NOPE], kpe], axis=0), gmk)
        k1, k2 = _rope_rows(kn[MLA_NOPE:MLA_NOPE + half], kn[MLA_NOPE + half:MLA_QK], cos, sin)
        kext = jnp.concatenate([kn[0:MLA_NOPE], k1, k2, pad_m], axis=0)
        km_ref[0, h] = kext.T.astype(BF16)
        v = kvT[kv0 + MLA_NOPE:kv0 + MLA_NOPE + HEAD_V].astype(BF16)
        for kb in range(ts // tk):
            vm_ref[0, h, kb, 0:HEAD_V, :] = v[:, kb * tk:(kb + 1) * tk]
            vm_ref[0, h, kb, HEAD_V:V_ROWS, :] = ones_tile


def _inproj(x, m, g, w, rope_cos, rope_sin, slope_parts):
    B, S, D = x.shape
    ts, _, tk = _tiles(S)
    nkb = S // tk
    c2 = lambda b, t: (0, 0)
    tok = lambda b, t: (b, t, 0)
    head_T = lambda b, t: (b, 0, 0, t)
    head_N = lambda b, t: (b, 0, t, 0)
    head_V = lambda b, t: (b, 0, t, 0, 0)
    small = [w['gdq'], w['gdk'], w['gqa'], w['gkva'], w['gmq'], w['gmk']]
    in_specs = ([pl.BlockSpec((1, ts, D), tok),
                 pl.BlockSpec((1, N_MOD, D), lambda b, t: (b, 0, 0)),
                 pl.BlockSpec((1, D), c2),
                 pl.BlockSpec(w['w_nat'].shape, c2, pipeline_mode=pl.Buffered(1)),
                 pl.BlockSpec(w['w_T'].shape, c2, pipeline_mode=pl.Buffered(1)),
                 pl.BlockSpec(w['wuqT'].shape, c2, pipeline_mode=pl.Buffered(1)),
                 pl.BlockSpec(w['wukvT'].shape, c2, pipeline_mode=pl.Buffered(1))]
                + [pl.BlockSpec(a.shape, c2) for a in small]
                + [pl.BlockSpec((MLA_ROPE // 2, ts), lambda b, t: (0, t))] * 2)
    out_shape = [jax.ShapeDtypeStruct((B, S, GROUP), F32),
                 jax.ShapeDtypeStruct((B, S, GROUP), F32),
                 jax.ShapeDtypeStruct((B, S, GROUP), BF16),
                 jax.ShapeDtypeStruct((B, HEADS, 2 * DIFF_QK, S), BF16),
                 jax.ShapeDtypeStruct((B, HEADS, S, KPAD), BF16),
                 jax.ShapeDtypeStruct((B, HEADS, nkb, V_ROWS, tk), BF16),
                 jax.ShapeDtypeStruct((B, HEADS, MLA_QK, S), BF16),
                 jax.ShapeDtypeStruct((B, HEADS, S, KPAD), BF16),
                 jax.ShapeDtypeStruct((B, HEADS, nkb, V_ROWS, tk), BF16)]
    out_specs = [pl.BlockSpec((1, ts, GROUP), tok),
                 pl.BlockSpec((1, ts, GROUP), tok),
                 pl.BlockSpec((1, ts, GROUP), tok),
                 pl.BlockSpec((1, HEADS, 2 * DIFF_QK, ts), head_T),
                 pl.BlockSpec((1, HEADS, ts, KPAD), head_N),
                 pl.BlockSpec((1, HEADS, ts // tk, V_ROWS, tk), head_V),
                 pl.BlockSpec((1, HEADS, MLA_QK, ts), head_T),
                 pl.BlockSpec((1, HEADS, ts, KPAD), head_N),
                 pl.BlockSpec((1, HEADS, ts // tk, V_ROWS, tk), head_V)]
    return pl.pallas_call(
        functools.partial(_inproj_kernel, ts=ts, tk=tk, slope_parts=slope_parts),
        grid=(B, S // ts),
        in_specs=in_specs,
        out_specs=out_specs,
        out_shape=out_shape,
        compiler_params=_cparams(("arbitrary", "arbitrary")),
        name="inproj",
    )(x, m, g, w['w_nat'], w['w_T'], w['wuqT'], w['wukvT'], *small, rope_cos, rope_sin)


def _attn_kernel(sc_ref, qT_ref, k_ref, vT_ref, gsub_ref, lv_ref, o_ref, rhs_scr,
                 *, n_maps, alibi, online, tq, tk, nkb, group_size, lambda_init):
    h = pl.program_id(1)
    q0 = pl.program_id(2) * tq
    q = qT_ref[0, 0]
    shift = sc_ref[h, 4]

    def key_rows(kb, n):
        return k_ref[0, 0, pl.ds(pl.multiple_of(kb * tk, tk), n * tk), :]

    if alibi:
        slope2 = sc_ref[h, 0]
        lane = lax.broadcasted_iota(jnp.int32, (1, tq), 1)
        r_hi = ((lane >> 8) << 8).astype(F32)
        r_lo = (lane & 255).astype(F32)
        rid = lax.broadcasted_iota(jnp.int32, (AUG_ROWS, tq), 0)
        third = _mod3(rid)
        part = jnp.where(third == 0, sc_ref[h, 1], jnp.where(third == 1, sc_ref[h, 2], sc_ref[h, 3]))
        aug = jnp.where(rid < 3, -r_hi, jnp.where(rid < 6, -r_lo, jnp.where(rid < 12, part, 0.0))).astype(BF16)
        zq = jnp.zeros((DIFF_QK, tq), BF16)
        zp = jnp.zeros((KPAD - 2 * DIFF_QK - AUG_ROWS, tq), BF16)
        rhs_scr[0] = jnp.concatenate([q[0:DIFF_QK], zq, aug, zp], axis=0)
        rhs_scr[1] = jnp.concatenate([zq, q[DIFF_QK:2 * DIFF_QK], aug, zp], axis=0)
        col = lax.broadcasted_iota(jnp.int32, (1, KPAD), 1)
        is_aug = (col >= 2 * DIFF_QK) & (col < 2 * DIFF_QK + AUG_ROWS)
        flip = jnp.where(is_aug, -1.0, 1.0).astype(BF16)
        keep = jnp.ones((1, KPAD), BF16)
        d_idx = lax.shift_right_logical(q0, int(math.log2(tk)))
    else:
        rhs_scr[0] = jnp.concatenate([q, jnp.zeros((KPAD - q.shape[0], tq), BF16)], axis=0)

    def accumulate(j, kb, s, cb, accs, ms):
        vblk = vT_ref[0, 0, kb]
        if online:
            m_new = jnp.maximum(ms[j], jnp.max(s, axis=0, keepdims=True) + cb)
            p = jnp.exp2(s + (cb - m_new)).astype(BF16)
            return (jnp.exp2(ms[j] - m_new) * accs[j] + jnp.dot(vblk, p, preferred_element_type=F32), m_new)
        p = jnp.exp2(s + (cb - shift)).astype(BF16)
        return accs[j] + jnp.dot(vblk, p, preferred_element_type=F32), ms[j]

    def group(g, carry, first):
        accs, ms = carry
        accs, ms = list(accs), list(ms)
        base = g * group_size
        if alibi:
            kbs = [(d_idx + base + u) & (nkb - 1) for u in range(group_size)]
            parts = [key_rows(kb, 1) * jnp.where(kb > d_idx, flip, keep) for kb in kbs]
            if first:
                parts = [parts[0] * flip] + parts
            kgrp = jnp.concatenate(parts, axis=0)
        else:
            kbs = [base + u for u in range(group_size)]
            kgrp = key_rows(base, group_size)
        off = tk if (alibi and first) else 0
        for j in range(n_maps):
            s = jnp.dot(kgrp, rhs_scr[j], preferred_element_type=F32)
            for u, kb in enumerate(kbs):
                s_u = s[off + u * tk:off + (u + 1) * tk]
                if alibi and first and u == 0:
                    s_u = jnp.minimum(s_u, s[0:tk])
                cb = -slope2 * jnp.abs(q0 - kb * tk).astype(F32) if alibi else 0.0
                accs[j], ms[j] = accumulate(j, kb, s_u, cb, accs, ms)
        return tuple(accs), tuple(ms)

    carry = (tuple(jnp.zeros((V_ROWS, tq), F32) for _ in range(n_maps)),
             tuple(jnp.full((1, tq), NEG_BIG, F32) for _ in range(n_maps)))
    first_rest = 0
    if alibi:
        carry = group(0, carry, True)
        first_rest = 1
    carry = lax.fori_loop(first_rest, nkb // group_size, lambda g, c: group(g, c, False), carry)
    accs, _ = carry

    a0 = accs[0]
    o = a0[0:HEAD_V] / a0[HEAD_V:HEAD_V + 1]
    if n_maps == 2:
        a1 = accs[1]
        lv = lv_ref[...]
        lam = (jnp.exp(jnp.sum(lv[0:1] * lv[1:2], axis=1, keepdims=True))
               - jnp.exp(jnp.sum(lv[2:3] * lv[3:4], axis=1, keepdims=True)) + lambda_init)
        o = o - lam * (a1[0:HEAD_V] / a1[HEAD_V:HEAD_V + 1])
        o = _rms_rows(o, gsub_ref[...]) * (1.0 - lambda_init)
    o_ref[0] = o.astype(BF16)


def _attention_call(sc, qT, k, vT, gsub, lv, *, n_maps, alibi, online, lambda_init):
    B, H, kq, S = qT.shape
    _, tq, tk = _tiles(S)
    nkb = S // tk
    assert tq == tk, "one key block per query tile sits on the diagonal"
    group_size = 1 if online else min(ATTN_GROUP, nkb)
    assert nkb % group_size == 0
    kern = functools.partial(_attn_kernel, n_maps=n_maps, alibi=alibi, online=online, tq=tq, tk=tk,
                             nkb=nkb, group_size=group_size, lambda_init=lambda_init)
    n_rhs = n_maps
    name = ("attn_diff" if alibi else "attn_mla") + ("_online" if online else "")
    return pl.pallas_call(
        kern,
        grid=(B, H, S // tq),
        in_specs=[pl.BlockSpec(memory_space=pltpu.SMEM),
                  pl.BlockSpec((1, 1, kq, tq), lambda b, h, i: (b, h, 0, i)),
                  pl.BlockSpec((1, 1, S, KPAD), lambda b, h, i: (b, h, 0, 0)),
                  pl.BlockSpec((1, 1, nkb, V_ROWS, tk), lambda b, h, i: (b, h, 0, 0, 0)),
                  pl.BlockSpec(gsub.shape, lambda b, h, i: (0, 0)),
                  pl.BlockSpec(lv.shape, lambda b, h, i: (0, 0))],
        out_specs=pl.BlockSpec((1, HEAD_V, tq), lambda b, h, i: (b, h, i)),
        out_shape=jax.ShapeDtypeStruct((B, H * HEAD_V, S), BF16),
        scratch_shapes=[pltpu.VMEM((n_rhs, KPAD, tq), BF16)],
        compiler_params=_cparams(("arbitrary", "arbitrary", "arbitrary")),
        name=name,
    )(sc, qT, k, vT, gsub, lv)


def _attention(qT, k, vT, gq, gk, gsub, lv, *, width, n_maps, alibi, lambda_init):
    bound = math.sqrt(width) * BOUND_MARGIN * jnp.max(jnp.abs(gq)) * jnp.max(jnp.abs(gk))
    slopes2 = [np.float32(s * LOG2E) for s in _alibi_slopes()]
    static = jnp.asarray([[s, *_bf16_parts(s)] for s in slopes2], F32)
    sc = jnp.concatenate([static, jnp.full((HEADS, 1), LOG2E, F32) * bound,
                          jnp.zeros((HEADS, 3), F32)], axis=1)
    call = functools.partial(_attention_call, n_maps=n_maps, alibi=alibi, lambda_init=lambda_init)
    return lax.cond(bound <= FAST_SOFTMAX_BOUND,
                    functools.partial(call, online=False),
                    functools.partial(call, online=True),
                    sc, qT, k, vT, gsub, lv)


def _f1_kernel(t_ref, x_ref, o_ref):
    o_ref[0] = jnp.dot(t_ref[...], x_ref[0], preferred_element_type=F32).astype(BF16)


def _f2_kernel(g_ref, a_ref, cs_ref, o_ref, *, tk1, n2, ch, norm):
    for j in range(tk1):
        z = jnp.concatenate([a_ref[0, 0, j], a_ref[0, 1, j]], axis=0)
        x = jnp.dot(g_ref[j], z, preferred_element_type=F32)
        xr = x[:n2].astype(BF16)
        xi = x[n2:].astype(BF16)
        y = (jnp.dot(xr, cs_ref[0:ch], preferred_element_type=F32)
             + jnp.dot(xi, cs_ref[ch:2 * ch], preferred_element_type=F32))
        o_ref[0, :, j * ch:(j + 1) * ch] = (y * norm).astype(BF16)


def _fourier_tables(S):
    n1, n2 = _fourier_split(S)
    a = jnp.arange(n1, dtype=jnp.int32)
    ang1 = (2.0 * math.pi / n1) * ((a[:, None] * a[None, :]) % n1).astype(F32)
    t1 = jnp.concatenate([jnp.cos(ang1), -jnp.sin(ang1)], axis=0).astype(BF16)
    k = a[:, None, None] + n1 * jnp.arange(n2, dtype=jnp.int32)[None, :, None]
    b = jnp.arange(n2, dtype=jnp.int32)[None, None, :]
    ang = (2.0 * math.pi / S) * ((k * b) % S).astype(F32)
    cg, sg = jnp.cos(ang), jnp.sin(ang)
    g = jnp.concatenate([jnp.concatenate([cg, sg], axis=2),
                         jnp.concatenate([-sg, cg], axis=2)], axis=1).astype(BF16)
    gw = GROUP // FOURIER_GROUPS
    c = jnp.arange(GROUP, dtype=jnp.int32)
    same = (c[:, None] // gw) == (c[None, :] // gw)
    angc = (2.0 * math.pi / gw) * (((c[:, None] % gw) * (c[None, :] % gw)) % gw).astype(F32)
    cs = jnp.concatenate([jnp.where(same, jnp.cos(angc), 0.0),
                          jnp.where(same, jnp.sin(angc), 0.0)], axis=0).astype(BF16)
    return t1, g, cs


def _fourier(fu, tables):
    B, S, C = fu.shape
    n1, n2 = _fourier_split(S)
    t1, g, cs = tables
    tn = min(4096, n2 * C)
    a = pl.pallas_call(
        _f1_kernel,
        grid=(B, n2 * C // tn),
        in_specs=[pl.BlockSpec(t1.shape, lambda b, j: (0, 0)),
                  pl.BlockSpec((1, n1, tn), lambda b, j: (b, 0, j))],
        out_specs=pl.BlockSpec((1, 2 * n1, tn), lambda b, j: (b, 0, j)),
        out_shape=jax.ShapeDtypeStruct((B, 2 * n1, n2 * C), BF16),
        compiler_params=_cparams(("arbitrary", "arbitrary")),
        name="fourier_stage1",
    )(t1, fu.reshape(B, n1, n2 * C))
    tk1 = 8
    norm = 1.0 / math.sqrt(S * (GROUP // FOURIER_GROUPS))
    y = pl.pallas_call(
        functools.partial(_f2_kernel, tk1=tk1, n2=n2, ch=C, norm=norm),
        grid=(B, n1 // tk1),
        in_specs=[pl.BlockSpec((tk1, 2 * n2, 2 * n2), lambda b, j: (j, 0, 0)),
                  pl.BlockSpec((1, 2, tk1, n2, C), lambda b, j: (b, 0, j, 0, 0)),
                  pl.BlockSpec(cs.shape, lambda b, j: (0, 0))],
        out_specs=pl.BlockSpec((1, n2, tk1 * C), lambda b, j: (b, 0, j)),
        out_shape=jax.ShapeDtypeStruct((B, n2, n1 * C), BF16),
        compiler_params=_cparams(("arbitrary", "arbitrary")),
        name="fourier_stage2",
    )(g, a.reshape(B, 2, n1, n2, C), cs)
    return y.reshape(B, S, C)


def _outproj_kernel(x_ref, m_ref, ab_ref, u_ref, up_ref, un_ref, cw_ref, yb_ref, yc_ref, yd_ref,
                    w_ref, o_ref, *, ts, nt):
    t = pl.program_id(1)
    u = u_ref[0]
    prev_row = jnp.where(t == 0, 0.0, up_ref[0, 7:8, :])
    next_row = jnp.where(t == nt - 1, 0.0, un_ref[0, 0:1, :])
    rid = lax.broadcasted_iota(jnp.int32, u.shape, 0)
    u_m1 = jnp.where(rid == 0, prev_row, pltpu.roll(u, 1, 0))
    u_p1 = jnp.where(rid == ts - 1, next_row, pltpu.roll(u, ts - 1, 0))
    conv = u_m1 * cw_ref[0:1, :] + u * cw_ref[1:2, :] + u_p1 * cw_ref[2:3, :]
    ya = (ab_ref[0] * conv).astype(BF16)
    tn_dims = (((0,), (0,)), ((), ()))
    acc = jnp.dot(ya, w_ref[0:GROUP], preferred_element_type=F32)
    acc += lax.dot_general(yb_ref[0], w_ref[GROUP:2 * GROUP], tn_dims, preferred_element_type=F32)
    acc += lax.dot_general(yc_ref[0], w_ref[2 * GROUP:3 * GROUP], tn_dims, preferred_element_type=F32)
    acc += jnp.dot(yd_ref[0], w_ref[3 * GROUP:4 * GROUP], preferred_element_type=F32)
    o_ref[0] = x_ref[0] + m_ref[0, 5:6, :] * acc


def _outproj(x, m, ab, u, conv_w, ybT, ycT, yd, w_out):
    B, S, D = x.shape
    ts, _, _ = _tiles(S)
    nt = S // ts
    r8 = ts // 8
    tok = lambda b, t: (b, t, 0)
    return pl.pallas_call(
        functools.partial(_outproj_kernel, ts=ts, nt=nt),
        grid=(B, nt),
        in_specs=[pl.BlockSpec((1, ts, D), tok),
                  pl.BlockSpec((1, N_MOD, D), lambda b, t: (b, 0, 0)),
                  pl.BlockSpec((1, ts, GROUP), tok),
                  pl.BlockSpec((1, ts, GROUP), tok),
                  pl.BlockSpec((1, 8, GROUP), lambda b, t: (b, jnp.maximum(t * r8 - 1, 0), 0)),
                  pl.BlockSpec((1, 8, GROUP), lambda b, t: (b, jnp.minimum((t + 1) * r8, S // 8 - 1), 0)),
                  pl.BlockSpec(conv_w.shape, lambda b, t: (0, 0)),
                  pl.BlockSpec((1, GROUP, ts), lambda b, t: (b, 0, t)),
                  pl.BlockSpec((1, GROUP, ts), lambda b, t: (b, 0, t)),
                  pl.BlockSpec((1, ts, GROUP), tok),
                  pl.BlockSpec(w_out.shape, lambda b, t: (0, 0), pipeline_mode=pl.Buffered(1))],
        out_specs=pl.BlockSpec((1, ts, D), tok),
        out_shape=jax.ShapeDtypeStruct((B, S, D), F32),
        compiler_params=_cparams(("arbitrary", "arbitrary")),
        name="outproj",
    )(x, m, ab, u, u, u, conv_w, ybT, ycT, yd, w_out)


def _rope_tables(S):
    inv = ROPE_THETA ** (-jnp.arange(0, MLA_ROPE, 2, dtype=F32) / MLA_ROPE)
    ang = inv[:, None] * jnp.arange(S, dtype=F32)[None, :]
    return jnp.cos(ang), jnp.sin(ang)


def _alibi_slopes():
    return tuple(float(2.0 ** (-8.0 * (i + 1) / HEADS)) for i in range(HEADS))


def _prep_layer(p, l):
    col = lambda v: v.astype(F32).reshape(-1, 1)
    w_in = p['w_in'][l].astype(BF16)
    nch = p['ffn1_w_gu'].shape[-1] // 2 // FF_CHUNK

    def ffn_w(wgu, wd):
        D = wgu.shape[0]
        g = wgu[:, :nch * FF_CHUNK].reshape(D, nch, FF_CHUNK)
        u = wgu[:, nch * FF_CHUNK:].reshape(D, nch, FF_CHUNK)
        wgu_c = jnp.transpose(jnp.concatenate([g, u], axis=2), (1, 0, 2)).astype(BF16)
        return wgu_c, wd.reshape(nch, FF_CHUNK, D).astype(BF16)

    return dict(
        ffn1=ffn_w(p['ffn1_w_gu'][l], p['ffn1_w_down'][l]),
        ffn2=ffn_w(p['ffn2_w_gu'][l], p['ffn2_w_down'][l]),
        w_nat=jnp.concatenate([w_in[:, :NAT_LO], w_in[:, ATT_HI:]], axis=1),
        w_T=w_in[:, NAT_LO:ATT_HI].T,
        wuqT=p['mla_w_uq'][l].astype(BF16).T,
        wukvT=p['mla_w_ukv'][l].astype(BF16).T,
        gdq=col(p['diff_q_g'][l]), gdk=col(p['diff_k_g'][l]),
        gqa=col(p['mla_q_a_g'][l]), gkva=col(p['mla_kv_a_g'][l]),
        gmq=col(p['mla_q_g'][l]), gmk=col(p['mla_k_g'][l]),
        gsub=col(p['diff_subln_g'][l]),
        lv=p['diff_lambda'][l].astype(F32),
        conv_w=p['conv_w'][l].astype(F32),
        w_out=p['w_out'][l].astype(BF16),
        norm_g=p['norm_g'][l].astype(F32),
    )


def _layer(x, m, w, l, consts):
    rope_cos, rope_sin, ftables = consts
    ng = w['norm_g']
    x = _ffn(x, m, ng[0:1], *w['ffn1'], 0)
    slope_parts = tuple(_bf16_parts(np.float32(s * LOG2E)) for s in _alibi_slopes())
    ab, u, fu, qd, kd, vd, qm, km, vm = _inproj(x, m, ng[1:2], w, rope_cos, rope_sin, slope_parts)
    lambda_init = 0.8 - 0.6 * math.exp(-0.3 * l)
    ybT = _attention(qd, kd, vd, w['gdq'], w['gdk'], w['gsub'], w['lv'],
                     width=DIFF_QK, n_maps=2, alibi=True, lambda_init=lambda_init)
    ycT = _attention(qm, km, vm, w['gmq'], w['gmk'], w['gsub'], w['lv'],
                     width=MLA_QK, n_maps=1, alibi=False, lambda_init=0.0)
    yd = _fourier(fu, ftables)
    x = _outproj(x, m, ab, u, w['conv_w'], ybT, ycT, yd, w['w_out'])
    return _ffn(x, m, ng[2:3], *w['ffn2'], 2)


def _trunk(groups, c_all, p):
    depth = p['w_mod'].shape[0]
    mods = _modulation(c_all, p['w_mod'], p['b_mod'])
    outs = []
    consts = []
    for x, _ in groups:
        S = x.shape[1]
        consts.append(_rope_tables(S) + (_fourier_tables(S),))
    xs = [x for x, _ in groups]
    for l in range(depth):
        w = _prep_layer(p, l)
        for gi, (x0, off) in enumerate(groups):
            B, _, D = x0.shape
            m = mods[l, off:off + B].reshape(B, N_MOD, D)
            xs[gi] = _layer(xs[gi], m, w, l, consts[gi])
    return xs


def kernel(x_prompt, x_sample, c_prompt, c_sample, w_mod, b_mod, norm_g, ffn1_w_gu, ffn1_w_down,
           w_in, conv_w, diff_lambda, diff_q_g, diff_k_g, diff_subln_g, mla_q_a_g, mla_w_uq,
           mla_kv_a_g, mla_w_ukv, mla_q_g, mla_k_g, w_out, ffn2_w_gu, ffn2_w_down):
    p = dict(w_mod=w_mod, b_mod=b_mod, norm_g=norm_g, ffn1_w_gu=ffn1_w_gu, ffn1_w_down=ffn1_w_down,
             w_in=w_in, conv_w=conv_w, diff_lambda=diff_lambda, diff_q_g=diff_q_g, diff_k_g=diff_k_g,
             diff_subln_g=diff_subln_g, mla_q_a_g=mla_q_a_g, mla_w_uq=mla_w_uq, mla_kv_a_g=mla_kv_a_g,
             mla_w_ukv=mla_w_ukv, mla_q_g=mla_q_g, mla_k_g=mla_k_g, w_out=w_out,
             ffn2_w_gu=ffn2_w_gu, ffn2_w_down=ffn2_w_down)
    nb = c_prompt.shape[0] + c_sample.shape[0]
    pad = (-nb) % 8
    c_all = jnp.concatenate([c_prompt, c_sample, jnp.zeros((pad, c_prompt.shape[1]), c_prompt.dtype)])
    y_prompt, y_sample = _trunk([(x_prompt, 0), (x_sample, c_prompt.shape[0])], c_all, p)
    return (y_prompt, y_sample)
```

```python
import functools
import math

import numpy as np
import jax
import jax.numpy as jnp
from jax import lax
from jax.experimental import pallas as pl
from jax.experimental.pallas import tpu as pltpu

F32 = jnp.float32
BF16 = jnp.bfloat16

HEADS = 4
GROUP = 256
DIFF_QK = 32
MLA_NOPE = 64
MLA_ROPE = 32
MLA_QK = MLA_NOPE + MLA_ROPE
MLA_Q_RANK = 256
MLA_KV_RANK = 128
HEAD_V = 64
FOURIER_GROUPS = 4
ROPE_THETA = 10000.0
EPS = 1e-6
N_MOD = 9
ATT_ROWS = 3 * GROUP + MLA_Q_RANK + MLA_KV_RANK + MLA_ROPE
NAT_LO = 3 * GROUP
ATT_HI = NAT_LO + ATT_ROWS

LANES = 128
BF16_SUBLANES = 16
KPAD = 128
V_ROWS = HEAD_V + BF16_SUBLANES
AUG_ROWS = BF16_SUBLANES
FF_CHUNK = 256
FFN_TOKENS = 512
ATTN_GROUP = 8
VMEM_LIMIT = 52 * 1024 * 1024
NEG_BIG = -1e30
LOG2E = 1.4426950408889634
FAST_SOFTMAX_BOUND = 25.0
BOUND_MARGIN = 1.02


def _mod3(i):
    three = lambda c: jnp.where(c, 3, 0)
    return i - three(i >= 3) - three(i >= 6) - three(i >= 9)


def _bf16_parts(x):
    parts, r = [], np.float32(x)
    for _ in range(3):
        p = np.float32(np.asarray(r, dtype=jnp.bfloat16))
        parts.append(float(p))
        r = np.float32(r - p)
    return tuple(parts)


def _cparams(sem):
    return pltpu.CompilerParams(dimension_semantics=sem, vmem_limit_bytes=VMEM_LIMIT)


def _tiles(S):
    ts = min(512, S)
    tq = min(512, S)
    tk = min(512, S)
    return ts, tq, tk


def _fourier_split(S):
    n1 = 1 << ((int(math.log2(S)) + 1) // 2)
    return n1, S // n1


def _mod_kernel(c_ref, w_ref, b_ref, o_ref):
    c = c_ref[...]
    s = (c * jax.nn.sigmoid(c)).astype(BF16)
    o_ref[0] = jnp.dot(s, w_ref[0].astype(BF16), preferred_element_type=F32) + b_ref[0]


def _modulation(c_all, w_mod, b_mod):
    L, D, N = w_mod.shape
    bp = c_all.shape[0]
    tn = 1536
    return pl.pallas_call(
        _mod_kernel,
        grid=(L, N // tn),
        in_specs=[pl.BlockSpec((bp, D), lambda l, j: (0, 0)),
                  pl.BlockSpec((1, D, tn), lambda l, j: (l, 0, j)),
                  pl.BlockSpec((1, 1, tn), lambda l, j: (l, 0, j))],
        out_specs=pl.BlockSpec((1, bp, tn), lambda l, j: (l, 0, j)),
        out_shape=jax.ShapeDtypeStruct((L, bp, N), F32),
        compiler_params=_cparams(("arbitrary", "arbitrary")),
        name="modulation",
    )(c_all, w_mod, b_mod.reshape(L, 1, N))


def _modulated_norm(x, m_ref, g_ref, sub):
    shift = m_ref[0, 3 * sub:3 * sub + 1, :]
    scale = m_ref[0, 3 * sub + 1:3 * sub + 2, :]
    ms = jnp.mean(x * x, axis=-1, keepdims=True)
    y = x * lax.rsqrt(ms + EPS) * g_ref[...]
    return y * (1.0 + scale) + shift


def _ffn_kernel(x_ref, m_ref, g_ref, wgu_ref, wd_ref, o_ref, h_scr, acc_scr, *, sub, nch):
    x = x_ref[0]
    h_scr[...] = _modulated_norm(x, m_ref, g_ref, sub).astype(BF16)
    acc_scr[...] = jnp.zeros_like(acc_scr)

    def body(c, carry):
        gu = jnp.dot(h_scr[...], wgu_ref[c], preferred_element_type=F32)
        g = gu[:, :FF_CHUNK]
        u = gu[:, FF_CHUNK:]
        a = (g * jax.nn.sigmoid(g) * u).astype(BF16)
        acc_scr[...] += jnp.dot(a, wd_ref[c], preferred_element_type=F32)
        return carry

    lax.fori_loop(0, nch, body, 0, unroll=True)
    gate = m_ref[0, 3 * sub + 2:3 * sub + 3, :]
    o_ref[0] = x + 0.5 * gate * acc_scr[...]


def _ffn(x, m, g, wgu, wd, sub):
    B, S, D = x.shape
    ts = min(FFN_TOKENS, S)
    nch = wgu.shape[0]
    const3 = lambda b, t: (0, 0, 0)
    return pl.pallas_call(
        functools.partial(_ffn_kernel, sub=sub, nch=nch),
        grid=(B, S // ts),
        in_specs=[pl.BlockSpec((1, ts, D), lambda b, t: (b, t, 0)),
                  pl.BlockSpec((1, N_MOD, D), lambda b, t: (b, 0, 0)),
                  pl.BlockSpec((1, D), lambda b, t: (0, 0)),
                  pl.BlockSpec(wgu.shape, const3, pipeline_mode=pl.Buffered(1)),
                  pl.BlockSpec(wd.shape, const3, pipeline_mode=pl.Buffered(1))],
        out_specs=pl.BlockSpec((1, ts, D), lambda b, t: (b, t, 0)),
        out_shape=jax.ShapeDtypeStruct((B, S, D), F32),
        scratch_shapes=[pltpu.VMEM((ts, D), BF16), pltpu.VMEM((ts, D), F32)],
        compiler_params=_cparams(("arbitrary", "arbitrary")),
        name="ffn",
    )(x, m, g, wgu, wd)


def _rms_rows(v, gain):
    inv = lax.rsqrt(jnp.mean(v * v, axis=0, keepdims=True) + EPS)
    return v * inv * gain


def _rope_rows(t1, t2, cos, sin):
    return t1 * cos - t2 * sin, t1 * sin + t2 * cos


def _ones_row_tile(tk):
    rid = lax.broadcasted_iota(jnp.int32, (V_ROWS - HEAD_V, tk), 0)
    return jnp.where(rid == 0, 1.0, 0.0).astype(BF16)


def _inproj_kernel(x_ref, m_ref, g_ref, wnat_ref, wT_ref, wuqT_ref, wukvT_ref,
                   gdq_ref, gdk_ref, gqa_ref, gkva_ref, gmq_ref, gmk_ref, cos_ref, sin_ref,
                   ab_ref, u_ref, fu_ref, qd_ref, kd_ref, vd_ref, qm_ref, km_ref, vm_ref,
                   *, ts, tk, slope_parts):
    t = pl.program_id(1)
    hb = _modulated_norm(x_ref[0], m_ref, g_ref, 1).astype(BF16)

    nat = jnp.dot(hb, wnat_ref[...], preferred_element_type=F32)
    ab_ref[0] = nat[:, 0:GROUP]
    u_ref[0] = nat[:, GROUP:2 * GROUP] * nat[:, 2 * GROUP:3 * GROUP]
    fu_ref[0] = nat[:, 3 * GROUP:4 * GROUP].astype(BF16)

    pT = lax.dot_general(wT_ref[...], hb, (((1,), (1,)), ((), ())),
                         preferred_element_type=F32)

    ones_tile = _ones_row_tile(tk)
    pos = t * ts + lax.broadcasted_iota(jnp.int32, (1, ts), 1)
    c = pos & (tk - 1)
    c_hi = ((c >> 8) << 8).astype(F32)
    c_lo = (c & 255).astype(F32)
    rid = lax.broadcasted_iota(jnp.int32, (AUG_ROWS, ts), 0)
    pad_d = jnp.zeros((KPAD - 2 * DIFF_QK - AUG_ROWS, ts), F32)
    pad_m = jnp.zeros((KPAD - MLA_QK, ts), F32)
    gdq = gdq_ref[...] * (DIFF_QK ** -0.5 * LOG2E)
    third = _mod3(rid)
    gdk = gdk_ref[...]

    for h in range(HEADS):
        base = 2 * DIFF_QK * h
        ks = []
        for j in range(2):
            r0 = base + DIFF_QK * j
            qd_ref[0, h, DIFF_QK * j:DIFF_QK * (j + 1), :] = _rms_rows(pT[r0:r0 + DIFF_QK], gdq).astype(BF16)
            ks.append(_rms_rows(pT[GROUP + r0:GROUP + r0 + DIFF_QK], gdk))
        s0, s1, s2 = slope_parts[h]
        part = jnp.where(third == 0, s0, jnp.where(third == 1, s1, s2))
        aug = jnp.where(rid < 6, part, jnp.where(rid < 9, c_hi, jnp.where(rid < 12, c_lo, 0.0)))
        kext = jnp.concatenate(ks + [aug, pad_d], axis=0)
        kd_ref[0, h] = kext.T.astype(BF16)
        v = pT[2 * GROUP + HEAD_V * h:2 * GROUP + HEAD_V * (h + 1)].astype(BF16)
        for kb in range(ts // tk):
            vd_ref[0, h, kb, 0:HEAD_V, :] = v[:, kb * tk:(kb + 1) * tk]
            vd_ref[0, h, kb, HEAD_V:V_ROWS, :] = ones_tile

    o_cq = 3 * GROUP
    o_ckv = o_cq + MLA_Q_RANK
    o_kpe = o_ckv + MLA_KV_RANK
    cqn = _rms_rows(pT[o_cq:o_ckv], gqa_ref[...]).astype(BF16)
    qmT = jnp.dot(wuqT_ref[...], cqn, preferred_element_type=F32)
    ckvn = _rms_rows(pT[o_ckv:o_kpe], gkva_ref[...]).astype(BF16)
    kvT = jnp.dot(wukvT_ref[...], ckvn, preferred_element_type=F32)
    kpe = pT[o_kpe:o_kpe + MLA_ROPE]
    cos = cos_ref[...]
    sin = sin_ref[...]
    gmq = gmq_ref[...] * (MLA_QK ** -0.5 * LOG2E)
    gmk = gmk_ref[...]
    half = MLA_ROPE // 2
    for h in range(HEADS):
        qn = _rms_rows(qmT[MLA_QK * h:MLA_QK * (h + 1)], gmq)
        q1, q2 = _rope_rows(qn[MLA_NOPE:MLA_NOPE + half], qn[MLA_NOPE + half:MLA_QK], cos, sin)
        qm_ref[0, h, 0:MLA_NOPE, :] = qn[0:MLA_NOPE].astype(BF16)
        qm_ref[0, h, MLA_NOPE:MLA_NOPE + half, :] = q1.astype(BF16)
        qm_ref[0, h, MLA_NOPE + half:MLA_QK, :] = q2.astype(BF16)
        kv0 = (MLA_NOPE + HEAD_V) * h
        kn = _rms_rows(jnp.concatenate([kvT[kv0:kv0 + MLA_NOPE], kpe], axis=0), gmk)
        k1, k2 = _rope_rows(kn[MLA_NOPE:MLA_NOPE + half], kn[MLA_NOPE + half:MLA_QK], cos, sin)
        kext = jnp.concatenate([kn[0:MLA_NOPE], k1, k2, pad_m], axis=0)
        km_ref[0, h] = kext.T.astype(BF16)
        v = kvT[kv0 + MLA_NOPE:kv0 + MLA_NOPE + HEAD_V].astype(BF16)
        for kb in range(ts // tk):
            vm_ref[0, h, kb, 0:HEAD_V, :] = v[:, kb * tk:(kb + 1) * tk]
            vm_ref[0, h, kb, HEAD_V:V_ROWS, :] = ones_tile


def _inproj(x, m, g, w, rope_cos, rope_sin, slope_parts):
    B, S, D = x.shape
    ts, _, tk = _tiles(S)
    nkb = S // tk
    c2 = lambda b, t: (0, 0)
    tok = lambda b, t: (b, t, 0)
    head_T = lambda b, t: (b, 0, 0, t)
    head_N = lambda b, t: (b, 0, t, 0)
    head_V = lambda b, t: (b, 0, t, 0, 0)
    small = [w['gdq'], w['gdk'], w['gqa'], w['gkva'], w['gmq'], w['gmk']]
    in_specs = ([pl.BlockSpec((1, ts, D), tok),
                 pl.BlockSpec((1, N_MOD, D), lambda b, t: (b, 0, 0)),
                 pl.BlockSpec((1, D), c2),
                 pl.BlockSpec(w['w_nat'].shape, c2, pipeline_mode=pl.Buffered(1)),
                 pl.BlockSpec(w['w_T'].shape, c2, pipeline_mode=pl.Buffered(1)),
                 pl.BlockSpec(w['wuqT'].shape, c2, pipeline_mode=pl.Buffered(1)),
                 pl.BlockSpec(w['wukvT'].shape, c2, pipeline_mode=pl.Buffered(1))]
                + [pl.BlockSpec(a.shape, c2) for a in small]
                + [pl.BlockSpec((MLA_ROPE // 2, ts), lambda b, t: (0, t))] * 2)
    out_shape = [jax.ShapeDtypeStruct((B, S, GROUP), F32),
                 jax.ShapeDtypeStruct((B, S, GROUP), F32),
                 jax.ShapeDtypeStruct((B, S, GROUP), BF16),
                 jax.ShapeDtypeStruct((B, HEADS, 2 * DIFF_QK, S), BF16),
                 jax.ShapeDtypeStruct((B, HEADS, S, KPAD), BF16),
                 jax.ShapeDtypeStruct((B, HEADS, nkb, V_ROWS, tk), BF16),
                 jax.ShapeDtypeStruct((B, HEADS, MLA_QK, S), BF16),
                 jax.ShapeDtypeStruct((B, HEADS, S, KPAD), BF16),
                 jax.ShapeDtypeStruct((B, HEADS, nkb, V_ROWS, tk), BF16)]
    out_specs = [pl.BlockSpec((1, ts, GROUP), tok),
                 pl.BlockSpec((1, ts, GROUP), tok),
                 pl.BlockSpec((1, ts, GROUP), tok),
                 pl.BlockSpec((1, HEADS, 2 * DIFF_QK, ts), head_T),
                 pl.BlockSpec((1, HEADS, ts, KPAD), head_N),
                 pl.BlockSpec((1, HEADS, ts // tk, V_ROWS, tk), head_V),
                 pl.BlockSpec((1, HEADS, MLA_QK, ts), head_T),
                 pl.BlockSpec((1, HEADS, ts, KPAD), head_N),
                 pl.BlockSpec((1, HEADS, ts // tk, V_ROWS, tk), head_V)]
    return pl.pallas_call(
        functools.partial(_inproj_kernel, ts=ts, tk=tk, slope_parts=slope_parts),
        grid=(B, S // ts),
        in_specs=in_specs,
        out_specs=out_specs,
        out_shape=out_shape,
        compiler_params=_cparams(("arbitrary", "arbitrary")),
        name="inproj",
    )(x, m, g, w['w_nat'], w['w_T'], w['wuqT'], w['wukvT'], *small, rope_cos, rope_sin)


def _attn_query_tile(qi, sc_ref, qT_ref, k_ref, vT_ref, gsub_ref, lv_ref, o_ref, rhs_scr,
                     *, n_maps, alibi, online, tq, tk, nkb, group_size, lambda_init):
    h = pl.program_id(1)
    q0 = pl.multiple_of(qi * tq, tq)
    q = qT_ref[0, 0, :, pl.ds(q0, tq)]
    shift = sc_ref[h, 4]

    def key_rows(kb, n):
        return k_ref[0, 0, pl.ds(pl.multiple_of(kb * tk, tk), n * tk), :]

    if alibi:
        slope2 = sc_ref[h, 0]
        lane = lax.broadcasted_iota(jnp.int32, (1, tq), 1)
        r_hi = ((lane >> 8) << 8).astype(F32)
        r_lo = (lane & 255).astype(F32)
        rid = lax.broadcasted_iota(jnp.int32, (AUG_ROWS, tq), 0)
        third = _mod3(rid)
        part = jnp.where(third == 0, sc_ref[h, 1], jnp.where(third == 1, sc_ref[h, 2], sc_ref[h, 3]))
        aug = jnp.where(rid < 3, -r_hi, jnp.where(rid < 6, -r_lo, jnp.where(rid < 12, part, 0.0))).astype(BF16)
        zq = jnp.zeros((DIFF_QK, tq), BF16)
        zp = jnp.zeros((KPAD - 2 * DIFF_QK - AUG_ROWS, tq), BF16)
        rhs_scr[0] = jnp.concatenate([q[0:DIFF_QK], zq, aug, zp], axis=0)
        rhs_scr[1] = jnp.concatenate([zq, q[DIFF_QK:2 * DIFF_QK], aug, zp], axis=0)
        col = lax.broadcasted_iota(jnp.int32, (1, KPAD), 1)
        is_aug = (col >= 2 * DIFF_QK) & (col < 2 * DIFF_QK + AUG_ROWS)
        flip = jnp.where(is_aug, -1.0, 1.0).astype(BF16)
        keep = jnp.ones((1, KPAD), BF16)
        d_idx = lax.shift_right_logical(q0, int(math.log2(tk)))
    else:
        rhs_scr[0] = jnp.concatenate([q, jnp.zeros((KPAD - q.shape[0], tq), BF16)], axis=0)

    def accumulate(j, kb, s, cb, accs, ms):
        vblk = vT_ref[0, 0, kb]
        if online:
            m_new = jnp.maximum(ms[j], jnp.max(s, axis=0, keepdims=True) + cb)
            p = jnp.exp2(s + (cb - m_new)).astype(BF16)
            return (jnp.exp2(ms[j] - m_new) * accs[j] + jnp.dot(vblk, p, preferred_element_type=F32), m_new)
        p = jnp.exp2(s + (cb - shift)).astype(BF16)
        return accs[j] + jnp.dot(vblk, p, preferred_element_type=F32), ms[j]

    def group(g, carry, first):
        accs, ms = carry
        accs, ms = list(accs), list(ms)
        base = g * group_size
        if alibi:
            kbs = [(d_idx + base + u) & (nkb - 1) for u in range(group_size)]
            parts = [key_rows(kb, 1) * jnp.where(kb > d_idx, flip, keep) for kb in kbs]
            if first:
                parts = [parts[0] * flip] + parts
            kgrp = jnp.concatenate(parts, axis=0)
        else:
            kbs = [base + u for u in range(group_size)]
            kgrp = key_rows(base, group_size)
        off = tk if (alibi and first) else 0
        for j in range(n_maps):
            s = jnp.dot(kgrp, rhs_scr[j], preferred_element_type=F32)
            for u, kb in enumerate(kbs):
                s_u = s[off + u * tk:off + (u + 1) * tk]
                if alibi and first and u == 0:
                    s_u = jnp.minimum(s_u, s[0:tk])
                cb = -slope2 * jnp.abs(q0 - kb * tk).astype(F32) if alibi else 0.0
                accs[j], ms[j] = accumulate(j, kb, s_u, cb, accs, ms)
        return tuple(accs), tuple(ms)

    carry = (tuple(jnp.zeros((V_ROWS, tq), F32) for _ in range(n_maps)),
             tuple(jnp.full((1, tq), NEG_BIG, F32) for _ in range(n_maps)))
    first_rest = 0
    if alibi:
        carry = group(0, carry, True)
        first_rest = 1
    carry = lax.fori_loop(first_rest, nkb // group_size, lambda g, c: group(g, c, False), carry)
    accs, _ = carry

    a0 = accs[0]
    o = a0[0:HEAD_V] / a0[HEAD_V:HEAD_V + 1]
    if n_maps == 2:
        a1 = accs[1]
        lv = lv_ref[...]
        lam = (jnp.exp(jnp.sum(lv[0:1] * lv[1:2], axis=1, keepdims=True))
               - jnp.exp(jnp.sum(lv[2:3] * lv[3:4], axis=1, keepdims=True)) + lambda_init)
        o = o - lam * (a1[0:HEAD_V] / a1[HEAD_V:HEAD_V + 1])
        o = _rms_rows(o, gsub_ref[...]) * (1.0 - lambda_init)
    o_ref[0, :, pl.ds(q0, tq)] = o.astype(BF16)


def _attn_kernel(*refs, nq, **kw):
    def tile(qi, carry):
        _attn_query_tile(qi, *refs, **kw)
        return carry
    lax.fori_loop(0, nq, tile, 0)


def _attention_call(sc, qT, k, vT, gsub, lv, *, n_maps, alibi, online, lambda_init):
    B, H, kq, S = qT.shape
    _, tq, tk = _tiles(S)
    nkb = S // tk
    assert tq == tk, "one key block per query tile sits on the diagonal"
    group_size = 1 if online else min(ATTN_GROUP, nkb)
    assert nkb % group_size == 0
    kern = functools.partial(_attn_kernel, n_maps=n_maps, alibi=alibi, online=online, tq=tq, tk=tk,
                             nkb=nkb, group_size=group_size, lambda_init=lambda_init, nq=S // tq)
    n_rhs = n_maps
    name = ("attn_diff" if alibi else "attn_mla") + ("_online" if online else "")
    return pl.pallas_call(
        kern,
        grid=(B, H),
        in_specs=[pl.BlockSpec(memory_space=pltpu.SMEM),
                  pl.BlockSpec((1, 1, kq, S), lambda b, h: (b, h, 0, 0)),
                  pl.BlockSpec((1, 1, S, KPAD), lambda b, h: (b, h, 0, 0)),
                  pl.BlockSpec((1, 1, nkb, V_ROWS, tk), lambda b, h: (b, h, 0, 0, 0)),
                  pl.BlockSpec(gsub.shape, lambda b, h: (0, 0)),
                  pl.BlockSpec(lv.shape, lambda b, h: (0, 0))],
        out_specs=pl.BlockSpec((1, HEAD_V, S), lambda b, h: (b, h, 0)),
        out_shape=jax.ShapeDtypeStruct((B, H * HEAD_V, S), BF16),
        scratch_shapes=[pltpu.VMEM((n_rhs, KPAD, tq), BF16)],
        compiler_params=_cparams(("arbitrary", "arbitrary")),
        name=name,
    )(sc, qT, k, vT, gsub, lv)


def _attention(qT, k, vT, gq, gk, gsub, lv, *, width, n_maps, alibi, lambda_init):
    bound = math.sqrt(width) * BOUND_MARGIN * jnp.max(jnp.abs(gq)) * jnp.max(jnp.abs(gk))
    slopes2 = [np.float32(s * LOG2E) for s in _alibi_slopes()]
    static = jnp.asarray([[s, *_bf16_parts(s)] for s in slopes2], F32)
    sc = jnp.concatenate([static, jnp.full((HEADS, 1), LOG2E, F32) * bound,
                          jnp.zeros((HEADS, 3), F32)], axis=1)
    call = functools.partial(_attention_call, n_maps=n_maps, alibi=alibi, lambda_init=lambda_init)
    return lax.cond(bound <= FAST_SOFTMAX_BOUND,
                    functools.partial(call, online=False),
                    functools.partial(call, online=True),
                    sc, qT, k, vT, gsub, lv)


def _f1_kernel(t_ref, x_ref, o_ref):
    o_ref[0] = jnp.dot(t_ref[...], x_ref[0], preferred_element_type=F32).astype(BF16)


def _f2_kernel(g_ref, a_ref, cs_ref, o_ref, *, tk1, n2, ch, norm):
    for j in range(tk1):
        z = jnp.concatenate([a_ref[0, 0, j], a_ref[0, 1, j]], axis=0)
        x = jnp.dot(g_ref[j], z, preferred_element_type=F32)
        xr = x[:n2].astype(BF16)
        xi = x[n2:].astype(BF16)
        y = (jnp.dot(xr, cs_ref[0:ch], preferred_element_type=F32)
             + jnp.dot(xi, cs_ref[ch:2 * ch], preferred_element_type=F32))
        o_ref[0, :, j * ch:(j + 1) * ch] = (y * norm).astype(BF16)


def _fourier_tables(S):
    n1, n2 = _fourier_split(S)
    a = jnp.arange(n1, dtype=jnp.int32)
    ang1 = (2.0 * math.pi / n1) * ((a[:, None] * a[None, :]) % n1).astype(F32)
    t1 = jnp.concatenate([jnp.cos(ang1), -jnp.sin(ang1)], axis=0).astype(BF16)
    k = a[:, None, None] + n1 * jnp.arange(n2, dtype=jnp.int32)[None, :, None]
    b = jnp.arange(n2, dtype=jnp.int32)[None, None, :]
    ang = (2.0 * math.pi / S) * ((k * b) % S).astype(F32)
    cg, sg = jnp.cos(ang), jnp.sin(ang)
    g = jnp.concatenate([jnp.concatenate([cg, sg], axis=2),
                         jnp.concatenate([-sg, cg], axis=2)], axis=1).astype(BF16)
    gw = GROUP // FOURIER_GROUPS
    c = jnp.arange(GROUP, dtype=jnp.int32)
    same = (c[:, None] // gw) == (c[None, :] // gw)
    angc = (2.0 * math.pi / gw) * (((c[:, None] % gw) * (c[None, :] % gw)) % gw).astype(F32)
    cs = jnp.concatenate([jnp.where(same, jnp.cos(angc), 0.0),
                          jnp.where(same, jnp.sin(angc), 0.0)], axis=0).astype(BF16)
    return t1, g, cs


def _fourier(fu, tables):
    B, S, C = fu.shape
    n1, n2 = _fourier_split(S)
    t1, g, cs = tables
    tn = min(4096, n2 * C)
    a = pl.pallas_call(
        _f1_kernel,
        grid=(B, n2 * C // tn),
        in_specs=[pl.BlockSpec(t1.shape, lambda b, j: (0, 0)),
                  pl.BlockSpec((1, n1, tn), lambda b, j: (b, 0, j))],
        out_specs=pl.BlockSpec((1, 2 * n1, tn), lambda b, j: (b, 0, j)),
        out_shape=jax.ShapeDtypeStruct((B, 2 * n1, n2 * C), BF16),
        compiler_params=_cparams(("arbitrary", "arbitrary")),
        name="fourier_stage1",
    )(t1, fu.reshape(B, n1, n2 * C))
    tk1 = 8
    norm = 1.0 / math.sqrt(S * (GROUP // FOURIER_GROUPS))
    y = pl.pallas_call(
        functools.partial(_f2_kernel, tk1=tk1, n2=n2, ch=C, norm=norm),
        grid=(B, n1 // tk1),
        in_specs=[pl.BlockSpec((tk1, 2 * n2, 2 * n2), lambda b, j: (j, 0, 0)),
                  pl.BlockSpec((1, 2, tk1, n2, C), lambda b, j: (b, 0, j, 0, 0)),
                  pl.BlockSpec(cs.shape, lambda b, j: (0, 0))],
        out_specs=pl.BlockSpec((1, n2, tk1 * C), lambda b, j: (b, 0, j)),
        out_shape=jax.ShapeDtypeStruct((B, n2, n1 * C), BF16),
        compiler_params=_cparams(("arbitrary", "arbitrary")),
        name="fourier_stage2",
    )(g, a.reshape(B, 2, n1, n2, C), cs)
    return y.reshape(B, S, C)


def _outproj_kernel(x_ref, m_ref, ab_ref, u_ref, up_ref, un_ref, cw_ref, yb_ref, yc_ref, yd_ref,
                    w_ref, o_ref, *, ts, nt):
    t = pl.program_id(1)
    u = u_ref[0]
    prev_row = jnp.where(t == 0, 0.0, up_ref[0, 7:8, :])
    next_row = jnp.where(t == nt - 1, 0.0, un_ref[0, 0:1, :])
    rid = lax.broadcasted_iota(jnp.int32, u.shape, 0)
    u_m1 = jnp.where(rid == 0, prev_row, pltpu.roll(u, 1, 0))
    u_p1 = jnp.where(rid == ts - 1, next_row, pltpu.roll(u, ts - 1, 0))
    conv = u_m1 * cw_ref[0:1, :] + u * cw_ref[1:2, :] + u_p1 * cw_ref[2:3, :]
    ya = (ab_ref[0] * conv).astype(BF16)
    tn_dims = (((0,), (0,)), ((), ()))
    acc = jnp.dot(ya, w_ref[0:GROUP], preferred_element_type=F32)
    acc += lax.dot_general(yb_ref[0], w_ref[GROUP:2 * GROUP], tn_dims, preferred_element_type=F32)
    acc += lax.dot_general(yc_ref[0], w_ref[2 * GROUP:3 * GROUP], tn_dims, preferred_element_type=F32)
    acc += jnp.dot(yd_ref[0], w_ref[3 * GROUP:4 * GROUP], preferred_element_type=F32)
    o_ref[0] = x_ref[0] + m_ref[0, 5:6, :] * acc


def _outproj(x, m, ab, u, conv_w, ybT, ycT, yd, w_out):
    B, S, D = x.shape
    ts, _, _ = _tiles(S)
    nt = S // ts
    r8 = ts // 8
    tok = lambda b, t: (b, t, 0)
    return pl.pallas_call(
        functools.partial(_outproj_kernel, ts=ts, nt=nt),
        grid=(B, nt),
        in_specs=[pl.BlockSpec((1, ts, D), tok),
                  pl.BlockSpec((1, N_MOD, D), lambda b, t: (b, 0, 0)),
                  pl.BlockSpec((1, ts, GROUP), tok),
                  pl.BlockSpec((1, ts, GROUP), tok),
                  pl.BlockSpec((1, 8, GROUP), lambda b, t: (b, jnp.maximum(t * r8 - 1, 0), 0)),
                  pl.BlockSpec((1, 8, GROUP), lambda b, t: (b, jnp.minimum((t + 1) * r8, S // 8 - 1), 0)),
                  pl.BlockSpec(conv_w.shape, lambda b, t: (0, 0)),
                  pl.BlockSpec((1, GROUP, ts), lambda b, t: (b, 0, t)),
                  pl.BlockSpec((1, GROUP, ts), lambda b, t: (b, 0, t)),
                  pl.BlockSpec((1, ts, GROUP), tok),
                  pl.BlockSpec(w_out.shape, lambda b, t: (0, 0), pipeline_mode=pl.Buffered(1))],
        out_specs=pl.BlockSpec((1, ts, D), tok),
        out_shape=jax.ShapeDtypeStruct((B, S, D), F32),
        compiler_params=_cparams(("arbitrary", "arbitrary")),
        name="outproj",
    )(x, m, ab, u, u, u, conv_w, ybT, ycT, yd, w_out)


def _rope_tables(S):
    inv = ROPE_THETA ** (-jnp.arange(0, MLA_ROPE, 2, dtype=F32) / MLA_ROPE)
    ang = inv[:, None] * jnp.arange(S, dtype=F32)[None, :]
    return jnp.cos(ang), jnp.sin(ang)


def _alibi_slopes():
    return tuple(float(2.0 ** (-8.0 * (i + 1) / HEADS)) for i in range(HEADS))


def _prep_layer(p, l):
    col = lambda v: v.astype(F32).reshape(-1, 1)
    w_in = p['w_in'][l].astype(BF16)
    nch = p['ffn1_w_gu'].shape[-1] // 2 // FF_CHUNK

    def ffn_w(wgu, wd):
        D = wgu.shape[0]
        g = wgu[:, :nch * FF_CHUNK].reshape(D, nch, FF_CHUNK)
        u = wgu[:, nch * FF_CHUNK:].reshape(D, nch, FF_CHUNK)
        wgu_c = jnp.transpose(jnp.concatenate([g, u], axis=2), (1, 0, 2)).astype(BF16)
        return wgu_c, wd.reshape(nch, FF_CHUNK, D).astype(BF16)

    return dict(
        ffn1=ffn_w(p['ffn1_w_gu'][l], p['ffn1_w_down'][l]),
        ffn2=ffn_w(p['ffn2_w_gu'][l], p['ffn2_w_down'][l]),
        w_nat=jnp.concatenate([w_in[:, :NAT_LO], w_in[:, ATT_HI:]], axis=1),
        w_T=w_in[:, NAT_LO:ATT_HI].T,
        wuqT=p['mla_w_uq'][l].astype(BF16).T,
        wukvT=p['mla_w_ukv'][l].astype(BF16).T,
        gdq=col(p['diff_q_g'][l]), gdk=col(p['diff_k_g'][l]),
        gqa=col(p['mla_q_a_g'][l]), gkva=col(p['mla_kv_a_g'][l]),
        gmq=col(p['mla_q_g'][l]), gmk=col(p['mla_k_g'][l]),
        gsub=col(p['diff_subln_g'][l]),
        lv=p['diff_lambda'][l].astype(F32),
        conv_w=p['conv_w'][l].astype(F32),
        w_out=p['w_out'][l].astype(BF16),
        norm_g=p['norm_g'][l].astype(F32),
    )


def _layer(x, m, w, l, consts):
    rope_cos, rope_sin, ftables = consts
    ng = w['norm_g']
    x = _ffn(x, m, ng[0:1], *w['ffn1'], 0)
    slope_parts = tuple(_bf16_parts(np.float32(s * LOG2E)) for s in _alibi_slopes())
    ab, u, fu, qd, kd, vd, qm, km, vm = _inproj(x, m, ng[1:2], w, rope_cos, rope_sin, slope_parts)
    lambda_init = 0.8 - 0.6 * math.exp(-0.3 * l)
    ybT = _attention(qd, kd, vd, w['gdq'], w['gdk'], w['gsub'], w['lv'],
                     width=DIFF_QK, n_maps=2, alibi=True, lambda_init=lambda_init)
    ycT = _attention(qm, km, vm, w['gmq'], w['gmk'], w['gsub'], w['lv'],
                     width=MLA_QK, n_maps=1, alibi=False, lambda_init=0.0)
    yd = _fourier(fu, ftables)
    x = _outproj(x, m, ab, u, w['conv_w'], ybT, ycT, yd, w['w_out'])
    return _ffn(x, m, ng[2:3], *w['ffn2'], 2)


def _trunk(groups, c_all, p):
    depth = p['w_mod'].shape[0]
    mods = _modulation(c_all, p['w_mod'], p['b_mod'])
    outs = []
    consts = []
    for x, _ in groups:
        S = x.shape[1]
        consts.append(_rope_tables(S) + (_fourier_tables(S),))
    xs = [x for x, _ in groups]
    for l in range(depth):
        w = _prep_layer(p, l)
        for gi, (x0, off) in enumerate(groups):
            B, _, D = x0.shape
            m = mods[l, off:off + B].reshape(B, N_MOD, D)
            xs[gi] = _layer(xs[gi], m, w, l, consts[gi])
    return xs


def kernel(x_prompt, x_sample, c_prompt, c_sample, w_mod, b_mod, norm_g, ffn1_w_gu, ffn1_w_down,
           w_in, conv_w, diff_lambda, diff_q_g, diff_k_g, diff_subln_g, mla_q_a_g, mla_w_uq,
           mla_kv_a_g, mla_w_ukv, mla_q_g, mla_k_g, w_out, ffn2_w_gu, ffn2_w_down):
    p = dict(w_mod=w_mod, b_mod=b_mod, norm_g=norm_g, ffn1_w_gu=ffn1_w_gu, ffn1_w_down=ffn1_w_down,
             w_in=w_in, conv_w=conv_w, diff_lambda=diff_lambda, diff_q_g=diff_q_g, diff_k_g=diff_k_g,
             diff_subln_g=diff_subln_g, mla_q_a_g=mla_q_a_g, mla_w_uq=mla_w_uq, mla_kv_a_g=mla_kv_a_g,
             mla_w_ukv=mla_w_ukv, mla_q_g=mla_q_g, mla_k_g=mla_k_g, w_out=w_out,
             ffn2_w_gu=ffn2_w_gu, ffn2_w_down=ffn2_w_down)
    nb = c_prompt.shape[0] + c_sample.shape[0]
    pad = (-nb) % 8
    c_all = jnp.concatenate([c_prompt, c_sample, jnp.zeros((pad, c_prompt.shape[1]), c_prompt.dtype)])
    y_prompt, y_sample = _trunk([(x_prompt, 0), (x_sample, c_prompt.shape[0])], c_all, p)
    return (y_prompt, y_sample)
```

```python
import functools
import math

import numpy as np
import jax
import jax.numpy as jnp
from jax import lax
from jax.experimental import pallas as pl
from jax.experimental.pallas import tpu as pltpu

F32 = jnp.float32
BF16 = jnp.bfloat16

HEADS = 4
GROUP = 256
DIFF_QK = 32
MLA_NOPE = 64
MLA_ROPE = 32
MLA_QK = MLA_NOPE + MLA_ROPE
MLA_Q_RANK = 256
MLA_KV_RANK = 128
HEAD_V = 64
FOURIER_GROUPS = 4
ROPE_THETA = 10000.0
EPS = 1e-6
N_MOD = 9
ATT_ROWS = 3 * GROUP + MLA_Q_RANK + MLA_KV_RANK + MLA_ROPE
NAT_LO = 3 * GROUP
ATT_HI = NAT_LO + ATT_ROWS

LANES = 128
BF16_SUBLANES = 16
KPAD = 128
V_ROWS = HEAD_V + BF16_SUBLANES
AUG_ROWS = BF16_SUBLANES
FF_CHUNK = 256
FFN_TOKENS = 512
ATTN_GROUP = 8
VMEM_LIMIT = 52 * 1024 * 1024
NEG_BIG = -1e30
LOG2E = 1.4426950408889634
FAST_SOFTMAX_BOUND = 25.0
BOUND_MARGIN = 1.02


def _mod3(i):
    three = lambda c: jnp.where(c, 3, 0)
    return i - three(i >= 3) - three(i >= 6) - three(i >= 9)


def _bf16_parts(x):
    parts, r = [], np.float32(x)
    for _ in range(3):
        p = np.float32(np.asarray(r, dtype=jnp.bfloat16))
        parts.append(float(p))
        r = np.float32(r - p)
    return tuple(parts)


def _cparams(sem):
    return pltpu.CompilerParams(dimension_semantics=sem, vmem_limit_bytes=VMEM_LIMIT)


def _tiles(S):
    ts = min(512, S)
    tq = min(512, S)
    tk = min(512, S)
    return ts, tq, tk


def _fourier_split(S):
    n1 = 1 << ((int(math.log2(S)) + 1) // 2)
    return n1, S // n1


def _mod_kernel(c_ref, w_ref, b_ref, o_ref):
    c = c_ref[...]
    s = (c * jax.nn.sigmoid(c)).astype(BF16)
    o_ref[0] = jnp.dot(s, w_ref[0].astype(BF16), preferred_element_type=F32) + b_ref[0]


def _modulation(c_all, w_mod, b_mod):
    L, D, N = w_mod.shape
    bp = c_all.shape[0]
    tn = 1536
    return pl.pallas_call(
        _mod_kernel,
        grid=(L, N // tn),
        in_specs=[pl.BlockSpec((bp, D), lambda l, j: (0, 0)),
                  pl.BlockSpec((1, D, tn), lambda l, j: (l, 0, j)),
                  pl.BlockSpec((1, 1, tn), lambda l, j: (l, 0, j))],
        out_specs=pl.BlockSpec((1, bp, tn), lambda l, j: (l, 0, j)),
        out_shape=jax.ShapeDtypeStruct((L, bp, N), F32),
        compiler_params=_cparams(("arbitrary", "arbitrary")),
        name="modulation",
    )(c_all, w_mod, b_mod.reshape(L, 1, N))


def _modulated_norm(x, m_ref, g_ref, sub):
    shift = m_ref[0, 3 * sub:3 * sub + 1, :]
    scale = m_ref[0, 3 * sub + 1:3 * sub + 2, :]
    ms = jnp.mean(x * x, axis=-1, keepdims=True)
    y = x * lax.rsqrt(ms + EPS) * g_ref[...]
    return y * (1.0 + scale) + shift


def _ffn_tile(x, m_ref, g_ref, wgu_ref, wd_ref, h_scr, acc_scr, sub, nch):
    h_scr[...] = _modulated_norm(x, m_ref, g_ref, sub).astype(BF16)
    acc_scr[...] = jnp.zeros_like(acc_scr)

    def body(c, carry):
        gu = jnp.dot(h_scr[...], wgu_ref[c], preferred_element_type=F32)
        g = gu[:, :FF_CHUNK]
        u = gu[:, FF_CHUNK:]
        a = (g * jax.nn.sigmoid(g) * u).astype(BF16)
        acc_scr[...] += jnp.dot(a, wd_ref[c], preferred_element_type=F32)
        return carry

    lax.fori_loop(0, nch, body, 0, unroll=True)
    gate = m_ref[0, 3 * sub + 2:3 * sub + 3, :]
    return x + 0.5 * gate * acc_scr[...]


def _ffn_kernel(x_ref, m_ref, g_ref, wgu_ref, wd_ref, o_ref, h_scr, acc_scr, *, sub, nch):
    o_ref[0] = _ffn_tile(x_ref[0], m_ref, g_ref, wgu_ref, wd_ref, h_scr, acc_scr, sub, nch)


def _ffn(x, m, g, wgu, wd, sub):
    B, S, D = x.shape
    ts = min(FFN_TOKENS, S)
    nch = wgu.shape[0]
    const3 = lambda b, t: (0, 0, 0)
    return pl.pallas_call(
        functools.partial(_ffn_kernel, sub=sub, nch=nch),
        grid=(B, S // ts),
        in_specs=[pl.BlockSpec((1, ts, D), lambda b, t: (b, t, 0)),
                  pl.BlockSpec((1, N_MOD, D), lambda b, t: (b, 0, 0)),
                  pl.BlockSpec((1, D), lambda b, t: (0, 0)),
                  pl.BlockSpec(wgu.shape, const3, pipeline_mode=pl.Buffered(1)),
                  pl.BlockSpec(wd.shape, const3, pipeline_mode=pl.Buffered(1))],
        out_specs=pl.BlockSpec((1, ts, D), lambda b, t: (b, t, 0)),
        out_shape=jax.ShapeDtypeStruct((B, S, D), F32),
        scratch_shapes=[pltpu.VMEM((ts, D), BF16), pltpu.VMEM((ts, D), F32)],
        compiler_params=_cparams(("arbitrary", "arbitrary")),
        name="ffn",
    )(x, m, g, wgu, wd)


def _rms_rows(v, gain):
    inv = lax.rsqrt(jnp.mean(v * v, axis=0, keepdims=True) + EPS)
    return v * inv * gain


def _rope_rows(t1, t2, cos, sin):
    return t1 * cos - t2 * sin, t1 * sin + t2 * cos


def _ones_row_tile(tk):
    rid = lax.broadcasted_iota(jnp.int32, (V_ROWS - HEAD_V, tk), 0)
    return jnp.where(rid == 0, 1.0, 0.0).astype(BF16)


def _inproj_kernel(x_ref, m_ref, g_ref, wnat_ref, wT_ref, wuqT_ref, wukvT_ref,
                   gdq_ref, gdk_ref, gqa_ref, gkva_ref, gmq_ref, gmk_ref, cos_ref, sin_ref,
                   ab_ref, u_ref, fu_ref, qd_ref, kd_ref, vd_ref, qm_ref, km_ref, vm_ref,
                   *, ts, tk, slope_parts):
    t = pl.program_id(1)
    hb = _modulated_norm(x_ref[0], m_ref, g_ref, 1).astype(BF16)

    nat = jnp.dot(hb, wnat_ref[...], preferred_element_type=F32)
    ab_ref[0] = nat[:, 0:GROUP]
    u_ref[0] = nat[:, GROUP:2 * GROUP] * nat[:, 2 * GROUP:3 * GROUP]
    fu_ref[0] = nat[:, 3 * GROUP:4 * GROUP].astype(BF16)

    pT = lax.dot_general(wT_ref[...], hb, (((1,), (1,)), ((), ())),
                         preferred_element_type=F32)

    ones_tile = _ones_row_tile(tk)
    pos = t * ts + lax.broadcasted_iota(jnp.int32, (1, ts), 1)
    c = pos & (tk - 1)
    c_hi = ((c >> 8) << 8).astype(F32)
    c_lo = (c & 255).astype(F32)
    rid = lax.broadcasted_iota(jnp.int32, (AUG_ROWS, ts), 0)
    pad_d = jnp.zeros((KPAD - 2 * DIFF_QK - AUG_ROWS, ts), F32)
    pad_m = jnp.zeros((KPAD - MLA_QK, ts), F32)
    gdq = gdq_ref[...] * (DIFF_QK ** -0.5 * LOG2E)
    third = _mod3(rid)
    gdk = gdk_ref[...]

    for h in range(HEADS):
        base = 2 * DIFF_QK * h
        ks = []
        for j in range(2):
            r0 = base + DIFF_QK * j
            qd_ref[0, h, DIFF_QK * j:DIFF_QK * (j + 1), :] = _rms_rows(pT[r0:r0 + DIFF_QK], gdq).astype(BF16)
            ks.append(_rms_rows(pT[GROUP + r0:GROUP + r0 + DIFF_QK], gdk))
        s0, s1, s2 = slope_parts[h]
        part = jnp.where(third == 0, s0, jnp.where(third == 1, s1, s2))
        aug = jnp.where(rid < 6, part, jnp.where(rid < 9, c_hi, jnp.where(rid < 12, c_lo, 0.0)))
        kext = jnp.concatenate(ks + [aug, pad_d], axis=0)
        kd_ref[0, h] = kext.T.astype(BF16)
        v = pT[2 * GROUP + HEAD_V * h:2 * GROUP + HEAD_V * (h + 1)].astype(BF16)
        for kb in range(ts // tk):
            vd_ref[0, h, kb, 0:HEAD_V, :] = v[:, kb * tk:(kb + 1) * tk]
            vd_ref[0, h, kb, HEAD_V:V_ROWS, :] = ones_tile

    o_cq = 3 * GROUP
    o_ckv = o_cq + MLA_Q_RANK
    o_kpe = o_ckv + MLA_KV_RANK
    cqn = _rms_rows(pT[o_cq:o_ckv], gqa_ref[...]).astype(BF16)
    qmT = jnp.dot(wuqT_ref[...], cqn, preferred_element_type=F32)
    ckvn = _rms_rows(pT[o_ckv:o_kpe], gkva_ref[...]).astype(BF16)
    kvT = jnp.dot(wukvT_ref[...], ckvn, preferred_element_type=F32)
    kpe = pT[o_kpe:o_kpe + MLA_ROPE]
    cos = cos_ref[...]
    sin = sin_ref[...]
    gmq = gmq_ref[...] * (MLA_QK ** -0.5 * LOG2E)
    gmk = gmk_ref[...]
    half = MLA_ROPE // 2
    for h in range(HEADS):
        qn = _rms_rows(qmT[MLA_QK * h:MLA_QK * (h + 1)], gmq)
        q1, q2 = _rope_rows(qn[MLA_NOPE:MLA_NOPE + half], qn[MLA_NOPE + half:MLA_QK], cos, sin)
        qm_ref[0, h, 0:MLA_NOPE, :] = qn[0:MLA_NOPE].astype(BF16)
        qm_ref[0, h, MLA_NOPE:MLA_NOPE + half, :] = q1.astype(BF16)
        qm_ref[0, h, MLA_NOPE + half:MLA_QK, :] = q2.astype(BF16)
        kv0 = (MLA_NOPE + HEAD_V) * h
        kn = _rms_rows(jnp.concatenate([kvT[kv0:kv0 + MLA_NOPE], kpe], axis=0), gmk)
        k1, k2 = _rope_rows(kn[MLA_NOPE:MLA_NOPE + half], kn[MLA_NOPE + half:MLA_QK], cos, sin)
        kext = jnp.concatenate([kn[0:MLA_NOPE], k1, k2, pad_m], axis=0)
        km_ref[0, h] = kext.T.astype(BF16)
        v = kvT[kv0 + MLA_NOPE:kv0 + MLA_NOPE + HEAD_V].astype(BF16)
        for kb in range(ts // tk):
            vm_ref[0, h, kb, 0:HEAD_V, :] = v[:, kb * tk:(kb + 1) * tk]
            vm_ref[0, h, kb, HEAD_V:V_ROWS, :] = ones_tile


def _inproj(x, m, g, w, rope_cos, rope_sin, slope_parts):
    B, S, D = x.shape
    ts, _, tk = _tiles(S)
    nkb = S // tk
    c2 = lambda b, t: (0, 0)
    tok = lambda b, t: (b, t, 0)
    head_T = lambda b, t: (b, 0, 0, t)
    head_N = lambda b, t: (b, 0, t, 0)
    head_V = lambda b, t: (b, 0, t, 0, 0)
    small = [w['gdq'], w['gdk'], w['gqa'], w['gkva'], w['gmq'], w['gmk']]
    in_specs = ([pl.BlockSpec((1, ts, D), tok),
                 pl.BlockSpec((1, N_MOD, D), lambda b, t: (b, 0, 0)),
                 pl.BlockSpec((1, D), c2),
                 pl.BlockSpec(w['w_nat'].shape, c2, pipeline_mode=pl.Buffered(1)),
                 pl.BlockSpec(w['w_T'].shape, c2, pipeline_mode=pl.Buffered(1)),
                 pl.BlockSpec(w['wuqT'].shape, c2, pipeline_mode=pl.Buffered(1)),
                 pl.BlockSpec(w['wukvT'].shape, c2, pipeline_mode=pl.Buffered(1))]
                + [pl.BlockSpec(a.shape, c2) for a in small]
                + [pl.BlockSpec((MLA_ROPE // 2, ts), lambda b, t: (0, t))] * 2)
    out_shape = [jax.ShapeDtypeStruct((B, S, GROUP), F32),
                 jax.ShapeDtypeStruct((B, S, GROUP), F32),
                 jax.ShapeDtypeStruct((B, S, GROUP), BF16),
                 jax.ShapeDtypeStruct((B, HEADS, 2 * DIFF_QK, S), BF16),
                 jax.ShapeDtypeStruct((B, HEADS, S, KPAD), BF16),
                 jax.ShapeDtypeStruct((B, HEADS, nkb, V_ROWS, tk), BF16),
                 jax.ShapeDtypeStruct((B, HEADS, MLA_QK, S), BF16),
                 jax.ShapeDtypeStruct((B, HEADS, S, KPAD), BF16),
                 jax.ShapeDtypeStruct((B, HEADS, nkb, V_ROWS, tk), BF16)]
    out_specs = [pl.BlockSpec((1, ts, GROUP), tok),
                 pl.BlockSpec((1, ts, GROUP), tok),
                 pl.BlockSpec((1, ts, GROUP), tok),
                 pl.BlockSpec((1, HEADS, 2 * DIFF_QK, ts), head_T),
                 pl.BlockSpec((1, HEADS, ts, KPAD), head_N),
                 pl.BlockSpec((1, HEADS, ts // tk, V_ROWS, tk), head_V),
                 pl.BlockSpec((1, HEADS, MLA_QK, ts), head_T),
                 pl.BlockSpec((1, HEADS, ts, KPAD), head_N),
                 pl.BlockSpec((1, HEADS, ts // tk, V_ROWS, tk), head_V)]
    return pl.pallas_call(
        functools.partial(_inproj_kernel, ts=ts, tk=tk, slope_parts=slope_parts),
        grid=(B, S // ts),
        in_specs=in_specs,
        out_specs=out_specs,
        out_shape=out_shape,
        compiler_params=_cparams(("arbitrary", "arbitrary")),
        name="inproj",
    )(x, m, g, w['w_nat'], w['w_T'], w['wuqT'], w['wukvT'], *small, rope_cos, rope_sin)


def _attn_query_tile(qi, sc_ref, qT_ref, k_ref, vT_ref, gsub_ref, lv_ref, o_ref, rhs_scr, p_scr,
                     *, n_maps, alibi, online, tq, tk, nkb, group_size, lambda_init):
    h = pl.program_id(1)
    q0 = pl.multiple_of(qi * tq, tq)
    q = qT_ref[0, 0, :, pl.ds(q0, tq)]
    shift = sc_ref[h, 4]

    def key_rows(kb, n):
        return k_ref[0, 0, pl.ds(pl.multiple_of(kb * tk, tk), n * tk), :]

    if alibi:
        slope2 = sc_ref[h, 0]
        lane = lax.broadcasted_iota(jnp.int32, (1, tq), 1)
        r_hi = ((lane >> 8) << 8).astype(F32)
        r_lo = (lane & 255).astype(F32)
        rid = lax.broadcasted_iota(jnp.int32, (AUG_ROWS, tq), 0)
        third = _mod3(rid)
        part = jnp.where(third == 0, sc_ref[h, 1], jnp.where(third == 1, sc_ref[h, 2], sc_ref[h, 3]))
        aug = jnp.where(rid < 3, -r_hi, jnp.where(rid < 6, -r_lo, jnp.where(rid < 12, part, 0.0))).astype(BF16)
        zq = jnp.zeros((DIFF_QK, tq), BF16)
        zp = jnp.zeros((KPAD - 2 * DIFF_QK - AUG_ROWS, tq), BF16)
        rhs_scr[0] = jnp.concatenate([q[0:DIFF_QK], zq, aug, zp], axis=0)
        rhs_scr[1] = jnp.concatenate([zq, q[DIFF_QK:2 * DIFF_QK], aug, zp], axis=0)
        col = lax.broadcasted_iota(jnp.int32, (1, KPAD), 1)
        is_aug = (col >= 2 * DIFF_QK) & (col < 2 * DIFF_QK + AUG_ROWS)
        flip = jnp.where(is_aug, -1.0, 1.0).astype(BF16)
        keep = jnp.ones((1, KPAD), BF16)
        d_idx = lax.shift_right_logical(q0, int(math.log2(tk)))
    else:
        rhs_scr[0] = jnp.concatenate([q, jnp.zeros((KPAD - q.shape[0], tq), BF16)], axis=0)

    def accumulate(j, u, kb, s, cb, accs, ms):
        vblk = vT_ref[0, 0, kb]
        if online:
            m_new = jnp.maximum(ms[j], jnp.max(s, axis=0, keepdims=True) + cb)
            p = jnp.exp2(s + (cb - m_new)).astype(BF16)
            return (jnp.exp2(ms[j] - m_new) * accs[j] + jnp.dot(vblk, p, preferred_element_type=F32), m_new)
        p_scr[j, u] = jnp.exp2(s + (cb - shift)).astype(BF16)
        return accs[j] + jnp.dot(vblk, p_scr[j, u], preferred_element_type=F32), ms[j]

    def group(g, carry, first):
        accs, ms = carry
        accs, ms = list(accs), list(ms)
        base = g * group_size
        if alibi:
            kbs = [(d_idx + base + u) & (nkb - 1) for u in range(group_size)]
            parts = [key_rows(kb, 1) * jnp.where(kb > d_idx, flip, keep) for kb in kbs]
            if first:
                parts = [parts[0] * flip] + parts
            kgrp = jnp.concatenate(parts, axis=0)
        else:
            kbs = [base + u for u in range(group_size)]
            kgrp = key_rows(base, group_size)
        off = tk if (alibi and first) else 0
        for j in range(n_maps):
            s = jnp.dot(kgrp, rhs_scr[j], preferred_element_type=F32)
            for u, kb in enumerate(kbs):
                s_u = s[off + u * tk:off + (u + 1) * tk]
                if alibi and first and u == 0:
                    s_u = jnp.minimum(s_u, s[0:tk])
                cb = -slope2 * jnp.abs(q0 - kb * tk).astype(F32) if alibi else 0.0
                accs[j], ms[j] = accumulate(j, u, kb, s_u, cb, accs, ms)
        return tuple(accs), tuple(ms)

    carry = (tuple(jnp.zeros((V_ROWS, tq), F32) for _ in range(n_maps)),
             tuple(jnp.full((1, tq), NEG_BIG, F32) for _ in range(n_maps)))
    first_rest = 0
    if alibi:
        carry = group(0, carry, True)
        first_rest = 1
    carry = lax.fori_loop(first_rest, nkb // group_size, lambda g, c: group(g, c, False), carry)
    accs, _ = carry

    a0 = accs[0]
    o = a0[0:HEAD_V] / a0[HEAD_V:HEAD_V + 1]
    if n_maps == 2:
        a1 = accs[1]
        lv = lv_ref[...]
        lam = (jnp.exp(jnp.sum(lv[0:1] * lv[1:2], axis=1, keepdims=True))
               - jnp.exp(jnp.sum(lv[2:3] * lv[3:4], axis=1, keepdims=True)) + lambda_init)
        o = o - lam * (a1[0:HEAD_V] / a1[HEAD_V:HEAD_V + 1])
        o = _rms_rows(o, gsub_ref[...]) * (1.0 - lambda_init)
    o_ref[0, :, pl.ds(q0, tq)] = o.astype(BF16)


def _attn_kernel(*refs, nq, **kw):
    def tile(qi, carry):
        _attn_query_tile(qi, *refs, **kw)
        return carry
    lax.fori_loop(0, nq, tile, 0)


def _attention_call(sc, qT, k, vT, gsub, lv, *, n_maps, alibi, online, lambda_init):
    B, H, kq, S = qT.shape
    _, tq, tk = _tiles(S)
    nkb = S // tk
    assert tq == tk, "one key block per query tile sits on the diagonal"
    group_size = 1 if online else min(ATTN_GROUP, nkb)
    assert nkb % group_size == 0
    kern = functools.partial(_attn_kernel, n_maps=n_maps, alibi=alibi, online=online, tq=tq, tk=tk,
                             nkb=nkb, group_size=group_size, lambda_init=lambda_init, nq=S // tq)
    n_rhs = n_maps
    name = ("attn_diff" if alibi else "attn_mla") + ("_online" if online else "")
    return pl.pallas_call(
        kern,
        grid=(B, H),
        in_specs=[pl.BlockSpec(memory_space=pltpu.SMEM),
                  pl.BlockSpec((1, 1, kq, S), lambda b, h: (b, h, 0, 0)),
                  pl.BlockSpec((1, 1, S, KPAD), lambda b, h: (b, h, 0, 0)),
                  pl.BlockSpec((1, 1, nkb, V_ROWS, tk), lambda b, h: (b, h, 0, 0, 0)),
                  pl.BlockSpec(gsub.shape, lambda b, h: (0, 0)),
                  pl.BlockSpec(lv.shape, lambda b, h: (0, 0))],
        out_specs=pl.BlockSpec((1, HEAD_V, S), lambda b, h: (b, h, 0)),
        out_shape=jax.ShapeDtypeStruct((B, H * HEAD_V, S), BF16),
        scratch_shapes=[pltpu.VMEM((n_rhs, KPAD, tq), BF16),
                        pltpu.VMEM((n_maps, group_size, tk, tq), BF16)],
        compiler_params=_cparams(("arbitrary", "arbitrary")),
        name=name,
    )(sc, qT, k, vT, gsub, lv)


def _attention(qT, k, vT, gq, gk, gsub, lv, *, width, n_maps, alibi, lambda_init):
    bound = math.sqrt(width) * BOUND_MARGIN * jnp.max(jnp.abs(gq)) * jnp.max(jnp.abs(gk))
    slopes2 = [np.float32(s * LOG2E) for s in _alibi_slopes()]
    static = jnp.asarray([[s, *_bf16_parts(s)] for s in slopes2], F32)
    sc = jnp.concatenate([static, jnp.full((HEADS, 1), LOG2E, F32) * bound,
                          jnp.zeros((HEADS, 3), F32)], axis=1)
    call = functools.partial(_attention_call, n_maps=n_maps, alibi=alibi, lambda_init=lambda_init)
    return lax.cond(bound <= FAST_SOFTMAX_BOUND,
                    functools.partial(call, online=False),
                    functools.partial(call, online=True),
                    sc, qT, k, vT, gsub, lv)


def _f1_kernel(t_ref, x_ref, o_ref):
    o_ref[0] = jnp.dot(t_ref[...], x_ref[0], preferred_element_type=F32).astype(BF16)


def _f2_kernel(g_ref, a_ref, cs_ref, o_ref, *, tk1, n2, ch, norm):
    for j in range(tk1):
        z = jnp.concatenate([a_ref[0, 0, j], a_ref[0, 1, j]], axis=0)
        x = jnp.dot(g_ref[j], z, preferred_element_type=F32)
        xr = x[:n2].astype(BF16)
        xi = x[n2:].astype(BF16)
        y = (jnp.dot(xr, cs_ref[0:ch], preferred_element_type=F32)
             + jnp.dot(xi, cs_ref[ch:2 * ch], preferred_element_type=F32))
        o_ref[0, :, j * ch:(j + 1) * ch] = (y * norm).astype(BF16)


def _fourier_tables(S):
    n1, n2 = _fourier_split(S)
    a = jnp.arange(n1, dtype=jnp.int32)
    ang1 = (2.0 * math.pi / n1) * ((a[:, None] * a[None, :]) % n1).astype(F32)
    t1 = jnp.concatenate([jnp.cos(ang1), -jnp.sin(ang1)], axis=0).astype(BF16)
    k = a[:, None, None] + n1 * jnp.arange(n2, dtype=jnp.int32)[None, :, None]
    b = jnp.arange(n2, dtype=jnp.int32)[None, None, :]
    ang = (2.0 * math.pi / S) * ((k * b) % S).astype(F32)
    cg, sg = jnp.cos(ang), jnp.sin(ang)
    g = jnp.concatenate([jnp.concatenate([cg, sg], axis=2),
                         jnp.concatenate([-sg, cg], axis=2)], axis=1).astype(BF16)
    gw = GROUP // FOURIER_GROUPS
    c = jnp.arange(GROUP, dtype=jnp.int32)
    same = (c[:, None] // gw) == (c[None, :] // gw)
    angc = (2.0 * math.pi / gw) * (((c[:, None] % gw) * (c[None, :] % gw)) % gw).astype(F32)
    cs = jnp.concatenate([jnp.where(same, jnp.cos(angc), 0.0),
                          jnp.where(same, jnp.sin(angc), 0.0)], axis=0).astype(BF16)
    return t1, g, cs


def _fourier(fu, tables):
    B, S, C = fu.shape
    n1, n2 = _fourier_split(S)
    t1, g, cs = tables
    tn = min(4096, n2 * C)
    a = pl.pallas_call(
        _f1_kernel,
        grid=(B, n2 * C // tn),
        in_specs=[pl.BlockSpec(t1.shape, lambda b, j: (0, 0)),
                  pl.BlockSpec((1, n1, tn), lambda b, j: (b, 0, j))],
        out_specs=pl.BlockSpec((1, 2 * n1, tn), lambda b, j: (b, 0, j)),
        out_shape=jax.ShapeDtypeStruct((B, 2 * n1, n2 * C), BF16),
        compiler_params=_cparams(("arbitrary", "arbitrary")),
        name="fourier_stage1",
    )(t1, fu.reshape(B, n1, n2 * C))
    tk1 = 8
    norm = 1.0 / math.sqrt(S * (GROUP // FOURIER_GROUPS))
    y = pl.pallas_call(
        functools.partial(_f2_kernel, tk1=tk1, n2=n2, ch=C, norm=norm),
        grid=(B, n1 // tk1),
        in_specs=[pl.BlockSpec((tk1, 2 * n2, 2 * n2), lambda b, j: (j, 0, 0)),
                  pl.BlockSpec((1, 2, tk1, n2, C), lambda b, j: (b, 0, j, 0, 0)),
                  pl.BlockSpec(cs.shape, lambda b, j: (0, 0))],
        out_specs=pl.BlockSpec((1, n2, tk1 * C), lambda b, j: (b, 0, j)),
        out_shape=jax.ShapeDtypeStruct((B, n2, n1 * C), BF16),
        compiler_params=_cparams(("arbitrary", "arbitrary")),
        name="fourier_stage2",
    )(g, a.reshape(B, 2, n1, n2, C), cs)
    return y.reshape(B, S, C)


def _outproj_ffn_kernel(x_ref, m_ref, ab_ref, u_ref, up_ref, un_ref, cw_ref, yb_ref, yc_ref, yd_ref,
                        w_ref, g_ref, wgu_ref, wd_ref, o_ref, h_scr, acc_scr, *, ts, nt, nch):
    t = pl.program_id(1)
    u = u_ref[0]
    prev_row = jnp.where(t == 0, 0.0, up_ref[0, 7:8, :])
    next_row = jnp.where(t == nt - 1, 0.0, un_ref[0, 0:1, :])
    rid = lax.broadcasted_iota(jnp.int32, u.shape, 0)
    u_m1 = jnp.where(rid == 0, prev_row, pltpu.roll(u, 1, 0))
    u_p1 = jnp.where(rid == ts - 1, next_row, pltpu.roll(u, ts - 1, 0))
    conv = u_m1 * cw_ref[0:1, :] + u * cw_ref[1:2, :] + u_p1 * cw_ref[2:3, :]
    ya = (ab_ref[0] * conv).astype(BF16)
    tn_dims = (((0,), (0,)), ((), ()))
    acc = jnp.dot(ya, w_ref[0:GROUP], preferred_element_type=F32)
    acc += lax.dot_general(yb_ref[0], w_ref[GROUP:2 * GROUP], tn_dims, preferred_element_type=F32)
    acc += lax.dot_general(yc_ref[0], w_ref[2 * GROUP:3 * GROUP], tn_dims, preferred_element_type=F32)
    acc += jnp.dot(yd_ref[0], w_ref[3 * GROUP:4 * GROUP], preferred_element_type=F32)
    x_mid = x_ref[0] + m_ref[0, 5:6, :] * acc
    o_ref[0] = _ffn_tile(x_mid, m_ref, g_ref, wgu_ref, wd_ref, h_scr, acc_scr, 2, nch)


def _outproj_ffn(x, m, ab, u, conv_w, ybT, ycT, yd, w_out, g, wgu, wd):
    B, S, D = x.shape
    ts, _, _ = _tiles(S)
    nt = S // ts
    r8 = ts // 8
    tok = lambda b, t: (b, t, 0)
    const3 = lambda b, t: (0, 0, 0)
    return pl.pallas_call(
        functools.partial(_outproj_ffn_kernel, ts=ts, nt=nt, nch=wgu.shape[0]),
        grid=(B, nt),
        in_specs=[pl.BlockSpec((1, ts, D), tok),
                  pl.BlockSpec((1, N_MOD, D), lambda b, t: (b, 0, 0)),
                  pl.BlockSpec((1, ts, GROUP), tok),
                  pl.BlockSpec((1, ts, GROUP), tok),
                  pl.BlockSpec((1, 8, GROUP), lambda b, t: (b, jnp.maximum(t * r8 - 1, 0), 0)),
                  pl.BlockSpec((1, 8, GROUP), lambda b, t: (b, jnp.minimum((t + 1) * r8, S // 8 - 1), 0)),
                  pl.BlockSpec(conv_w.shape, lambda b, t: (0, 0)),
                  pl.BlockSpec((1, GROUP, ts), lambda b, t: (b, 0, t)),
                  pl.BlockSpec((1, GROUP, ts), lambda b, t: (b, 0, t)),
                  pl.BlockSpec((1, ts, GROUP), tok),
                  pl.BlockSpec(w_out.shape, lambda b, t: (0, 0), pipeline_mode=pl.Buffered(1)),
                  pl.BlockSpec((1, D), lambda b, t: (0, 0)),
                  pl.BlockSpec(wgu.shape, const3, pipeline_mode=pl.Buffered(1)),
                  pl.BlockSpec(wd.shape, const3, pipeline_mode=pl.Buffered(1))],
        out_specs=pl.BlockSpec((1, ts, D), tok),
        out_shape=jax.ShapeDtypeStruct((B, S, D), F32),
        scratch_shapes=[pltpu.VMEM((ts, D), BF16), pltpu.VMEM((ts, D), F32)],
        compiler_params=_cparams(("arbitrary", "arbitrary")),
        name="outproj_ffn",
    )(x, m, ab, u, u, u, conv_w, ybT, ycT, yd, w_out, g, wgu, wd)


def _rope_tables(S):
    inv = ROPE_THETA ** (-jnp.arange(0, MLA_ROPE, 2, dtype=F32) / MLA_ROPE)
    ang = inv[:, None] * jnp.arange(S, dtype=F32)[None, :]
    return jnp.cos(ang), jnp.sin(ang)


def _alibi_slopes():
    return tuple(float(2.0 ** (-8.0 * (i + 1) / HEADS)) for i in range(HEADS))


def _prep_layer(p, l):
    col = lambda v: v.astype(F32).reshape(-1, 1)
    w_in = p['w_in'][l].astype(BF16)
    nch = p['ffn1_w_gu'].shape[-1] // 2 // FF_CHUNK

    def ffn_w(wgu, wd):
        D = wgu.shape[0]
        g = wgu[:, :nch * FF_CHUNK].reshape(D, nch, FF_CHUNK)
        u = wgu[:, nch * FF_CHUNK:].reshape(D, nch, FF_CHUNK)
        wgu_c = jnp.transpose(jnp.concatenate([g, u], axis=2), (1, 0, 2)).astype(BF16)
        return wgu_c, wd.reshape(nch, FF_CHUNK, D).astype(BF16)

    return dict(
        ffn1=ffn_w(p['ffn1_w_gu'][l], p['ffn1_w_down'][l]),
        ffn2=ffn_w(p['ffn2_w_gu'][l], p['ffn2_w_down'][l]),
        w_nat=jnp.concatenate([w_in[:, :NAT_LO], w_in[:, ATT_HI:]], axis=1),
        w_T=w_in[:, NAT_LO:ATT_HI].T,
        wuqT=p['mla_w_uq'][l].astype(BF16).T,
        wukvT=p['mla_w_ukv'][l].astype(BF16).T,
        gdq=col(p['diff_q_g'][l]), gdk=col(p['diff_k_g'][l]),
        gqa=col(p['mla_q_a_g'][l]), gkva=col(p['mla_kv_a_g'][l]),
        gmq=col(p['mla_q_g'][l]), gmk=col(p['mla_k_g'][l]),
        gsub=col(p['diff_subln_g'][l]),
        lv=p['diff_lambda'][l].astype(F32),
        conv_w=p['conv_w'][l].astype(F32),
        w_out=p['w_out'][l].astype(BF16),
        norm_g=p['norm_g'][l].astype(F32),
    )


def _layer(x, m, w, l, consts):
    rope_cos, rope_sin, ftables = consts
    ng = w['norm_g']
    x = _ffn(x, m, ng[0:1], *w['ffn1'], 0)
    slope_parts = tuple(_bf16_parts(np.float32(s * LOG2E)) for s in _alibi_slopes())
    ab, u, fu, qd, kd, vd, qm, km, vm = _inproj(x, m, ng[1:2], w, rope_cos, rope_sin, slope_parts)
    lambda_init = 0.8 - 0.6 * math.exp(-0.3 * l)
    ybT = _attention(qd, kd, vd, w['gdq'], w['gdk'], w['gsub'], w['lv'],
                     width=DIFF_QK, n_maps=2, alibi=True, lambda_init=lambda_init)
    ycT = _attention(qm, km, vm, w['gmq'], w['gmk'], w['gsub'], w['lv'],
                     width=MLA_QK, n_maps=1, alibi=False, lambda_init=0.0)
    yd = _fourier(fu, ftables)
    return _outproj_ffn(x, m, ab, u, w['conv_w'], ybT, ycT, yd, w['w_out'], ng[2:3], *w['ffn2'])


def _trunk(groups, c_all, p):
    depth = p['w_mod'].shape[0]
    mods = _modulation(c_all, p['w_mod'], p['b_mod'])
    outs = []
    consts = []
    for x, _ in groups:
        S = x.shape[1]
        consts.append(_rope_tables(S) + (_fourier_tables(S),))
    xs = [x for x, _ in groups]
    for l in range(depth):
        w = _prep_layer(p, l)
        for gi, (x0, off) in enumerate(groups):
            B, _, D = x0.shape
            m = mods[l, off:off + B].reshape(B, N_MOD, D)
            xs[gi] = _layer(xs[gi], m, w, l, consts[gi])
    return xs


def kernel(x_prompt, x_sample, c_prompt, c_sample, w_mod, b_mod, norm_g, ffn1_w_gu, ffn1_w_down,
           w_in, conv_w, diff_lambda, diff_q_g, diff_k_g, diff_subln_g, mla_q_a_g, mla_w_uq,
           mla_kv_a_g, mla_w_ukv, mla_q_g, mla_k_g, w_out, ffn2_w_gu, ffn2_w_down):
    p = dict(w_mod=w_mod, b_mod=b_mod, norm_g=norm_g, ffn1_w_gu=ffn1_w_gu, ffn1_w_down=ffn1_w_down,
             w_in=w_in, conv_w=conv_w, diff_lambda=diff_lambda, diff_q_g=diff_q_g, diff_k_g=diff_k_g,
             diff_subln_g=diff_subln_g, mla_q_a_g=mla_q_a_g, mla_w_uq=mla_w_uq, mla_kv_a_g=mla_kv_a_g,
             mla_w_ukv=mla_w_ukv, mla_q_g=mla_q_g, mla_k_g=mla_k_g, w_out=w_out,
             ffn2_w_gu=ffn2_w_gu, ffn2_w_down=ffn2_w_down)
    nb = c_prompt.shape[0] + c_sample.shape[0]
    pad = (-nb) % 8
    c_all = jnp.concatenate([c_prompt, c_sample, jnp.zeros((pad, c_prompt.shape[1]), c_prompt.dtype)])
    y_prompt, y_sample = _trunk([(x_prompt, 0), (x_sample, c_prompt.shape[0])], c_all, p)
    return (y_prompt, y_sample)
```

```python
import functools
import math

import numpy as np
import jax
import jax.numpy as jnp
from jax import lax
from jax.experimental import pallas as pl
from jax.experimental.pallas import tpu as pltpu

F32 = jnp.float32
BF16 = jnp.bfloat16

HEADS = 4
GROUP = 256
DIFF_QK = 32
MLA_NOPE = 64
MLA_ROPE = 32
MLA_QK = MLA_NOPE + MLA_ROPE
MLA_Q_RANK = 256
MLA_KV_RANK = 128
HEAD_V = 64
FOURIER_GROUPS = 4
ROPE_THETA = 10000.0
EPS = 1e-6
N_MOD = 9
ATT_ROWS = 3 * GROUP + MLA_Q_RANK + MLA_KV_RANK + MLA_ROPE
NAT_LO = 3 * GROUP
ATT_HI = NAT_LO + ATT_ROWS

BF16_SUBLANES = 16
KPAD = 128
V_ROWS = HEAD_V + BF16_SUBLANES
AUG_ROWS = BF16_SUBLANES
FF_CHUNK = 256
FFN_TOKENS = 512
TOKEN_TILE = 512
MOD_COLS = 1536
ATTN_GROUP = 8
VMEM_LIMIT = 52 * 1024 * 1024
NEG_BIG = -1e30
LOG2E = 1.4426950408889634
FAST_SOFTMAX_BOUND = 25.0
BOUND_MARGIN = 1.02


def _mod3(i):
    three = lambda c: jnp.where(c, 3, 0)
    return i - three(i >= 3) - three(i >= 6) - three(i >= 9)


def _bf16_parts(x):
    parts, r = [], np.float32(x)
    for _ in range(3):
        p = np.float32(np.asarray(r, dtype=jnp.bfloat16))
        parts.append(float(p))
        r = np.float32(r - p)
    return tuple(parts)


def _cparams(sem):
    return pltpu.CompilerParams(dimension_semantics=sem, vmem_limit_bytes=VMEM_LIMIT)


def _tiles(S):
    t = min(TOKEN_TILE, S)
    return t, t, t


def _fourier_split(S):
    n1 = 1 << ((int(math.log2(S)) + 1) // 2)
    return n1, S // n1


def _mod_kernel(c_ref, w_ref, b_ref, o_ref):
    c = c_ref[...]
    s = (c * jax.nn.sigmoid(c)).astype(BF16)
    o_ref[0] = jnp.dot(s, w_ref[0].astype(BF16), preferred_element_type=F32) + b_ref[0]


def _modulation(c_all, w_mod, b_mod):
    L, D, N = w_mod.shape
    bp = c_all.shape[0]
    tn = MOD_COLS
    assert N % tn == 0
    return pl.pallas_call(
        _mod_kernel,
        grid=(L, N // tn),
        in_specs=[pl.BlockSpec((bp, D), lambda l, j: (0, 0)),
                  pl.BlockSpec((1, D, tn), lambda l, j: (l, 0, j)),
                  pl.BlockSpec((1, 1, tn), lambda l, j: (l, 0, j))],
        out_specs=pl.BlockSpec((1, bp, tn), lambda l, j: (l, 0, j)),
        out_shape=jax.ShapeDtypeStruct((L, bp, N), F32),
        compiler_params=_cparams(("arbitrary", "arbitrary")),
        name="modulation",
    )(c_all, w_mod, b_mod.reshape(L, 1, N))


def _modulated_norm(x, m_ref, g_ref, sub):
    shift = m_ref[0, 3 * sub:3 * sub + 1, :]
    scale = m_ref[0, 3 * sub + 1:3 * sub + 2, :]
    ms = jnp.mean(x * x, axis=-1, keepdims=True)
    y = x * lax.rsqrt(ms + EPS) * g_ref[...]
    return y * (1.0 + scale) + shift


def _ffn_tile(x, m_ref, g_ref, wgu_ref, wd_ref, h_scr, acc_scr, sub, nch):
    h_scr[...] = _modulated_norm(x, m_ref, g_ref, sub).astype(BF16)
    acc_scr[...] = jnp.zeros_like(acc_scr)

    def body(c, carry):
        gu = jnp.dot(h_scr[...], wgu_ref[c], preferred_element_type=F32)
        g = gu[:, :FF_CHUNK]
        u = gu[:, FF_CHUNK:]
        a = (g * jax.nn.sigmoid(g) * u).astype(BF16)
        acc_scr[...] += jnp.dot(a, wd_ref[c], preferred_element_type=F32)
        return carry

    lax.fori_loop(0, nch, body, 0, unroll=True)
    gate = m_ref[0, 3 * sub + 2:3 * sub + 3, :]
    return x + 0.5 * gate * acc_scr[...]


def _ffn_kernel(x_ref, m_ref, g_ref, wgu_ref, wd_ref, o_ref, h_scr, acc_scr, *, sub, nch):
    o_ref[0] = _ffn_tile(x_ref[0], m_ref, g_ref, wgu_ref, wd_ref, h_scr, acc_scr, sub, nch)


def _ffn(x, m, g, wgu, wd, sub):
    B, S, D = x.shape
    ts = min(FFN_TOKENS, S)
    nch = wgu.shape[0]
    const3 = lambda b, t: (0, 0, 0)
    return pl.pallas_call(
        functools.partial(_ffn_kernel, sub=sub, nch=nch),
        grid=(B, S // ts),
        in_specs=[pl.BlockSpec((1, ts, D), lambda b, t: (b, t, 0)),
                  pl.BlockSpec((1, N_MOD, D), lambda b, t: (b, 0, 0)),
                  pl.BlockSpec((1, D), lambda b, t: (0, 0)),
                  pl.BlockSpec(wgu.shape, const3, pipeline_mode=pl.Buffered(1)),
                  pl.BlockSpec(wd.shape, const3, pipeline_mode=pl.Buffered(1))],
        out_specs=pl.BlockSpec((1, ts, D), lambda b, t: (b, t, 0)),
        out_shape=jax.ShapeDtypeStruct((B, S, D), F32),
        scratch_shapes=[pltpu.VMEM((ts, D), BF16), pltpu.VMEM((ts, D), F32)],
        compiler_params=_cparams(("arbitrary", "arbitrary")),
        name="ffn",
    )(x, m, g, wgu, wd)


def _rms_rows(v, gain):
    inv = lax.rsqrt(jnp.mean(v * v, axis=0, keepdims=True) + EPS)
    return v * inv * gain


def _rope_rows(t1, t2, cos, sin):
    return t1 * cos - t2 * sin, t1 * sin + t2 * cos


def _ones_row_tile(tk):
    rid = lax.broadcasted_iota(jnp.int32, (V_ROWS - HEAD_V, tk), 0)
    return jnp.where(rid == 0, 1.0, 0.0).astype(BF16)


def _inproj_kernel(x_ref, m_ref, g_ref, wnat_ref, wT_ref, wuqT_ref, wukvT_ref,
                   gdq_ref, gdk_ref, gqa_ref, gkva_ref, gmq_ref, gmk_ref, cos_ref, sin_ref,
                   ab_ref, u_ref, fu_ref, qd_ref, kd_ref, vd_ref, qm_ref, km_ref, vm_ref,
                   *, ts, tk, slope_parts):
    t = pl.program_id(1)
    hb = _modulated_norm(x_ref[0], m_ref, g_ref, 1).astype(BF16)

    nat = jnp.dot(hb, wnat_ref[...], preferred_element_type=F32)
    ab_ref[0] = nat[:, 0:GROUP]
    u_ref[0] = nat[:, GROUP:2 * GROUP] * nat[:, 2 * GROUP:3 * GROUP]
    fu_ref[0] = nat[:, 3 * GROUP:4 * GROUP].astype(BF16)

    pT = lax.dot_general(wT_ref[...], hb, (((1,), (1,)), ((), ())),
                         preferred_element_type=F32)

    ones_tile = _ones_row_tile(tk)
    pos = t * ts + lax.broadcasted_iota(jnp.int32, (1, ts), 1)
    c = pos & (tk - 1)
    c_hi = ((c >> 8) << 8).astype(F32)
    c_lo = (c & 255).astype(F32)
    rid = lax.broadcasted_iota(jnp.int32, (AUG_ROWS, ts), 0)
    pad_d = jnp.zeros((KPAD - 2 * DIFF_QK - AUG_ROWS, ts), F32)
    pad_m = jnp.zeros((KPAD - MLA_QK, ts), F32)
    gdq = gdq_ref[...] * (DIFF_QK ** -0.5 * LOG2E)
    third = _mod3(rid)
    gdk = gdk_ref[...]

    for h in range(HEADS):
        base = 2 * DIFF_QK * h
        ks = []
        for j in range(2):
            r0 = base + DIFF_QK * j
            qd_ref[0, h, DIFF_QK * j:DIFF_QK * (j + 1), :] = _rms_rows(pT[r0:r0 + DIFF_QK], gdq).astype(BF16)
            ks.append(_rms_rows(pT[GROUP + r0:GROUP + r0 + DIFF_QK], gdk))
        s0, s1, s2 = slope_parts[h]
        part = jnp.where(third == 0, s0, jnp.where(third == 1, s1, s2))
        aug = jnp.where(rid < 6, part, jnp.where(rid < 9, c_hi, jnp.where(rid < 12, c_lo, 0.0)))
        kext = jnp.concatenate(ks + [aug, pad_d], axis=0)
        kd_ref[0, h] = kext.T.astype(BF16)
        v = pT[2 * GROUP + HEAD_V * h:2 * GROUP + HEAD_V * (h + 1)].astype(BF16)
        for kb in range(ts // tk):
            vd_ref[0, h, kb, 0:HEAD_V, :] = v[:, kb * tk:(kb + 1) * tk]
            vd_ref[0, h, kb, HEAD_V:V_ROWS, :] = ones_tile

    o_cq = 3 * GROUP
    o_ckv = o_cq + MLA_Q_RANK
    o_kpe = o_ckv + MLA_KV_RANK
    cqn = _rms_rows(pT[o_cq:o_ckv], gqa_ref[...]).astype(BF16)
    qmT = jnp.dot(wuqT_ref[...], cqn, preferred_element_type=F32)
    ckvn = _rms_rows(pT[o_ckv:o_kpe], gkva_ref[...]).astype(BF16)
    kvT = jnp.dot(wukvT_ref[...], ckvn, preferred_element_type=F32)
    kpe = pT[o_kpe:o_kpe + MLA_ROPE]
    cos = cos_ref[...]
    sin = sin_ref[...]
    gmq = gmq_ref[...] * (MLA_QK ** -0.5 * LOG2E)
    gmk = gmk_ref[...]
    half = MLA_ROPE // 2
    for h in range(HEADS):
        qn = _rms_rows(qmT[MLA_QK * h:MLA_QK * (h + 1)], gmq)
        q1, q2 = _rope_rows(qn[MLA_NOPE:MLA_NOPE + half], qn[MLA_NOPE + half:MLA_QK], cos, sin)
        qm_ref[0, h, 0:MLA_NOPE, :] = qn[0:MLA_NOPE].astype(BF16)
        qm_ref[0, h, MLA_NOPE:MLA_NOPE + half, :] = q1.astype(BF16)
        qm_ref[0, h, MLA_NOPE + half:MLA_QK, :] = q2.astype(BF16)
        kv0 = (MLA_NOPE + HEAD_V) * h
        kn = _rms_rows(jnp.concatenate([kvT[kv0:kv0 + MLA_NOPE], kpe], axis=0), gmk)
        k1, k2 = _rope_rows(kn[MLA_NOPE:MLA_NOPE + half], kn[MLA_NOPE + half:MLA_QK], cos, sin)
        kext = jnp.concatenate([kn[0:MLA_NOPE], k1, k2, pad_m], axis=0)
        km_ref[0, h] = kext.T.astype(BF16)
        v = kvT[kv0 + MLA_NOPE:kv0 + MLA_NOPE + HEAD_V].astype(BF16)
        for kb in range(ts // tk):
            vm_ref[0, h, kb, 0:HEAD_V, :] = v[:, kb * tk:(kb + 1) * tk]
            vm_ref[0, h, kb, HEAD_V:V_ROWS, :] = ones_tile


def _inproj(x, m, g, w, rope_cos, rope_sin, slope_parts):
    B, S, D = x.shape
    ts, _, tk = _tiles(S)
    nkb = S // tk
    c2 = lambda b, t: (0, 0)
    tok = lambda b, t: (b, t, 0)
    head_T = lambda b, t: (b, 0, 0, t)
    head_N = lambda b, t: (b, 0, t, 0)
    head_V = lambda b, t: (b, 0, t, 0, 0)
    small = [w['gdq'], w['gdk'], w['gqa'], w['gkva'], w['gmq'], w['gmk']]
    in_specs = ([pl.BlockSpec((1, ts, D), tok),
                 pl.BlockSpec((1, N_MOD, D), lambda b, t: (b, 0, 0)),
                 pl.BlockSpec((1, D), c2),
                 pl.BlockSpec(w['w_nat'].shape, c2, pipeline_mode=pl.Buffered(1)),
                 pl.BlockSpec(w['w_T'].shape, c2, pipeline_mode=pl.Buffered(1)),
                 pl.BlockSpec(w['wuqT'].shape, c2, pipeline_mode=pl.Buffered(1)),
                 pl.BlockSpec(w['wukvT'].shape, c2, pipeline_mode=pl.Buffered(1))]
                + [pl.BlockSpec(a.shape, c2) for a in small]
                + [pl.BlockSpec((MLA_ROPE // 2, ts), lambda b, t: (0, t))] * 2)
    out_shape = [jax.ShapeDtypeStruct((B, S, GROUP), F32),
                 jax.ShapeDtypeStruct((B, S, GROUP), F32),
                 jax.ShapeDtypeStruct((B, S, GROUP), BF16),
                 jax.ShapeDtypeStruct((B, HEADS, 2 * DIFF_QK, S), BF16),
                 jax.ShapeDtypeStruct((B, HEADS, S, KPAD), BF16),
                 jax.ShapeDtypeStruct((B, HEADS, nkb, V_ROWS, tk), BF16),
                 jax.ShapeDtypeStruct((B, HEADS, MLA_QK, S), BF16),
                 jax.ShapeDtypeStruct((B, HEADS, S, KPAD), BF16),
                 jax.ShapeDtypeStruct((B, HEADS, nkb, V_ROWS, tk), BF16)]
    out_specs = [pl.BlockSpec((1, ts, GROUP), tok),
                 pl.BlockSpec((1, ts, GROUP), tok),
                 pl.BlockSpec((1, ts, GROUP), tok),
                 pl.BlockSpec((1, HEADS, 2 * DIFF_QK, ts), head_T),
                 pl.BlockSpec((1, HEADS, ts, KPAD), head_N),
                 pl.BlockSpec((1, HEADS, ts // tk, V_ROWS, tk), head_V),
                 pl.BlockSpec((1, HEADS, MLA_QK, ts), head_T),
                 pl.BlockSpec((1, HEADS, ts, KPAD), head_N),
                 pl.BlockSpec((1, HEADS, ts // tk, V_ROWS, tk), head_V)]
    return pl.pallas_call(
        functools.partial(_inproj_kernel, ts=ts, tk=tk, slope_parts=slope_parts),
        grid=(B, S // ts),
        in_specs=in_specs,
        out_specs=out_specs,
        out_shape=out_shape,
        compiler_params=_cparams(("arbitrary", "arbitrary")),
        name="inproj",
    )(x, m, g, w['w_nat'], w['w_T'], w['wuqT'], w['wukvT'], *small, rope_cos, rope_sin)


def _attn_query_tile(qi, sc_ref, qT_ref, k_ref, vT_ref, gsub_ref, lv_ref, o_ref, rhs_scr, p_scr,
                     *, n_maps, alibi, online, tq, tk, nkb, group_size, lambda_init):
    h = pl.program_id(1)
    q0 = pl.multiple_of(qi * tq, tq)
    q = qT_ref[0, 0, :, pl.ds(q0, tq)]
    shift = sc_ref[h, 4]

    def key_rows(kb, n):
        return k_ref[0, 0, pl.ds(pl.multiple_of(kb * tk, tk), n * tk), :]

    if alibi:
        slope2 = sc_ref[h, 0]
        lane = lax.broadcasted_iota(jnp.int32, (1, tq), 1)
        r_hi = ((lane >> 8) << 8).astype(F32)
        r_lo = (lane & 255).astype(F32)
        rid = lax.broadcasted_iota(jnp.int32, (AUG_ROWS, tq), 0)
        third = _mod3(rid)
        part = jnp.where(third == 0, sc_ref[h, 1], jnp.where(third == 1, sc_ref[h, 2], sc_ref[h, 3]))
        aug = jnp.where(rid < 3, -r_hi, jnp.where(rid < 6, -r_lo, jnp.where(rid < 12, part, 0.0))).astype(BF16)
        zq = jnp.zeros((DIFF_QK, tq), BF16)
        zp = jnp.zeros((KPAD - 2 * DIFF_QK - AUG_ROWS, tq), BF16)
        rhs_scr[0] = jnp.concatenate([q[0:DIFF_QK], zq, aug, zp], axis=0)
        rhs_scr[1] = jnp.concatenate([zq, q[DIFF_QK:2 * DIFF_QK], aug, zp], axis=0)
        col = lax.broadcasted_iota(jnp.int32, (1, KPAD), 1)
        is_aug = (col >= 2 * DIFF_QK) & (col < 2 * DIFF_QK + AUG_ROWS)
        flip = jnp.where(is_aug, -1.0, 1.0).astype(BF16)
        keep = jnp.ones((1, KPAD), BF16)
        d_idx = lax.shift_right_logical(q0, int(math.log2(tk)))
    else:
        rhs_scr[0] = jnp.concatenate([q, jnp.zeros((KPAD - q.shape[0], tq), BF16)], axis=0)

    def accumulate(j, u, kb, s, cb, accs, ms):
        vblk = vT_ref[0, 0, kb]
        if online:
            m_new = jnp.maximum(ms[j], jnp.max(s, axis=0, keepdims=True) + cb)
            p = jnp.exp2(s + (cb - m_new)).astype(BF16)
            return (jnp.exp2(ms[j] - m_new) * accs[j] + jnp.dot(vblk, p, preferred_element_type=F32), m_new)
        p_scr[j, u] = jnp.exp2(s + (cb - shift)).astype(BF16)
        return accs[j] + jnp.dot(vblk, p_scr[j, u], preferred_element_type=F32), ms[j]

    def group(g, carry, first):
        accs, ms = carry
        accs, ms = list(accs), list(ms)
        base = g * group_size
        if alibi:
            kbs = [(d_idx + base + u) & (nkb - 1) for u in range(group_size)]
            parts = [key_rows(kb, 1) * jnp.where(kb > d_idx, flip, keep) for kb in kbs]
            if first:
                parts = [parts[0] * flip] + parts
            kgrp = jnp.concatenate(parts, axis=0)
        else:
            kbs = [base + u for u in range(group_size)]
            kgrp = key_rows(base, group_size)
        off = tk if (alibi and first) else 0
        for j in range(n_maps):
            s = jnp.dot(kgrp, rhs_scr[j], preferred_element_type=F32)
            for u, kb in enumerate(kbs):
                s_u = s[off + u * tk:off + (u + 1) * tk]
                if alibi and first and u == 0:
                    s_u = jnp.minimum(s_u, s[0:tk])
                cb = -slope2 * jnp.abs(q0 - kb * tk).astype(F32) if alibi else 0.0
                accs[j], ms[j] = accumulate(j, u, kb, s_u, cb, accs, ms)
        return tuple(accs), tuple(ms)

    carry = (tuple(jnp.zeros((V_ROWS, tq), F32) for _ in range(n_maps)),
             tuple(jnp.full((1, tq), NEG_BIG, F32) for _ in range(n_maps)))
    first_rest = 0
    if alibi:
        carry = group(0, carry, True)
        first_rest = 1
    carry = lax.fori_loop(first_rest, nkb // group_size, lambda g, c: group(g, c, False), carry)
    accs, _ = carry

    a0 = accs[0]
    o = a0[0:HEAD_V] / a0[HEAD_V:HEAD_V + 1]
    if n_maps == 2:
        a1 = accs[1]
        lv = lv_ref[...]
        lam = (jnp.exp(jnp.sum(lv[0:1] * lv[1:2], axis=1, keepdims=True))
               - jnp.exp(jnp.sum(lv[2:3] * lv[3:4], axis=1, keepdims=True)) + lambda_init)
        o = o - lam * (a1[0:HEAD_V] / a1[HEAD_V:HEAD_V + 1])
        o = _rms_rows(o, gsub_ref[...]) * (1.0 - lambda_init)
    o_ref[0, :, pl.ds(q0, tq)] = o.astype(BF16)


def _attn_kernel(*refs, nq, **kw):
    def tile(qi, carry):
        _attn_query_tile(qi, *refs, **kw)
        return carry
    lax.fori_loop(0, nq, tile, 0)


def _attention_call(sc, qT, k, vT, gsub, lv, *, n_maps, alibi, online, lambda_init):
    B, H, kq, S = qT.shape
    _, tq, tk = _tiles(S)
    nkb = S // tk
    assert tq == tk, "one key block per query tile sits on the diagonal"
    group_size = 1 if online else min(ATTN_GROUP, nkb)
    assert nkb % group_size == 0 and nkb & (nkb - 1) == 0, "rotated block order wraps with a mask"
    kern = functools.partial(_attn_kernel, n_maps=n_maps, alibi=alibi, online=online, tq=tq, tk=tk,
                             nkb=nkb, group_size=group_size, lambda_init=lambda_init, nq=S // tq)
    n_rhs = n_maps
    name = ("attn_diff" if alibi else "attn_mla") + ("_online" if online else "")
    return pl.pallas_call(
        kern,
        grid=(B, H),
        in_specs=[pl.BlockSpec(memory_space=pltpu.SMEM),
                  pl.BlockSpec((1, 1, kq, S), lambda b, h: (b, h, 0, 0)),
                  pl.BlockSpec((1, 1, S, KPAD), lambda b, h: (b, h, 0, 0)),
                  pl.BlockSpec((1, 1, nkb, V_ROWS, tk), lambda b, h: (b, h, 0, 0, 0)),
                  pl.BlockSpec(gsub.shape, lambda b, h: (0, 0)),
                  pl.BlockSpec(lv.shape, lambda b, h: (0, 0))],
        out_specs=pl.BlockSpec((1, HEAD_V, S), lambda b, h: (b, h, 0)),
        out_shape=jax.ShapeDtypeStruct((B, H * HEAD_V, S), BF16),
        scratch_shapes=[pltpu.VMEM((n_rhs, KPAD, tq), BF16),
                        pltpu.VMEM((n_maps, group_size, tk, tq), BF16)],
        compiler_params=_cparams(("arbitrary", "arbitrary")),
        name=name,
    )(sc, qT, k, vT, gsub, lv)


def _attention(qT, k, vT, gq, gk, gsub, lv, *, width, n_maps, alibi, lambda_init):
    bound = math.sqrt(width) * BOUND_MARGIN * jnp.max(jnp.abs(gq)) * jnp.max(jnp.abs(gk))
    slopes2 = [np.float32(s * LOG2E) for s in _alibi_slopes()]
    static = jnp.asarray([[s, *_bf16_parts(s)] for s in slopes2], F32)
    sc = jnp.concatenate([static, jnp.full((HEADS, 1), LOG2E, F32) * bound,
                          jnp.zeros((HEADS, 3), F32)], axis=1)
    call = functools.partial(_attention_call, n_maps=n_maps, alibi=alibi, lambda_init=lambda_init)
    return lax.cond(bound <= FAST_SOFTMAX_BOUND,
                    functools.partial(call, online=False),
                    functools.partial(call, online=True),
                    sc, qT, k, vT, gsub, lv)


def _f1_kernel(t_ref, x_ref, o_ref):
    o_ref[0] = jnp.dot(t_ref[...], x_ref[0], preferred_element_type=F32).astype(BF16)


def _f2_kernel(g_ref, a_ref, cs_ref, o_ref, *, tk1, n2, ch, norm):
    xr, xi = [], []
    for j in range(tk1):
        z = jnp.concatenate([a_ref[0, 0, j], a_ref[0, 1, j]], axis=0)
        x = jnp.dot(g_ref[j], z, preferred_element_type=F32)
        xr.append(x[:n2].astype(BF16))
        xi.append(x[n2:].astype(BF16))
    y = (jnp.dot(jnp.concatenate(xr, axis=0), cs_ref[0:ch], preferred_element_type=F32)
         + jnp.dot(jnp.concatenate(xi, axis=0), cs_ref[ch:2 * ch], preferred_element_type=F32))
    for j in range(tk1):
        o_ref[0, :, j * ch:(j + 1) * ch] = (y[j * n2:(j + 1) * n2] * norm).astype(BF16)


def _fourier_tables(S):
    n1, n2 = _fourier_split(S)
    a = jnp.arange(n1, dtype=jnp.int32)
    ang1 = (2.0 * math.pi / n1) * ((a[:, None] * a[None, :]) % n1).astype(F32)
    t1 = jnp.concatenate([jnp.cos(ang1), -jnp.sin(ang1)], axis=0).astype(BF16)
    k = a[:, None, None] + n1 * jnp.arange(n2, dtype=jnp.int32)[None, :, None]
    b = jnp.arange(n2, dtype=jnp.int32)[None, None, :]
    ang = (2.0 * math.pi / S) * ((k * b) % S).astype(F32)
    cg, sg = jnp.cos(ang), jnp.sin(ang)
    g = jnp.concatenate([jnp.concatenate([cg, sg], axis=2),
                         jnp.concatenate([-sg, cg], axis=2)], axis=1).astype(BF16)
    gw = GROUP // FOURIER_GROUPS
    c = jnp.arange(GROUP, dtype=jnp.int32)
    same = (c[:, None] // gw) == (c[None, :] // gw)
    angc = (2.0 * math.pi / gw) * (((c[:, None] % gw) * (c[None, :] % gw)) % gw).astype(F32)
    cs = jnp.concatenate([jnp.where(same, jnp.cos(angc), 0.0),
                          jnp.where(same, jnp.sin(angc), 0.0)], axis=0).astype(BF16)
    return t1, g, cs


def _fourier(fu, tables):
    B, S, C = fu.shape
    n1, n2 = _fourier_split(S)
    t1, g, cs = tables
    tn = min(8192, n2 * C)
    a = pl.pallas_call(
        _f1_kernel,
        grid=(B, n2 * C // tn),
        in_specs=[pl.BlockSpec(t1.shape, lambda b, j: (0, 0)),
                  pl.BlockSpec((1, n1, tn), lambda b, j: (b, 0, j))],
        out_specs=pl.BlockSpec((1, 2 * n1, tn), lambda b, j: (b, 0, j)),
        out_shape=jax.ShapeDtypeStruct((B, 2 * n1, n2 * C), BF16),
        compiler_params=_cparams(("arbitrary", "arbitrary")),
        name="fourier_stage1",
    )(t1, fu.reshape(B, n1, n2 * C))
    tk1 = min(16, n1)
    norm = 1.0 / math.sqrt(S * (GROUP // FOURIER_GROUPS))
    y = pl.pallas_call(
        functools.partial(_f2_kernel, tk1=tk1, n2=n2, ch=C, norm=norm),
        grid=(B, n1 // tk1),
        in_specs=[pl.BlockSpec((tk1, 2 * n2, 2 * n2), lambda b, j: (j, 0, 0)),
                  pl.BlockSpec((1, 2, tk1, n2, C), lambda b, j: (b, 0, j, 0, 0)),
                  pl.BlockSpec(cs.shape, lambda b, j: (0, 0))],
        out_specs=pl.BlockSpec((1, n2, tk1 * C), lambda b, j: (b, 0, j)),
        out_shape=jax.ShapeDtypeStruct((B, n2, n1 * C), BF16),
        compiler_params=_cparams(("arbitrary", "arbitrary")),
        name="fourier_stage2",
    )(g, a.reshape(B, 2, n1, n2, C), cs)
    return y.reshape(B, S, C)


def _outproj_ffn_kernel(x_ref, m_ref, ab_ref, u_ref, up_ref, un_ref, cw_ref, yb_ref, yc_ref, yd_ref,
                        w_ref, g_ref, wgu_ref, wd_ref, o_ref, h_scr, acc_scr, *, ts, nt, nch):
    t = pl.program_id(1)
    u = u_ref[0]
    prev_row = jnp.where(t == 0, 0.0, up_ref[0, 7:8, :])
    next_row = jnp.where(t == nt - 1, 0.0, un_ref[0, 0:1, :])
    rid = lax.broadcasted_iota(jnp.int32, u.shape, 0)
    u_m1 = jnp.where(rid == 0, prev_row, pltpu.roll(u, 1, 0))
    u_p1 = jnp.where(rid == ts - 1, next_row, pltpu.roll(u, ts - 1, 0))
    conv = u_m1 * cw_ref[0:1, :] + u * cw_ref[1:2, :] + u_p1 * cw_ref[2:3, :]
    ya = (ab_ref[0] * conv).astype(BF16)
    tn_dims = (((0,), (0,)), ((), ()))
    acc = jnp.dot(ya, w_ref[0:GROUP], preferred_element_type=F32)
    acc += lax.dot_general(yb_ref[0], w_ref[GROUP:2 * GROUP], tn_dims, preferred_element_type=F32)
    acc += lax.dot_general(yc_ref[0], w_ref[2 * GROUP:3 * GROUP], tn_dims, preferred_element_type=F32)
    acc += jnp.dot(yd_ref[0], w_ref[3 * GROUP:4 * GROUP], preferred_element_type=F32)
    x_mid = x_ref[0] + m_ref[0, 5:6, :] * acc
    o_ref[0] = _ffn_tile(x_mid, m_ref, g_ref, wgu_ref, wd_ref, h_scr, acc_scr, 2, nch)


def _outproj_ffn(x, m, ab, u, conv_w, ybT, ycT, yd, w_out, g, wgu, wd):
    B, S, D = x.shape
    ts, _, _ = _tiles(S)
    nt = S // ts
    r8 = ts // 8
    tok = lambda b, t: (b, t, 0)
    const3 = lambda b, t: (0, 0, 0)
    return pl.pallas_call(
        functools.partial(_outproj_ffn_kernel, ts=ts, nt=nt, nch=wgu.shape[0]),
        grid=(B, nt),
        in_specs=[pl.BlockSpec((1, ts, D), tok),
                  pl.BlockSpec((1, N_MOD, D), lambda b, t: (b, 0, 0)),
                  pl.BlockSpec((1, ts, GROUP), tok),
                  pl.BlockSpec((1, ts, GROUP), tok),
                  pl.BlockSpec((1, 8, GROUP), lambda b, t: (b, jnp.maximum(t * r8 - 1, 0), 0)),
                  pl.BlockSpec((1, 8, GROUP), lambda b, t: (b, jnp.minimum((t + 1) * r8, S // 8 - 1), 0)),
                  pl.BlockSpec(conv_w.shape, lambda b, t: (0, 0)),
                  pl.BlockSpec((1, GROUP, ts), lambda b, t: (b, 0, t)),
                  pl.BlockSpec((1, GROUP, ts), lambda b, t: (b, 0, t)),
                  pl.BlockSpec((1, ts, GROUP), tok),
                  pl.BlockSpec(w_out.shape, lambda b, t: (0, 0), pipeline_mode=pl.Buffered(1)),
                  pl.BlockSpec((1, D), lambda b, t: (0, 0)),
                  pl.BlockSpec(wgu.shape, const3, pipeline_mode=pl.Buffered(1)),
                  pl.BlockSpec(wd.shape, const3, pipeline_mode=pl.Buffered(1))],
        out_specs=pl.BlockSpec((1, ts, D), tok),
        out_shape=jax.ShapeDtypeStruct((B, S, D), F32),
        scratch_shapes=[pltpu.VMEM((ts, D), BF16), pltpu.VMEM((ts, D), F32)],
        compiler_params=_cparams(("arbitrary", "arbitrary")),
        name="outproj_ffn",
    )(x, m, ab, u, u, u, conv_w, ybT, ycT, yd, w_out, g, wgu, wd)


def _rope_tables(S):
    inv = ROPE_THETA ** (-jnp.arange(0, MLA_ROPE, 2, dtype=F32) / MLA_ROPE)
    ang = inv[:, None] * jnp.arange(S, dtype=F32)[None, :]
    return jnp.cos(ang), jnp.sin(ang)


def _alibi_slopes():
    return tuple(float(2.0 ** (-8.0 * (i + 1) / HEADS)) for i in range(HEADS))


def _prep_layer(p, l):
    col = lambda v: v.astype(F32).reshape(-1, 1)
    w_in = p['w_in'][l].astype(BF16)
    nch = p['ffn1_w_gu'].shape[-1] // 2 // FF_CHUNK

    def ffn_w(wgu, wd):
        D = wgu.shape[0]
        g = wgu[:, :nch * FF_CHUNK].reshape(D, nch, FF_CHUNK)
        u = wgu[:, nch * FF_CHUNK:].reshape(D, nch, FF_CHUNK)
        wgu_c = jnp.transpose(jnp.concatenate([g, u], axis=2), (1, 0, 2)).astype(BF16)
        return wgu_c, wd.reshape(nch, FF_CHUNK, D).astype(BF16)

    return dict(
        ffn1=ffn_w(p['ffn1_w_gu'][l], p['ffn1_w_down'][l]),
        ffn2=ffn_w(p['ffn2_w_gu'][l], p['ffn2_w_down'][l]),
        w_nat=jnp.concatenate([w_in[:, :NAT_LO], w_in[:, ATT_HI:]], axis=1),
        w_T=w_in[:, NAT_LO:ATT_HI].T,
        wuqT=p['mla_w_uq'][l].astype(BF16).T,
        wukvT=p['mla_w_ukv'][l].astype(BF16).T,
        gdq=col(p['diff_q_g'][l]), gdk=col(p['diff_k_g'][l]),
        gqa=col(p['mla_q_a_g'][l]), gkva=col(p['mla_kv_a_g'][l]),
        gmq=col(p['mla_q_g'][l]), gmk=col(p['mla_k_g'][l]),
        gsub=col(p['diff_subln_g'][l]),
        lv=p['diff_lambda'][l].astype(F32),
        conv_w=p['conv_w'][l].astype(F32),
        w_out=p['w_out'][l].astype(BF16),
        norm_g=p['norm_g'][l].astype(F32),
    )


def _layer(x, m, w, l, consts):
    rope_cos, rope_sin, ftables = consts
    ng = w['norm_g']
    x = _ffn(x, m, ng[0:1], *w['ffn1'], 0)
    slope_parts = tuple(_bf16_parts(np.float32(s * LOG2E)) for s in _alibi_slopes())
    ab, u, fu, qd, kd, vd, qm, km, vm = _inproj(x, m, ng[1:2], w, rope_cos, rope_sin, slope_parts)
    lambda_init = 0.8 - 0.6 * math.exp(-0.3 * l)
    ybT = _attention(qd, kd, vd, w['gdq'], w['gdk'], w['gsub'], w['lv'],
                     width=DIFF_QK, n_maps=2, alibi=True, lambda_init=lambda_init)
    ycT = _attention(qm, km, vm, w['gmq'], w['gmk'], w['gsub'], w['lv'],
                     width=MLA_QK, n_maps=1, alibi=False, lambda_init=0.0)
    yd = _fourier(fu, ftables)
    return _outproj_ffn(x, m, ab, u, w['conv_w'], ybT, ycT, yd, w['w_out'], ng[2:3], *w['ffn2'])


def _trunk(groups, c_all, p):
    depth = p['w_mod'].shape[0]
    mods = _modulation(c_all, p['w_mod'], p['b_mod'])
    outs = []
    consts = []
    for x, _ in groups:
        S = x.shape[1]
        consts.append(_rope_tables(S) + (_fourier_tables(S),))
    xs = [x for x, _ in groups]
    for l in range(depth):
        w = _prep_layer(p, l)
        for gi, (x0, off) in enumerate(groups):
            B, _, D = x0.shape
            m = mods[l, off:off + B].reshape(B, N_MOD, D)
            xs[gi] = _layer(xs[gi], m, w, l, consts[gi])
    return xs


def kernel(x_prompt, x_sample, c_prompt, c_sample, w_mod, b_mod, norm_g, ffn1_w_gu, ffn1_w_down,
           w_in, conv_w, diff_lambda, diff_q_g, diff_k_g, diff_subln_g, mla_q_a_g, mla_w_uq,
           mla_kv_a_g, mla_w_ukv, mla_q_g, mla_k_g, w_out, ffn2_w_gu, ffn2_w_down):
    p = dict(w_mod=w_mod, b_mod=b_mod, norm_g=norm_g, ffn1_w_gu=ffn1_w_gu, ffn1_w_down=ffn1_w_down,
             w_in=w_in, conv_w=conv_w, diff_lambda=diff_lambda, diff_q_g=diff_q_g, diff_k_g=diff_k_g,
             diff_subln_g=diff_subln_g, mla_q_a_g=mla_q_a_g, mla_w_uq=mla_w_uq, mla_kv_a_g=mla_kv_a_g,
             mla_w_ukv=mla_w_ukv, mla_q_g=mla_q_g, mla_k_g=mla_k_g, w_out=w_out,
             ffn2_w_gu=ffn2_w_gu, ffn2_w_down=ffn2_w_down)
    nb = c_prompt.shape[0] + c_sample.shape[0]
    pad = (-nb) % 8
    c_all = jnp.concatenate([c_prompt, c_sample, jnp.zeros((pad, c_prompt.shape[1]), c_prompt.dtype)])
    y_prompt, y_sample = _trunk([(x_prompt, 0), (x_sample, c_prompt.shape[0])], c_all, p)
    return (y_prompt, y_sample)
```

```python
import functools
import math

import numpy as np
import jax
import jax.numpy as jnp
from jax import lax
from jax.experimental import pallas as pl
from jax.experimental.pallas import tpu as pltpu

F32 = jnp.float32
BF16 = jnp.bfloat16

HEADS = 4
GROUP = 256
DIFF_QK = 32
MLA_NOPE = 64
MLA_ROPE = 32
MLA_QK = MLA_NOPE + MLA_ROPE
MLA_Q_RANK = 256
MLA_KV_RANK = 128
HEAD_V = 64
FOURIER_GROUPS = 4
ROPE_THETA = 10000.0
EPS = 1e-6
N_MOD = 9
ATT_ROWS = 3 * GROUP + MLA_Q_RANK + MLA_KV_RANK + MLA_ROPE
NAT_LO = 3 * GROUP
ATT_HI = NAT_LO + ATT_ROWS

BF16_SUBLANES = 16
KPAD = 128
V_ROWS = HEAD_V + BF16_SUBLANES
AUG_ROWS = BF16_SUBLANES
FF_CHUNK = 256
FFN_TOKENS = 512
TOKEN_TILE = 512
MOD_COLS = 1536
INPROJ_BLOCKS = 2
ATTN_GROUP = 8
VMEM_LIMIT = 52 * 1024 * 1024
NEG_BIG = -1e30
LOG2E = 1.4426950408889634
FAST_SOFTMAX_BOUND = 25.0
BOUND_MARGIN = 1.02


def _mod3(i):
    three = lambda c: jnp.where(c, 3, 0)
    return i - three(i >= 3) - three(i >= 6) - three(i >= 9)


def _bf16_parts(x):
    parts, r = [], np.float32(x)
    for _ in range(3):
        p = np.float32(np.asarray(r, dtype=jnp.bfloat16))
        parts.append(float(p))
        r = np.float32(r - p)
    return tuple(parts)


def _cparams(sem):
    return pltpu.CompilerParams(dimension_semantics=sem, vmem_limit_bytes=VMEM_LIMIT)


def _tiles(S):
    t = min(TOKEN_TILE, S)
    return t, t, t


def _fourier_split(S):
    n1 = 1 << ((int(math.log2(S)) + 1) // 2)
    return n1, S // n1


def _mod_kernel(c_ref, w_ref, b_ref, o_ref):
    c = c_ref[...]
    s = (c * jax.nn.sigmoid(c)).astype(BF16)
    o_ref[0] = jnp.dot(s, w_ref[0].astype(BF16), preferred_element_type=F32) + b_ref[0]


def _modulation(c_all, w_mod, b_mod):
    L, D, N = w_mod.shape
    bp = c_all.shape[0]
    tn = MOD_COLS
    assert N % tn == 0
    return pl.pallas_call(
        _mod_kernel,
        grid=(L, N // tn),
        in_specs=[pl.BlockSpec((bp, D), lambda l, j: (0, 0)),
                  pl.BlockSpec((1, D, tn), lambda l, j: (l, 0, j)),
                  pl.BlockSpec((1, 1, tn), lambda l, j: (l, 0, j))],
        out_specs=pl.BlockSpec((1, bp, tn), lambda l, j: (l, 0, j)),
        out_shape=jax.ShapeDtypeStruct((L, bp, N), F32),
        compiler_params=_cparams(("arbitrary", "arbitrary")),
        name="modulation",
    )(c_all, w_mod, b_mod.reshape(L, 1, N))


def _modulated_norm(x, m_ref, g_ref, sub):
    shift = m_ref[0, 3 * sub:3 * sub + 1, :]
    scale = m_ref[0, 3 * sub + 1:3 * sub + 2, :]
    ms = jnp.mean(x * x, axis=-1, keepdims=True)
    y = x * lax.rsqrt(ms + EPS) * g_ref[...]
    return y * (1.0 + scale) + shift


def _ffn_tile(x, m_ref, g_ref, wgu_ref, wd_ref, h_scr, acc_scr, sub, nch):
    h_scr[...] = _modulated_norm(x, m_ref, g_ref, sub).astype(BF16)
    acc_scr[...] = jnp.zeros_like(acc_scr)

    def body(c, carry):
        gu = jnp.dot(h_scr[...], wgu_ref[c], preferred_element_type=F32)
        g = gu[:, :FF_CHUNK]
        u = gu[:, FF_CHUNK:]
        a = (g * jax.nn.sigmoid(g) * u).astype(BF16)
        acc_scr[...] += jnp.dot(a, wd_ref[c], preferred_element_type=F32)
        return carry

    lax.fori_loop(0, nch, body, 0, unroll=True)
    gate = m_ref[0, 3 * sub + 2:3 * sub + 3, :]
    return x + 0.5 * gate * acc_scr[...]


def _ffn_kernel(x_ref, m_ref, g_ref, wgu_ref, wd_ref, o_ref, h_scr, acc_scr, *, sub, nch):
    o_ref[0] = _ffn_tile(x_ref[0], m_ref, g_ref, wgu_ref, wd_ref, h_scr, acc_scr, sub, nch)


def _ffn(x, m, g, wgu, wd, sub):
    B, S, D = x.shape
    ts = min(FFN_TOKENS, S)
    nch = wgu.shape[0]
    const3 = lambda b, t: (0, 0, 0)
    return pl.pallas_call(
        functools.partial(_ffn_kernel, sub=sub, nch=nch),
        grid=(B, S // ts),
        in_specs=[pl.BlockSpec((1, ts, D), lambda b, t: (b, t, 0)),
                  pl.BlockSpec((1, N_MOD, D), lambda b, t: (b, 0, 0)),
                  pl.BlockSpec((1, D), lambda b, t: (0, 0)),
                  pl.BlockSpec(wgu.shape, const3, pipeline_mode=pl.Buffered(1)),
                  pl.BlockSpec(wd.shape, const3, pipeline_mode=pl.Buffered(1))],
        out_specs=pl.BlockSpec((1, ts, D), lambda b, t: (b, t, 0)),
        out_shape=jax.ShapeDtypeStruct((B, S, D), F32),
        scratch_shapes=[pltpu.VMEM((ts, D), BF16), pltpu.VMEM((ts, D), F32)],
        compiler_params=_cparams(("arbitrary", "arbitrary")),
        name="ffn",
    )(x, m, g, wgu, wd)


def _rms_rows(v, gain):
    inv = lax.rsqrt(jnp.mean(v * v, axis=0, keepdims=True) + EPS)
    return v * inv * gain


def _rope_rows(t1, t2, cos, sin):
    return t1 * cos - t2 * sin, t1 * sin + t2 * cos


def _ones_row_tile(tk):
    rid = lax.broadcasted_iota(jnp.int32, (V_ROWS - HEAD_V, tk), 0)
    return jnp.where(rid == 0, 1.0, 0.0).astype(BF16)


def _inproj_kernel(x_ref, m_ref, g_ref, wnat_ref, wT_ref, wuqT_ref, wukvT_ref,
                   gdq_ref, gdk_ref, gqa_ref, gkva_ref, gmq_ref, gmk_ref, cos_ref, sin_ref,
                   ab_ref, u_ref, fu_ref, qd_ref, kd_ref, vd_ref, qm_ref, km_ref, vm_ref,
                   *, ts, tk, slope_parts):
    sub = range(ts // tk)
    projected = [_inproj_project(si, x_ref, m_ref, g_ref, wnat_ref, wT_ref, ab_ref, u_ref, fu_ref, tk)
                 for si in sub]
    for si in sub:
        _inproj_operands(si, projected[si], wuqT_ref, wukvT_ref,
                         gdq_ref, gdk_ref, gqa_ref, gkva_ref, gmq_ref, gmk_ref, cos_ref, sin_ref,
                         qd_ref, kd_ref, vd_ref, qm_ref, km_ref, vm_ref,
                         ts=ts, tk=tk, slope_parts=slope_parts)


def _inproj_project(si, x_ref, m_ref, g_ref, wnat_ref, wT_ref, ab_ref, u_ref, fu_ref, tk):
    rows = slice(si * tk, (si + 1) * tk)
    hb = _modulated_norm(x_ref[0, rows], m_ref, g_ref, 1).astype(BF16)

    nat = jnp.dot(hb, wnat_ref[...], preferred_element_type=F32)
    ab_ref[0, rows] = nat[:, 0:GROUP]
    u_ref[0, rows] = nat[:, GROUP:2 * GROUP] * nat[:, 2 * GROUP:3 * GROUP]
    fu_ref[0, rows] = nat[:, 3 * GROUP:4 * GROUP].astype(BF16)

    return lax.dot_general(wT_ref[...], hb, (((1,), (1,)), ((), ())), preferred_element_type=F32)


def _inproj_operands(si, pT, wuqT_ref, wukvT_ref,
                     gdq_ref, gdk_ref, gqa_ref, gkva_ref, gmq_ref, gmk_ref, cos_ref, sin_ref,
                     qd_ref, kd_ref, vd_ref, qm_ref, km_ref, vm_ref, *, ts, tk, slope_parts):
    t = pl.program_id(1)
    rows = slice(si * tk, (si + 1) * tk)
    ones_tile = _ones_row_tile(tk)
    pos = t * ts + si * tk + lax.broadcasted_iota(jnp.int32, (1, tk), 1)
    c = pos & (tk - 1)
    c_hi = ((c >> 8) << 8).astype(F32)
    c_lo = (c & 255).astype(F32)
    rid = lax.broadcasted_iota(jnp.int32, (AUG_ROWS, tk), 0)
    pad_d = jnp.zeros((KPAD - 2 * DIFF_QK - AUG_ROWS, tk), F32)
    pad_m = jnp.zeros((KPAD - MLA_QK, tk), F32)
    gdq = gdq_ref[...] * (DIFF_QK ** -0.5 * LOG2E)
    third = _mod3(rid)
    gdk = gdk_ref[...]

    for h in range(HEADS):
        base = 2 * DIFF_QK * h
        ks = []
        for j in range(2):
            r0 = base + DIFF_QK * j
            qd_ref[0, h, DIFF_QK * j:DIFF_QK * (j + 1), rows] = _rms_rows(pT[r0:r0 + DIFF_QK], gdq).astype(BF16)
            ks.append(_rms_rows(pT[GROUP + r0:GROUP + r0 + DIFF_QK], gdk))
        s0, s1, s2 = slope_parts[h]
        part = jnp.where(third == 0, s0, jnp.where(third == 1, s1, s2))
        aug = jnp.where(rid < 6, part, jnp.where(rid < 9, c_hi, jnp.where(rid < 12, c_lo, 0.0)))
        kext = jnp.concatenate(ks + [aug, pad_d], axis=0)
        kd_ref[0, h, rows] = kext.T.astype(BF16)
        vd_ref[0, h, si, 0:HEAD_V, :] = pT[2 * GROUP + HEAD_V * h:2 * GROUP + HEAD_V * (h + 1)].astype(BF16)
        vd_ref[0, h, si, HEAD_V:V_ROWS, :] = ones_tile

    o_cq = 3 * GROUP
    o_ckv = o_cq + MLA_Q_RANK
    o_kpe = o_ckv + MLA_KV_RANK
    cqn = _rms_rows(pT[o_cq:o_ckv], gqa_ref[...]).astype(BF16)
    qmT = jnp.dot(wuqT_ref[...], cqn, preferred_element_type=F32)
    ckvn = _rms_rows(pT[o_ckv:o_kpe], gkva_ref[...]).astype(BF16)
    kvT = jnp.dot(wukvT_ref[...], ckvn, preferred_element_type=F32)
    kpe = pT[o_kpe:o_kpe + MLA_ROPE]
    cos = cos_ref[:, rows]
    sin = sin_ref[:, rows]
    gmq = gmq_ref[...] * (MLA_QK ** -0.5 * LOG2E)
    gmk = gmk_ref[...]
    half = MLA_ROPE // 2
    for h in range(HEADS):
        qn = _rms_rows(qmT[MLA_QK * h:MLA_QK * (h + 1)], gmq)
        q1, q2 = _rope_rows(qn[MLA_NOPE:MLA_NOPE + half], qn[MLA_NOPE + half:MLA_QK], cos, sin)
        qm_ref[0, h, 0:MLA_NOPE, rows] = qn[0:MLA_NOPE].astype(BF16)
        qm_ref[0, h, MLA_NOPE:MLA_NOPE + half, rows] = q1.astype(BF16)
        qm_ref[0, h, MLA_NOPE + half:MLA_QK, rows] = q2.astype(BF16)
        kv0 = (MLA_NOPE + HEAD_V) * h
        kn = _rms_rows(jnp.concatenate([kvT[kv0:kv0 + MLA_NOPE], kpe], axis=0), gmk)
        k1, k2 = _rope_rows(kn[MLA_NOPE:MLA_NOPE + half], kn[MLA_NOPE + half:MLA_QK], cos, sin)
        kext = jnp.concatenate([kn[0:MLA_NOPE], k1, k2, pad_m], axis=0)
        km_ref[0, h, rows] = kext.T.astype(BF16)
        vm_ref[0, h, si, 0:HEAD_V, :] = kvT[kv0 + MLA_NOPE:kv0 + MLA_NOPE + HEAD_V].astype(BF16)
        vm_ref[0, h, si, HEAD_V:V_ROWS, :] = ones_tile


def _inproj(x, m, g, w, rope_cos, rope_sin, slope_parts):
    B, S, D = x.shape
    _, _, tk = _tiles(S)
    ts = min(INPROJ_BLOCKS * tk, S)
    nkb = S // tk
    c2 = lambda b, t: (0, 0)
    tok = lambda b, t: (b, t, 0)
    head_T = lambda b, t: (b, 0, 0, t)
    head_N = lambda b, t: (b, 0, t, 0)
    head_V = lambda b, t: (b, 0, t, 0, 0)
    small = [w['gdq'], w['gdk'], w['gqa'], w['gkva'], w['gmq'], w['gmk']]
    in_specs = ([pl.BlockSpec((1, ts, D), tok),
                 pl.BlockSpec((1, N_MOD, D), lambda b, t: (b, 0, 0)),
                 pl.BlockSpec((1, D), c2),
                 pl.BlockSpec(w['w_nat'].shape, c2, pipeline_mode=pl.Buffered(1)),
                 pl.BlockSpec(w['w_T'].shape, c2, pipeline_mode=pl.Buffered(1)),
                 pl.BlockSpec(w['wuqT'].shape, c2, pipeline_mode=pl.Buffered(1)),
                 pl.BlockSpec(w['wukvT'].shape, c2, pipeline_mode=pl.Buffered(1))]
                + [pl.BlockSpec(a.shape, c2) for a in small]
                + [pl.BlockSpec((MLA_ROPE // 2, ts), lambda b, t: (0, t))] * 2)
    out_shape = [jax.ShapeDtypeStruct((B, S, GROUP), F32),
                 jax.ShapeDtypeStruct((B, S, GROUP), F32),
                 jax.ShapeDtypeStruct((B, S, GROUP), BF16),
                 jax.ShapeDtypeStruct((B, HEADS, 2 * DIFF_QK, S), BF16),
                 jax.ShapeDtypeStruct((B, HEADS, S, KPAD), BF16),
                 jax.ShapeDtypeStruct((B, HEADS, nkb, V_ROWS, tk), BF16),
                 jax.ShapeDtypeStruct((B, HEADS, MLA_QK, S), BF16),
                 jax.ShapeDtypeStruct((B, HEADS, S, KPAD), BF16),
                 jax.ShapeDtypeStruct((B, HEADS, nkb, V_ROWS, tk), BF16)]
    out_specs = [pl.BlockSpec((1, ts, GROUP), tok),
                 pl.BlockSpec((1, ts, GROUP), tok),
                 pl.BlockSpec((1, ts, GROUP), tok),
                 pl.BlockSpec((1, HEADS, 2 * DIFF_QK, ts), head_T),
                 pl.BlockSpec((1, HEADS, ts, KPAD), head_N),
                 pl.BlockSpec((1, HEADS, ts // tk, V_ROWS, tk), head_V),
                 pl.BlockSpec((1, HEADS, MLA_QK, ts), head_T),
                 pl.BlockSpec((1, HEADS, ts, KPAD), head_N),
                 pl.BlockSpec((1, HEADS, ts // tk, V_ROWS, tk), head_V)]
    return pl.pallas_call(
        functools.partial(_inproj_kernel, ts=ts, tk=tk, slope_parts=slope_parts),
        grid=(B, S // ts),
        in_specs=in_specs,
        out_specs=out_specs,
        out_shape=out_shape,
        compiler_params=_cparams(("arbitrary", "arbitrary")),
        name="inproj",
    )(x, m, g, w['w_nat'], w['w_T'], w['wuqT'], w['wukvT'], *small, rope_cos, rope_sin)


def _attn_query_tile(qi, sc_ref, qT_ref, k_ref, vT_ref, gsub_ref, lv_ref, o_ref, rhs_scr, p_scr,
                     *, n_maps, alibi, online, tq, tk, nkb, group_size, lambda_init):
    h = pl.program_id(1)
    q0 = pl.multiple_of(qi * tq, tq)
    q = qT_ref[0, 0, :, pl.ds(q0, tq)]
    shift = sc_ref[h, 4]

    def key_rows(kb, n):
        return k_ref[0, 0, pl.ds(pl.multiple_of(kb * tk, tk), n * tk), :]

    if alibi:
        slope2 = sc_ref[h, 0]
        lane = lax.broadcasted_iota(jnp.int32, (1, tq), 1)
        r_hi = ((lane >> 8) << 8).astype(F32)
        r_lo = (lane & 255).astype(F32)
        rid = lax.broadcasted_iota(jnp.int32, (AUG_ROWS, tq), 0)
        third = _mod3(rid)
        part = jnp.where(third == 0, sc_ref[h, 1], jnp.where(third == 1, sc_ref[h, 2], sc_ref[h, 3]))
        aug = jnp.where(rid < 3, -r_hi, jnp.where(rid < 6, -r_lo, jnp.where(rid < 12, part, 0.0))).astype(BF16)
        zq = jnp.zeros((DIFF_QK, tq), BF16)
        zp = jnp.zeros((KPAD - 2 * DIFF_QK - AUG_ROWS, tq), BF16)
        rhs_scr[0] = jnp.concatenate([q[0:DIFF_QK], zq, aug, zp], axis=0)
        rhs_scr[1] = jnp.concatenate([zq, q[DIFF_QK:2 * DIFF_QK], aug, zp], axis=0)
        col = lax.broadcasted_iota(jnp.int32, (1, KPAD), 1)
        is_aug = (col >= 2 * DIFF_QK) & (col < 2 * DIFF_QK + AUG_ROWS)
        flip = jnp.where(is_aug, -1.0, 1.0).astype(BF16)
        keep = jnp.ones((1, KPAD), BF16)
        d_idx = lax.shift_right_logical(q0, int(math.log2(tk)))
    else:
        rhs_scr[0] = jnp.concatenate([q, jnp.zeros((KPAD - q.shape[0], tq), BF16)], axis=0)

    def accumulate(j, u, kb, s, cb, accs, ms):
        vblk = vT_ref[0, 0, kb]
        if online:
            m_new = jnp.maximum(ms[j], jnp.max(s, axis=0, keepdims=True) + cb)
            p = jnp.exp2(s + (cb - m_new)).astype(BF16)
            return (jnp.exp2(ms[j] - m_new) * accs[j] + jnp.dot(vblk, p, preferred_element_type=F32), m_new)
        p_scr[j, u] = jnp.exp2(s + (cb - shift)).astype(BF16)
        return accs[j] + jnp.dot(vblk, p_scr[j, u], preferred_element_type=F32), ms[j]

    def group(g, carry, first):
        accs, ms = carry
        accs, ms = list(accs), list(ms)
        base = g * group_size
        if alibi:
            kbs = [(d_idx + base + u) & (nkb - 1) for u in range(group_size)]
            parts = [key_rows(kb, 1) * jnp.where(kb > d_idx, flip, keep) for kb in kbs]
            if first:
                parts = [parts[0] * flip] + parts
            kgrp = jnp.concatenate(parts, axis=0)
        else:
            kbs = [base + u for u in range(group_size)]
            kgrp = key_rows(base, group_size)
        off = tk if (alibi and first) else 0
        for j in range(n_maps):
            s = jnp.dot(kgrp, rhs_scr[j], preferred_element_type=F32)
            for u, kb in enumerate(kbs):
                s_u = s[off + u * tk:off + (u + 1) * tk]
                if alibi and first and u == 0:
                    s_u = jnp.minimum(s_u, s[0:tk])
                cb = -slope2 * jnp.abs(q0 - kb * tk).astype(F32) if alibi else 0.0
                accs[j], ms[j] = accumulate(j, u, kb, s_u, cb, accs, ms)
        return tuple(accs), tuple(ms)

    carry = (tuple(jnp.zeros((V_ROWS, tq), F32) for _ in range(n_maps)),
             tuple(jnp.full((1, tq), NEG_BIG, F32) for _ in range(n_maps)))
    first_rest = 0
    if alibi:
        carry = group(0, carry, True)
        first_rest = 1
    carry = lax.fori_loop(first_rest, nkb // group_size, lambda g, c: group(g, c, False), carry)
    accs, _ = carry

    a0 = accs[0]
    o = a0[0:HEAD_V] / a0[HEAD_V:HEAD_V + 1]
    if n_maps == 2:
        a1 = accs[1]
        lv = lv_ref[...]
        lam = (jnp.exp(jnp.sum(lv[0:1] * lv[1:2], axis=1, keepdims=True))
               - jnp.exp(jnp.sum(lv[2:3] * lv[3:4], axis=1, keepdims=True)) + lambda_init)
        o = o - lam * (a1[0:HEAD_V] / a1[HEAD_V:HEAD_V + 1])
        o = _rms_rows(o, gsub_ref[...]) * (1.0 - lambda_init)
    o_ref[0, :, pl.ds(q0, tq)] = o.astype(BF16)


def _attn_kernel(*refs, nq, **kw):
    def tile(qi, carry):
        _attn_query_tile(qi, *refs, **kw)
        return carry
    lax.fori_loop(0, nq, tile, 0)


def _attention_call(sc, qT, k, vT, gsub, lv, *, n_maps, alibi, online, lambda_init):
    B, H, kq, S = qT.shape
    _, tq, tk = _tiles(S)
    nkb = S // tk
    assert tq == tk, "one key block per query tile sits on the diagonal"
    group_size = 1 if online else min(ATTN_GROUP, nkb)
    assert nkb % group_size == 0 and nkb & (nkb - 1) == 0, "rotated block order wraps with a mask"
    kern = functools.partial(_attn_kernel, n_maps=n_maps, alibi=alibi, online=online, tq=tq, tk=tk,
                             nkb=nkb, group_size=group_size, lambda_init=lambda_init, nq=S // tq)
    n_rhs = n_maps
    name = ("attn_diff" if alibi else "attn_mla") + ("_online" if online else "")
    return pl.pallas_call(
        kern,
        grid=(B, H),
        in_specs=[pl.BlockSpec(memory_space=pltpu.SMEM),
                  pl.BlockSpec((1, 1, kq, S), lambda b, h: (b, h, 0, 0)),
                  pl.BlockSpec((1, 1, S, KPAD), lambda b, h: (b, h, 0, 0)),
                  pl.BlockSpec((1, 1, nkb, V_ROWS, tk), lambda b, h: (b, h, 0, 0, 0)),
                  pl.BlockSpec(gsub.shape, lambda b, h: (0, 0)),
                  pl.BlockSpec(lv.shape, lambda b, h: (0, 0))],
        out_specs=pl.BlockSpec((1, HEAD_V, S), lambda b, h: (b, h, 0)),
        out_shape=jax.ShapeDtypeStruct((B, H * HEAD_V, S), BF16),
        scratch_shapes=[pltpu.VMEM((n_rhs, KPAD, tq), BF16),
                        pltpu.VMEM((n_maps, group_size, tk, tq), BF16)],
        compiler_params=_cparams(("arbitrary", "arbitrary")),
        name=name,
    )(sc, qT, k, vT, gsub, lv)


def _attention(qT, k, vT, gq, gk, gsub, lv, *, width, n_maps, alibi, lambda_init):
    bound = math.sqrt(width) * BOUND_MARGIN * jnp.max(jnp.abs(gq)) * jnp.max(jnp.abs(gk))
    slopes2 = [np.float32(s * LOG2E) for s in _alibi_slopes()]
    static = jnp.asarray([[s, *_bf16_parts(s)] for s in slopes2], F32)
    sc = jnp.concatenate([static, jnp.full((HEADS, 1), LOG2E, F32) * bound,
                          jnp.zeros((HEADS, 3), F32)], axis=1)
    call = functools.partial(_attention_call, n_maps=n_maps, alibi=alibi, lambda_init=lambda_init)
    return lax.cond(bound <= FAST_SOFTMAX_BOUND,
                    functools.partial(call, online=False),
                    functools.partial(call, online=True),
                    sc, qT, k, vT, gsub, lv)


def _f1_kernel(t_ref, x_ref, o_ref):
    o_ref[0] = jnp.dot(t_ref[...], x_ref[0], preferred_element_type=F32).astype(BF16)


def _f2_kernel(g_ref, a_ref, cs_ref, o_ref, *, tk1, n2, ch, norm):
    xr, xi = [], []
    for j in range(tk1):
        z = jnp.concatenate([a_ref[0, 0, j], a_ref[0, 1, j]], axis=0)
        x = jnp.dot(g_ref[j], z, preferred_element_type=F32)
        xr.append(x[:n2].astype(BF16))
        xi.append(x[n2:].astype(BF16))
    y = (jnp.dot(jnp.concatenate(xr, axis=0), cs_ref[0:ch], preferred_element_type=F32)
         + jnp.dot(jnp.concatenate(xi, axis=0), cs_ref[ch:2 * ch], preferred_element_type=F32))
    for j in range(tk1):
        o_ref[0, :, j * ch:(j + 1) * ch] = (y[j * n2:(j + 1) * n2] * norm).astype(BF16)


def _fourier_tables(S):
    n1, n2 = _fourier_split(S)
    a = jnp.arange(n1, dtype=jnp.int32)
    ang1 = (2.0 * math.pi / n1) * ((a[:, None] * a[None, :]) % n1).astype(F32)
    t1 = jnp.concatenate([jnp.cos(ang1), -jnp.sin(ang1)], axis=0).astype(BF16)
    k = a[:, None, None] + n1 * jnp.arange(n2, dtype=jnp.int32)[None, :, None]
    b = jnp.arange(n2, dtype=jnp.int32)[None, None, :]
    ang = (2.0 * math.pi / S) * ((k * b) % S).astype(F32)
    cg, sg = jnp.cos(ang), jnp.sin(ang)
    g = jnp.concatenate([jnp.concatenate([cg, sg], axis=2),
                         jnp.concatenate([-sg, cg], axis=2)], axis=1).astype(BF16)
    gw = GROUP // FOURIER_GROUPS
    c = jnp.arange(GROUP, dtype=jnp.int32)
    same = (c[:, None] // gw) == (c[None, :] // gw)
    angc = (2.0 * math.pi / gw) * (((c[:, None] % gw) * (c[None, :] % gw)) % gw).astype(F32)
    cs = jnp.concatenate([jnp.where(same, jnp.cos(angc), 0.0),
                          jnp.where(same, jnp.sin(angc), 0.0)], axis=0).astype(BF16)
    return t1, g, cs


def _fourier(fu, tables):
    B, S, C = fu.shape
    n1, n2 = _fourier_split(S)
    t1, g, cs = tables
    tn = min(8192, n2 * C)
    a = pl.pallas_call(
        _f1_kernel,
        grid=(B, n2 * C // tn),
        in_specs=[pl.BlockSpec(t1.shape, lambda b, j: (0, 0)),
                  pl.BlockSpec((1, n1, tn), lambda b, j: (b, 0, j))],
        out_specs=pl.BlockSpec((1, 2 * n1, tn), lambda b, j: (b, 0, j)),
        out_shape=jax.ShapeDtypeStruct((B, 2 * n1, n2 * C), BF16),
        compiler_params=_cparams(("arbitrary", "arbitrary")),
        name="fourier_stage1",
    )(t1, fu.reshape(B, n1, n2 * C))
    tk1 = min(16, n1)
    norm = 1.0 / math.sqrt(S * (GROUP // FOURIER_GROUPS))
    y = pl.pallas_call(
        functools.partial(_f2_kernel, tk1=tk1, n2=n2, ch=C, norm=norm),
        grid=(B, n1 // tk1),
        in_specs=[pl.BlockSpec((tk1, 2 * n2, 2 * n2), lambda b, j: (j, 0, 0)),
                  pl.BlockSpec((1, 2, tk1, n2, C), lambda b, j: (b, 0, j, 0, 0)),
                  pl.BlockSpec(cs.shape, lambda b, j: (0, 0))],
        out_specs=pl.BlockSpec((1, n2, tk1 * C), lambda b, j: (b, 0, j)),
        out_shape=jax.ShapeDtypeStruct((B, n2, n1 * C), BF16),
        compiler_params=_cparams(("arbitrary", "arbitrary")),
        name="fourier_stage2",
    )(g, a.reshape(B, 2, n1, n2, C), cs)
    return y.reshape(B, S, C)


def _outproj_ffn_kernel(x_ref, m_ref, ab_ref, u_ref, up_ref, un_ref, cw_ref, yb_ref, yc_ref, yd_ref,
                        w_ref, g_ref, wgu_ref, wd_ref, o_ref, h_scr, acc_scr, *, ts, nt, nch):
    t = pl.program_id(1)
    u = u_ref[0]
    prev_row = jnp.where(t == 0, 0.0, up_ref[0, 7:8, :])
    next_row = jnp.where(t == nt - 1, 0.0, un_ref[0, 0:1, :])
    rid = lax.broadcasted_iota(jnp.int32, u.shape, 0)
    u_m1 = jnp.where(rid == 0, prev_row, pltpu.roll(u, 1, 0))
    u_p1 = jnp.where(rid == ts - 1, next_row, pltpu.roll(u, ts - 1, 0))
    conv = u_m1 * cw_ref[0:1, :] + u * cw_ref[1:2, :] + u_p1 * cw_ref[2:3, :]
    ya = (ab_ref[0] * conv).astype(BF16)
    tn_dims = (((0,), (0,)), ((), ()))
    acc = jnp.dot(ya, w_ref[0:GROUP], preferred_element_type=F32)
    acc += lax.dot_general(yb_ref[0], w_ref[GROUP:2 * GROUP], tn_dims, preferred_element_type=F32)
    acc += lax.dot_general(yc_ref[0], w_ref[2 * GROUP:3 * GROUP], tn_dims, preferred_element_type=F32)
    acc += jnp.dot(yd_ref[0], w_ref[3 * GROUP:4 * GROUP], preferred_element_type=F32)
    x_mid = x_ref[0] + m_ref[0, 5:6, :] * acc
    o_ref[0] = _ffn_tile(x_mid, m_ref, g_ref, wgu_ref, wd_ref, h_scr, acc_scr, 2, nch)


def _outproj_ffn(x, m, ab, u, conv_w, ybT, ycT, yd, w_out, g, wgu, wd):
    B, S, D = x.shape
    ts, _, _ = _tiles(S)
    nt = S // ts
    r8 = ts // 8
    tok = lambda b, t: (b, t, 0)
    const3 = lambda b, t: (0, 0, 0)
    return pl.pallas_call(
        functools.partial(_outproj_ffn_kernel, ts=ts, nt=nt, nch=wgu.shape[0]),
        grid=(B, nt),
        in_specs=[pl.BlockSpec((1, ts, D), tok),
                  pl.BlockSpec((1, N_MOD, D), lambda b, t: (b, 0, 0)),
                  pl.BlockSpec((1, ts, GROUP), tok),
                  pl.BlockSpec((1, ts, GROUP), tok),
                  pl.BlockSpec((1, 8, GROUP), lambda b, t: (b, jnp.maximum(t * r8 - 1, 0), 0)),
                  pl.BlockSpec((1, 8, GROUP), lambda b, t: (b, jnp.minimum((t + 1) * r8, S // 8 - 1), 0)),
                  pl.BlockSpec(conv_w.shape, lambda b, t: (0, 0)),
                  pl.BlockSpec((1, GROUP, ts), lambda b, t: (b, 0, t)),
                  pl.BlockSpec((1, GROUP, ts), lambda b, t: (b, 0, t)),
                  pl.BlockSpec((1, ts, GROUP), tok),
                  pl.BlockSpec(w_out.shape, lambda b, t: (0, 0), pipeline_mode=pl.Buffered(1)),
                  pl.BlockSpec((1, D), lambda b, t: (0, 0)),
                  pl.BlockSpec(wgu.shape, const3, pipeline_mode=pl.Buffered(1)),
                  pl.BlockSpec(wd.shape, const3, pipeline_mode=pl.Buffered(1))],
        out_specs=pl.BlockSpec((1, ts, D), tok),
        out_shape=jax.ShapeDtypeStruct((B, S, D), F32),
        scratch_shapes=[pltpu.VMEM((ts, D), BF16), pltpu.VMEM((ts, D), F32)],
        compiler_params=_cparams(("arbitrary", "arbitrary")),
        name="outproj_ffn",
    )(x, m, ab, u, u, u, conv_w, ybT, ycT, yd, w_out, g, wgu, wd)


def _rope_tables(S):
    inv = ROPE_THETA ** (-jnp.arange(0, MLA_ROPE, 2, dtype=F32) / MLA_ROPE)
    ang = inv[:, None] * jnp.arange(S, dtype=F32)[None, :]
    return jnp.cos(ang), jnp.sin(ang)


def _alibi_slopes():
    return tuple(float(2.0 ** (-8.0 * (i + 1) / HEADS)) for i in range(HEADS))


def _prep_layer(p, l):
    col = lambda v: v.astype(F32).reshape(-1, 1)
    w_in = p['w_in'][l].astype(BF16)
    nch = p['ffn1_w_gu'].shape[-1] // 2 // FF_CHUNK

    def ffn_w(wgu, wd):
        D = wgu.shape[0]
        g = wgu[:, :nch * FF_CHUNK].reshape(D, nch, FF_CHUNK)
        u = wgu[:, nch * FF_CHUNK:].reshape(D, nch, FF_CHUNK)
        wgu_c = jnp.transpose(jnp.concatenate([g, u], axis=2), (1, 0, 2)).astype(BF16)
        return wgu_c, wd.reshape(nch, FF_CHUNK, D).astype(BF16)

    return dict(
        ffn1=ffn_w(p['ffn1_w_gu'][l], p['ffn1_w_down'][l]),
        ffn2=ffn_w(p['ffn2_w_gu'][l], p['ffn2_w_down'][l]),
        w_nat=jnp.concatenate([w_in[:, :NAT_LO], w_in[:, ATT_HI:]], axis=1),
        w_T=w_in[:, NAT_LO:ATT_HI].T,
        wuqT=p['mla_w_uq'][l].astype(BF16).T,
        wukvT=p['mla_w_ukv'][l].astype(BF16).T,
        gdq=col(p['diff_q_g'][l]), gdk=col(p['diff_k_g'][l]),
        gqa=col(p['mla_q_a_g'][l]), gkva=col(p['mla_kv_a_g'][l]),
        gmq=col(p['mla_q_g'][l]), gmk=col(p['mla_k_g'][l]),
        gsub=col(p['diff_subln_g'][l]),
        lv=p['diff_lambda'][l].astype(F32),
        conv_w=p['conv_w'][l].astype(F32),
        w_out=p['w_out'][l].astype(BF16),
        norm_g=p['norm_g'][l].astype(F32),
    )


def _layer(x, m, w, l, consts):
    rope_cos, rope_sin, ftables = consts
    ng = w['norm_g']
    x = _ffn(x, m, ng[0:1], *w['ffn1'], 0)
    slope_parts = tuple(_bf16_parts(np.float32(s * LOG2E)) for s in _alibi_slopes())
    ab, u, fu, qd, kd, vd, qm, km, vm = _inproj(x, m, ng[1:2], w, rope_cos, rope_sin, slope_parts)
    lambda_init = 0.8 - 0.6 * math.exp(-0.3 * l)
    ybT = _attention(qd, kd, vd, w['gdq'], w['gdk'], w['gsub'], w['lv'],
                     width=DIFF_QK, n_maps=2, alibi=True, lambda_init=lambda_init)
    ycT = _attention(qm, km, vm, w['gmq'], w['gmk'], w['gsub'], w['lv'],
                     width=MLA_QK, n_maps=1, alibi=False, lambda_init=0.0)
    yd = _fourier(fu, ftables)
    return _outproj_ffn(x, m, ab, u, w['conv_w'], ybT, ycT, yd, w['w_out'], ng[2:3], *w['ffn2'])


def _trunk(groups, c_all, p):
    depth = p['w_mod'].shape[0]
    mods = _modulation(c_all, p['w_mod'], p['b_mod'])
    outs = []
    consts = []
    for x, _ in groups:
        S = x.shape[1]
        consts.append(_rope_tables(S) + (_fourier_tables(S),))
    xs = [x for x, _ in groups]
    for l in range(depth):
        w = _prep_layer(p, l)
        for gi, (x0, off) in enumerate(groups):
            B, _, D = x0.shape
            m = mods[l, off:off + B].reshape(B, N_MOD, D)
            xs[gi] = _layer(xs[gi], m, w, l, consts[gi])
    return xs


def kernel(x_prompt, x_sample, c_prompt, c_sample, w_mod, b_mod, norm_g, ffn1_w_gu, ffn1_w_down,
           w_in, conv_w, diff_lambda, diff_q_g, diff_k_g, diff_subln_g, mla_q_a_g, mla_w_uq,
           mla_kv_a_g, mla_w_ukv, mla_q_g, mla_k_g, w_out, ffn2_w_gu, ffn2_w_down):
    p = dict(w_mod=w_mod, b_mod=b_mod, norm_g=norm_g, ffn1_w_gu=ffn1_w_gu, ffn1_w_down=ffn1_w_down,
             w_in=w_in, conv_w=conv_w, diff_lambda=diff_lambda, diff_q_g=diff_q_g, diff_k_g=diff_k_g,
             diff_subln_g=diff_subln_g, mla_q_a_g=mla_q_a_g, mla_w_uq=mla_w_uq, mla_kv_a_g=mla_kv_a_g,
             mla_w_ukv=mla_w_ukv, mla_q_g=mla_q_g, mla_k_g=mla_k_g, w_out=w_out,
             ffn2_w_gu=ffn2_w_gu, ffn2_w_down=ffn2_w_down)
    nb = c_prompt.shape[0] + c_sample.shape[0]
    pad = (-nb) % 8
    c_all = jnp.concatenate([c_prompt, c_sample, jnp.zeros((pad, c_prompt.shape[1]), c_prompt.dtype)])
    y_prompt, y_sample = _trunk([(x_prompt, 0), (x_sample, c_prompt.shape[0])], c_all, p)
    return (y_prompt, y_sample)
```

```python
import functools
import math

import numpy as np
import jax
import jax.numpy as jnp
from jax import lax
from jax.experimental import pallas as pl
from jax.experimental.pallas import tpu as pltpu

F32 = jnp.float32
BF16 = jnp.bfloat16

HEADS = 4
GROUP = 256
DIFF_QK = 32
MLA_NOPE = 64
MLA_ROPE = 32
MLA_QK = MLA_NOPE + MLA_ROPE
MLA_Q_RANK = 256
MLA_KV_RANK = 128
HEAD_V = 64
FOURIER_GROUPS = 4
ROPE_THETA = 10000.0
EPS = 1e-6
N_MOD = 9
ATT_ROWS = 3 * GROUP + MLA_Q_RANK + MLA_KV_RANK + MLA_ROPE
NAT_LO = 3 * GROUP
ATT_HI = NAT_LO + ATT_ROWS

BF16_SUBLANES = 16
KPAD = 128
V_ROWS = HEAD_V + BF16_SUBLANES
AUG_ROWS = BF16_SUBLANES
FF_CHUNK = 256
FFN_TOKENS = 512
TOKEN_TILE = 512
MOD_COLS = 1536
INPROJ_BLOCKS = 2
ATTN_GROUP = 8
VMEM_LIMIT = 52 * 1024 * 1024
NEG_BIG = -1e30
LOG2E = 1.4426950408889634
FAST_SOFTMAX_BOUND = 25.0
BOUND_MARGIN = 1.02


def _mod3(i):
    three = lambda c: jnp.where(c, 3, 0)
    return i - three(i >= 3) - three(i >= 6) - three(i >= 9)


def _bf16_parts(x):
    parts, r = [], np.float32(x)
    for _ in range(3):
        p = np.float32(np.asarray(r, dtype=jnp.bfloat16))
        parts.append(float(p))
        r = np.float32(r - p)
    return tuple(parts)


def _cparams(sem):
    return pltpu.CompilerParams(dimension_semantics=sem, vmem_limit_bytes=VMEM_LIMIT)


def _tiles(S):
    t = min(TOKEN_TILE, S)
    return t, t, t


def _fourier_split(S):
    n1 = 1 << ((int(math.log2(S)) + 1) // 2)
    return n1, S // n1


def _mod_kernel(c_ref, w_ref, b_ref, o_ref):
    c = c_ref[...]
    s = (c * jax.nn.sigmoid(c)).astype(BF16)
    o_ref[0] = jnp.dot(s, w_ref[0].astype(BF16), preferred_element_type=F32) + b_ref[0]


def _modulation(c_all, w_mod, b_mod):
    L, D, N = w_mod.shape
    bp = c_all.shape[0]
    tn = MOD_COLS
    assert N % tn == 0
    return pl.pallas_call(
        _mod_kernel,
        grid=(L, N // tn),
        in_specs=[pl.BlockSpec((bp, D), lambda l, j: (0, 0)),
                  pl.BlockSpec((1, D, tn), lambda l, j: (l, 0, j)),
                  pl.BlockSpec((1, 1, tn), lambda l, j: (l, 0, j))],
        out_specs=pl.BlockSpec((1, bp, tn), lambda l, j: (l, 0, j)),
        out_shape=jax.ShapeDtypeStruct((L, bp, N), F32),
        compiler_params=_cparams(("arbitrary", "arbitrary")),
        name="modulation",
    )(c_all, w_mod, b_mod.reshape(L, 1, N))


def _modulated_norm(x, m_ref, g_ref, sub):
    shift = m_ref[0, 3 * sub:3 * sub + 1, :]
    scale = m_ref[0, 3 * sub + 1:3 * sub + 2, :]
    ms = jnp.mean(x * x, axis=-1, keepdims=True)
    y = x * lax.rsqrt(ms + EPS) * g_ref[...]
    return y * (1.0 + scale) + shift


def _ffn_tile(x, m_ref, g_ref, wgu_ref, wd_ref, h_scr, acc_scr, sub, nch):
    h_scr[...] = _modulated_norm(x, m_ref, g_ref, sub).astype(BF16)
    acc_scr[...] = jnp.zeros_like(acc_scr)

    hidden = nch * FF_CHUNK
    for c in range(nch):
        cols = slice(c * FF_CHUNK, (c + 1) * FF_CHUNK)
        g = jnp.dot(h_scr[...], wgu_ref[:, cols], preferred_element_type=F32)
        u = jnp.dot(h_scr[...], wgu_ref[:, hidden + c * FF_CHUNK:hidden + (c + 1) * FF_CHUNK],
                    preferred_element_type=F32)
        a = (g * jax.nn.sigmoid(g) * u).astype(BF16)
        acc_scr[...] += jnp.dot(a, wd_ref[cols, :], preferred_element_type=F32)
    gate = m_ref[0, 3 * sub + 2:3 * sub + 3, :]
    return x + 0.5 * gate * acc_scr[...]


def _ffn_kernel(x_ref, m_ref, g_ref, wgu_ref, wd_ref, o_ref, h_scr, acc_scr, *, sub, nch):
    o_ref[0] = _ffn_tile(x_ref[0], m_ref, g_ref, wgu_ref, wd_ref, h_scr, acc_scr, sub, nch)


def _ffn(x, m, g, wgu, wd, sub):
    B, S, D = x.shape
    ts = min(FFN_TOKENS, S)
    nch = wd.shape[0] // FF_CHUNK
    const3 = lambda b, t: (0, 0)
    return pl.pallas_call(
        functools.partial(_ffn_kernel, sub=sub, nch=nch),
        grid=(B, S // ts),
        in_specs=[pl.BlockSpec((1, ts, D), lambda b, t: (b, t, 0)),
                  pl.BlockSpec((1, N_MOD, D), lambda b, t: (b, 0, 0)),
                  pl.BlockSpec((1, D), lambda b, t: (0, 0)),
                  pl.BlockSpec(wgu.shape, const3, pipeline_mode=pl.Buffered(1)),
                  pl.BlockSpec(wd.shape, const3, pipeline_mode=pl.Buffered(1))],
        out_specs=pl.BlockSpec((1, ts, D), lambda b, t: (b, t, 0)),
        out_shape=jax.ShapeDtypeStruct((B, S, D), F32),
        scratch_shapes=[pltpu.VMEM((ts, D), BF16), pltpu.VMEM((ts, D), F32)],
        compiler_params=_cparams(("arbitrary", "arbitrary")),
        name="ffn",
    )(x, m, g, wgu, wd)


def _rms_rows(v, gain):
    inv = lax.rsqrt(jnp.mean(v * v, axis=0, keepdims=True) + EPS)
    return v * inv * gain


def _rope_rows(t1, t2, cos, sin):
    return t1 * cos - t2 * sin, t1 * sin + t2 * cos


def _ones_row_tile(tk):
    rid = lax.broadcasted_iota(jnp.int32, (V_ROWS - HEAD_V, tk), 0)
    return jnp.where(rid == 0, 1.0, 0.0).astype(BF16)


def _inproj_kernel(x_ref, m_ref, g_ref, wnat_ref, wT_ref, wuqT_ref, wukvT_ref,
                   gdq_ref, gdk_ref, gqa_ref, gkva_ref, gmq_ref, gmk_ref, cos_ref, sin_ref,
                   ab_ref, u_ref, fu_ref, qd_ref, kd_ref, vd_ref, qm_ref, km_ref, vm_ref,
                   *, ts, tk, slope_parts):
    sub = range(ts // tk)
    projected = [_inproj_project(si, x_ref, m_ref, g_ref, wnat_ref, wT_ref, ab_ref, u_ref, fu_ref, tk)
                 for si in sub]
    for si in sub:
        _inproj_operands(si, projected[si], wuqT_ref, wukvT_ref,
                         gdq_ref, gdk_ref, gqa_ref, gkva_ref, gmq_ref, gmk_ref, cos_ref, sin_ref,
                         qd_ref, kd_ref, vd_ref, qm_ref, km_ref, vm_ref,
                         ts=ts, tk=tk, slope_parts=slope_parts)


def _inproj_project(si, x_ref, m_ref, g_ref, wnat_ref, wT_ref, ab_ref, u_ref, fu_ref, tk):
    rows = slice(si * tk, (si + 1) * tk)
    hb = _modulated_norm(x_ref[0, rows], m_ref, g_ref, 1).astype(BF16)

    nat = jnp.dot(hb, wnat_ref[...], preferred_element_type=F32)
    ab_ref[0, rows] = nat[:, 0:GROUP]
    u_ref[0, rows] = nat[:, GROUP:2 * GROUP] * nat[:, 2 * GROUP:3 * GROUP]
    fu_ref[0, rows] = nat[:, 3 * GROUP:4 * GROUP].astype(BF16)

    return lax.dot_general(wT_ref[...], hb, (((1,), (1,)), ((), ())), preferred_element_type=F32)


def _inproj_operands(si, pT, wuqT_ref, wukvT_ref,
                     gdq_ref, gdk_ref, gqa_ref, gkva_ref, gmq_ref, gmk_ref, cos_ref, sin_ref,
                     qd_ref, kd_ref, vd_ref, qm_ref, km_ref, vm_ref, *, ts, tk, slope_parts):
    t = pl.program_id(1)
    rows = slice(si * tk, (si + 1) * tk)
    ones_tile = _ones_row_tile(tk)
    pos = t * ts + si * tk + lax.broadcasted_iota(jnp.int32, (1, tk), 1)
    c = pos & (tk - 1)
    c_hi = ((c >> 8) << 8).astype(F32)
    c_lo = (c & 255).astype(F32)
    rid = lax.broadcasted_iota(jnp.int32, (AUG_ROWS, tk), 0)
    pad_d = jnp.zeros((KPAD - 2 * DIFF_QK - AUG_ROWS, tk), F32)
    pad_m = jnp.zeros((KPAD - MLA_QK, tk), F32)
    gdq = gdq_ref[...] * (DIFF_QK ** -0.5 * LOG2E)
    third = _mod3(rid)
    gdk = gdk_ref[...]

    for h in range(HEADS):
        base = 2 * DIFF_QK * h
        ks = []
        for j in range(2):
            r0 = base + DIFF_QK * j
            qd_ref[0, h, DIFF_QK * j:DIFF_QK * (j + 1), rows] = _rms_rows(pT[r0:r0 + DIFF_QK], gdq).astype(BF16)
            ks.append(_rms_rows(pT[GROUP + r0:GROUP + r0 + DIFF_QK], gdk))
        s0, s1, s2 = slope_parts[h]
        part = jnp.where(third == 0, s0, jnp.where(third == 1, s1, s2))
        aug = jnp.where(rid < 6, part, jnp.where(rid < 9, c_hi, jnp.where(rid < 12, c_lo, 0.0)))
        kext = jnp.concatenate(ks + [aug, pad_d], axis=0)
        kd_ref[0, h, rows] = kext.T.astype(BF16)
        vd_ref[0, h, si, 0:HEAD_V, :] = pT[2 * GROUP + HEAD_V * h:2 * GROUP + HEAD_V * (h + 1)].astype(BF16)
        vd_ref[0, h, si, HEAD_V:V_ROWS, :] = ones_tile

    o_cq = 3 * GROUP
    o_ckv = o_cq + MLA_Q_RANK
    o_kpe = o_ckv + MLA_KV_RANK
    cqn = _rms_rows(pT[o_cq:o_ckv], gqa_ref[...]).astype(BF16)
    qmT = jnp.dot(wuqT_ref[...], cqn, preferred_element_type=F32)
    ckvn = _rms_rows(pT[o_ckv:o_kpe], gkva_ref[...]).astype(BF16)
    kvT = jnp.dot(wukvT_ref[...], ckvn, preferred_element_type=F32)
    kpe = pT[o_kpe:o_kpe + MLA_ROPE]
    cos = cos_ref[:, rows]
    sin = sin_ref[:, rows]
    gmq = gmq_ref[...] * (MLA_QK ** -0.5 * LOG2E)
    gmk = gmk_ref[...]
    half = MLA_ROPE // 2
    for h in range(HEADS):
        qn = _rms_rows(qmT[MLA_QK * h:MLA_QK * (h + 1)], gmq)
        q1, q2 = _rope_rows(qn[MLA_NOPE:MLA_NOPE + half], qn[MLA_NOPE + half:MLA_QK], cos, sin)
        qm_ref[0, h, 0:MLA_NOPE, rows] = qn[0:MLA_NOPE].astype(BF16)
        qm_ref[0, h, MLA_NOPE:MLA_NOPE + half, rows] = q1.astype(BF16)
        qm_ref[0, h, MLA_NOPE + half:MLA_QK, rows] = q2.astype(BF16)
        kv0 = (MLA_NOPE + HEAD_V) * h
        kn = _rms_rows(jnp.concatenate([kvT[kv0:kv0 + MLA_NOPE], kpe], axis=0), gmk)
        k1, k2 = _rope_rows(kn[MLA_NOPE:MLA_NOPE + half], kn[MLA_NOPE + half:MLA_QK], cos, sin)
        kext = jnp.concatenate([kn[0:MLA_NOPE], k1, k2, pad_m], axis=0)
        km_ref[0, h, rows] = kext.T.astype(BF16)
        vm_ref[0, h, si, 0:HEAD_V, :] = kvT[kv0 + MLA_NOPE:kv0 + MLA_NOPE + HEAD_V].astype(BF16)
        vm_ref[0, h, si, HEAD_V:V_ROWS, :] = ones_tile


def _inproj(x, m, g, w, rope_cos, rope_sin, slope_parts):
    B, S, D = x.shape
    _, _, tk = _tiles(S)
    ts = min(INPROJ_BLOCKS * tk, S)
    nkb = S // tk
    c2 = lambda b, t: (0, 0)
    tok = lambda b, t: (b, t, 0)
    head_T = lambda b, t: (b, 0, 0, t)
    head_N = lambda b, t: (b, 0, t, 0)
    head_V = lambda b, t: (b, 0, t, 0, 0)
    small = [w['gdq'], w['gdk'], w['gqa'], w['gkva'], w['gmq'], w['gmk']]
    in_specs = ([pl.BlockSpec((1, ts, D), tok),
                 pl.BlockSpec((1, N_MOD, D), lambda b, t: (b, 0, 0)),
                 pl.BlockSpec((1, D), c2),
                 pl.BlockSpec(w['w_nat'].shape, c2, pipeline_mode=pl.Buffered(1)),
                 pl.BlockSpec(w['w_T'].shape, c2, pipeline_mode=pl.Buffered(1)),
                 pl.BlockSpec(w['wuqT'].shape, c2, pipeline_mode=pl.Buffered(1)),
                 pl.BlockSpec(w['wukvT'].shape, c2, pipeline_mode=pl.Buffered(1))]
                + [pl.BlockSpec(a.shape, c2) for a in small]
                + [pl.BlockSpec((MLA_ROPE // 2, ts), lambda b, t: (0, t))] * 2)
    out_shape = [jax.ShapeDtypeStruct((B, S, GROUP), F32),
                 jax.ShapeDtypeStruct((B, S, GROUP), F32),
                 jax.ShapeDtypeStruct((B, S, GROUP), BF16),
                 jax.ShapeDtypeStruct((B, HEADS, 2 * DIFF_QK, S), BF16),
                 jax.ShapeDtypeStruct((B, HEADS, S, KPAD), BF16),
                 jax.ShapeDtypeStruct((B, HEADS, nkb, V_ROWS, tk), BF16),
                 jax.ShapeDtypeStruct((B, HEADS, MLA_QK, S), BF16),
                 jax.ShapeDtypeStruct((B, HEADS, S, KPAD), BF16),
                 jax.ShapeDtypeStruct((B, HEADS, nkb, V_ROWS, tk), BF16)]
    out_specs = [pl.BlockSpec((1, ts, GROUP), tok),
                 pl.BlockSpec((1, ts, GROUP), tok),
                 pl.BlockSpec((1, ts, GROUP), tok),
                 pl.BlockSpec((1, HEADS, 2 * DIFF_QK, ts), head_T),
                 pl.BlockSpec((1, HEADS, ts, KPAD), head_N),
                 pl.BlockSpec((1, HEADS, ts // tk, V_ROWS, tk), head_V),
                 pl.BlockSpec((1, HEADS, MLA_QK, ts), head_T),
                 pl.BlockSpec((1, HEADS, ts, KPAD), head_N),
                 pl.BlockSpec((1, HEADS, ts // tk, V_ROWS, tk), head_V)]
    return pl.pallas_call(
        functools.partial(_inproj_kernel, ts=ts, tk=tk, slope_parts=slope_parts),
        grid=(B, S // ts),
        in_specs=in_specs,
        out_specs=out_specs,
        out_shape=out_shape,
        compiler_params=_cparams(("arbitrary", "arbitrary")),
        name="inproj",
    )(x, m, g, w['w_nat'], w['w_T'], w['wuqT'], w['wukvT'], *small, rope_cos, rope_sin)


def _attn_query_tile(qi, sc_ref, qT_ref, k_ref, vT_ref, gsub_ref, lv_ref, o_ref, rhs_scr, p_scr,
                     *, n_maps, alibi, online, tq, tk, nkb, group_size, lambda_init):
    h = pl.program_id(1)
    q0 = pl.multiple_of(qi * tq, tq)
    q = qT_ref[0, 0, :, pl.ds(q0, tq)]
    shift = sc_ref[h, 4]

    def key_rows(kb, n):
        return k_ref[0, 0, pl.ds(pl.multiple_of(kb * tk, tk), n * tk), :]

    if alibi:
        slope2 = sc_ref[h, 0]
        lane = lax.broadcasted_iota(jnp.int32, (1, tq), 1)
        r_hi = ((lane >> 8) << 8).astype(F32)
        r_lo = (lane & 255).astype(F32)
        rid = lax.broadcasted_iota(jnp.int32, (AUG_ROWS, tq), 0)
        third = _mod3(rid)
        part = jnp.where(third == 0, sc_ref[h, 1], jnp.where(third == 1, sc_ref[h, 2], sc_ref[h, 3]))
        aug = jnp.where(rid < 3, -r_hi, jnp.where(rid < 6, -r_lo, jnp.where(rid < 12, part, 0.0))).astype(BF16)
        zq = jnp.zeros((DIFF_QK, tq), BF16)
        zp = jnp.zeros((KPAD - 2 * DIFF_QK - AUG_ROWS, tq), BF16)
        rhs_scr[0] = jnp.concatenate([q[0:DIFF_QK], zq, aug, zp], axis=0)
        rhs_scr[1] = jnp.concatenate([zq, q[DIFF_QK:2 * DIFF_QK], aug, zp], axis=0)
        col = lax.broadcasted_iota(jnp.int32, (1, KPAD), 1)
        is_aug = (col >= 2 * DIFF_QK) & (col < 2 * DIFF_QK + AUG_ROWS)
        flip = jnp.where(is_aug, -1.0, 1.0).astype(BF16)
        keep = jnp.ones((1, KPAD), BF16)
        d_idx = lax.shift_right_logical(q0, int(math.log2(tk)))
    else:
        rhs_scr[0] = jnp.concatenate([q, jnp.zeros((KPAD - q.shape[0], tq), BF16)], axis=0)

    def accumulate(j, u, kb, s, cb, accs, ms):
        vblk = vT_ref[0, 0, kb]
        if online:
            m_new = jnp.maximum(ms[j], jnp.max(s, axis=0, keepdims=True) + cb)
            p = jnp.exp2(s + (cb - m_new)).astype(BF16)
            return (jnp.exp2(ms[j] - m_new) * accs[j] + jnp.dot(vblk, p, preferred_element_type=F32), m_new)
        p_scr[j, u] = jnp.exp2(s + (cb - shift)).astype(BF16)
        return accs[j] + jnp.dot(vblk, p_scr[j, u], preferred_element_type=F32), ms[j]

    def group(g, carry, first):
        accs, ms = carry
        accs, ms = list(accs), list(ms)
        base = g * group_size
        if alibi:
            kbs = [(d_idx + base + u) & (nkb - 1) for u in range(group_size)]
            parts = [key_rows(kb, 1) * jnp.where(kb > d_idx, flip, keep) for kb in kbs]
            if first:
                parts = [parts[0] * flip] + parts
            kgrp = jnp.concatenate(parts, axis=0)
        else:
            kbs = [base + u for u in range(group_size)]
            kgrp = key_rows(base, group_size)
        off = tk if (alibi and first) else 0
        for j in range(n_maps):
            s = jnp.dot(kgrp, rhs_scr[j], preferred_element_type=F32)
            for u, kb in enumerate(kbs):
                s_u = s[off + u * tk:off + (u + 1) * tk]
                if alibi and first and u == 0:
                    s_u = jnp.minimum(s_u, s[0:tk])
                cb = -slope2 * jnp.abs(q0 - kb * tk).astype(F32) if alibi else 0.0
                accs[j], ms[j] = accumulate(j, u, kb, s_u, cb, accs, ms)
        return tuple(accs), tuple(ms)

    carry = (tuple(jnp.zeros((V_ROWS, tq), F32) for _ in range(n_maps)),
             tuple(jnp.full((1, tq), NEG_BIG, F32) for _ in range(n_maps)))
    first_rest = 0
    if alibi:
        carry = group(0, carry, True)
        first_rest = 1
    carry = lax.fori_loop(first_rest, nkb // group_size, lambda g, c: group(g, c, False), carry)
    accs, _ = carry

    a0 = accs[0]
    o = a0[0:HEAD_V] / a0[HEAD_V:HEAD_V + 1]
    if n_maps == 2:
        a1 = accs[1]
        lv = lv_ref[...]
        lam = (jnp.exp(jnp.sum(lv[0:1] * lv[1:2], axis=1, keepdims=True))
               - jnp.exp(jnp.sum(lv[2:3] * lv[3:4], axis=1, keepdims=True)) + lambda_init)
        o = o - lam * (a1[0:HEAD_V] / a1[HEAD_V:HEAD_V + 1])
        o = _rms_rows(o, gsub_ref[...]) * (1.0 - lambda_init)
    o_ref[0, :, pl.ds(q0, tq)] = o.astype(BF16)


def _attn_kernel(*refs, nq, **kw):
    def tile(qi, carry):
        _attn_query_tile(qi, *refs, **kw)
        return carry
    lax.fori_loop(0, nq, tile, 0)


def _attention_call(sc, qT, k, vT, gsub, lv, *, n_maps, alibi, online, lambda_init):
    B, H, kq, S = qT.shape
    _, tq, tk = _tiles(S)
    nkb = S // tk
    assert tq == tk, "one key block per query tile sits on the diagonal"
    group_size = 1 if online else min(ATTN_GROUP, nkb)
    assert nkb % group_size == 0 and nkb & (nkb - 1) == 0, "rotated block order wraps with a mask"
    kern = functools.partial(_attn_kernel, n_maps=n_maps, alibi=alibi, online=online, tq=tq, tk=tk,
                             nkb=nkb, group_size=group_size, lambda_init=lambda_init, nq=S // tq)
    n_rhs = n_maps
    name = ("attn_diff" if alibi else "attn_mla") + ("_online" if online else "")
    return pl.pallas_call(
        kern,
        grid=(B, H),
        in_specs=[pl.BlockSpec(memory_space=pltpu.SMEM),
                  pl.BlockSpec((1, 1, kq, S), lambda b, h: (b, h, 0, 0)),
                  pl.BlockSpec((1, 1, S, KPAD), lambda b, h: (b, h, 0, 0)),
                  pl.BlockSpec((1, 1, nkb, V_ROWS, tk), lambda b, h: (b, h, 0, 0, 0)),
                  pl.BlockSpec(gsub.shape, lambda b, h: (0, 0)),
                  pl.BlockSpec(lv.shape, lambda b, h: (0, 0))],
        out_specs=pl.BlockSpec((1, HEAD_V, S), lambda b, h: (b, h, 0)),
        out_shape=jax.ShapeDtypeStruct((B, H * HEAD_V, S), BF16),
        scratch_shapes=[pltpu.VMEM((n_rhs, KPAD, tq), BF16),
                        pltpu.VMEM((n_maps, group_size, tk, tq), BF16)],
        compiler_params=_cparams(("arbitrary", "arbitrary")),
        name=name,
    )(sc, qT, k, vT, gsub, lv)


def _attention(qT, k, vT, gq, gk, gsub, lv, *, width, n_maps, alibi, lambda_init):
    bound = math.sqrt(width) * BOUND_MARGIN * jnp.max(jnp.abs(gq)) * jnp.max(jnp.abs(gk))
    slopes2 = [np.float32(s * LOG2E) for s in _alibi_slopes()]
    static = jnp.asarray([[s, *_bf16_parts(s)] for s in slopes2], F32)
    sc = jnp.concatenate([static, jnp.full((HEADS, 1), LOG2E, F32) * bound,
                          jnp.zeros((HEADS, 3), F32)], axis=1)
    call = functools.partial(_attention_call, n_maps=n_maps, alibi=alibi, lambda_init=lambda_init)
    return lax.cond(bound <= FAST_SOFTMAX_BOUND,
                    functools.partial(call, online=False),
                    functools.partial(call, online=True),
                    sc, qT, k, vT, gsub, lv)


def _f1_kernel(t_ref, x_ref, o_ref):
    o_ref[0] = jnp.dot(t_ref[...], x_ref[0], preferred_element_type=F32).astype(BF16)


def _f2_kernel(g_ref, a_ref, cs_ref, o_ref, *, tk1, n2, ch, norm):
    xr, xi = [], []
    for j in range(tk1):
        z = jnp.concatenate([a_ref[0, 0, j], a_ref[0, 1, j]], axis=0)
        x = jnp.dot(g_ref[j], z, preferred_element_type=F32)
        xr.append(x[:n2].astype(BF16))
        xi.append(x[n2:].astype(BF16))
    y = (jnp.dot(jnp.concatenate(xr, axis=0), cs_ref[0:ch], preferred_element_type=F32)
         + jnp.dot(jnp.concatenate(xi, axis=0), cs_ref[ch:2 * ch], preferred_element_type=F32))
    for j in range(tk1):
        o_ref[0, :, j * ch:(j + 1) * ch] = (y[j * n2:(j + 1) * n2] * norm).astype(BF16)


def _fourier_tables(S):
    n1, n2 = _fourier_split(S)
    a = jnp.arange(n1, dtype=jnp.int32)
    ang1 = (2.0 * math.pi / n1) * ((a[:, None] * a[None, :]) % n1).astype(F32)
    t1 = jnp.concatenate([jnp.cos(ang1), -jnp.sin(ang1)], axis=0).astype(BF16)
    k = a[:, None, None] + n1 * jnp.arange(n2, dtype=jnp.int32)[None, :, None]
    b = jnp.arange(n2, dtype=jnp.int32)[None, None, :]
    ang = (2.0 * math.pi / S) * ((k * b) % S).astype(F32)
    cg, sg = jnp.cos(ang), jnp.sin(ang)
    g = jnp.concatenate([jnp.concatenate([cg, sg], axis=2),
                         jnp.concatenate([-sg, cg], axis=2)], axis=1).astype(BF16)
    gw = GROUP // FOURIER_GROUPS
    c = jnp.arange(GROUP, dtype=jnp.int32)
    same = (c[:, None] // gw) == (c[None, :] // gw)
    angc = (2.0 * math.pi / gw) * (((c[:, None] % gw) * (c[None, :] % gw)) % gw).astype(F32)
    cs = jnp.concatenate([jnp.where(same, jnp.cos(angc), 0.0),
                          jnp.where(same, jnp.sin(angc), 0.0)], axis=0).astype(BF16)
    return t1, g, cs


def _fourier(fu, tables):
    B, S, C = fu.shape
    n1, n2 = _fourier_split(S)
    t1, g, cs = tables
    tn = min(8192, n2 * C)
    a = pl.pallas_call(
        _f1_kernel,
        grid=(B, n2 * C // tn),
        in_specs=[pl.BlockSpec(t1.shape, lambda b, j: (0, 0)),
                  pl.BlockSpec((1, n1, tn), lambda b, j: (b, 0, j))],
        out_specs=pl.BlockSpec((1, 2 * n1, tn), lambda b, j: (b, 0, j)),
        out_shape=jax.ShapeDtypeStruct((B, 2 * n1, n2 * C), BF16),
        compiler_params=_cparams(("arbitrary", "arbitrary")),
        name="fourier_stage1",
    )(t1, fu.reshape(B, n1, n2 * C))
    tk1 = min(16, n1)
    norm = 1.0 / math.sqrt(S * (GROUP // FOURIER_GROUPS))
    y = pl.pallas_call(
        functools.partial(_f2_kernel, tk1=tk1, n2=n2, ch=C, norm=norm),
        grid=(B, n1 // tk1),
        in_specs=[pl.BlockSpec((tk1, 2 * n2, 2 * n2), lambda b, j: (j, 0, 0)),
                  pl.BlockSpec((1, 2, tk1, n2, C), lambda b, j: (b, 0, j, 0, 0)),
                  pl.BlockSpec(cs.shape, lambda b, j: (0, 0))],
        out_specs=pl.BlockSpec((1, n2, tk1 * C), lambda b, j: (b, 0, j)),
        out_shape=jax.ShapeDtypeStruct((B, n2, n1 * C), BF16),
        compiler_params=_cparams(("arbitrary", "arbitrary")),
        name="fourier_stage2",
    )(g, a.reshape(B, 2, n1, n2, C), cs)
    return y.reshape(B, S, C)


def _outproj_ffn_kernel(x_ref, m_ref, ab_ref, u_ref, up_ref, un_ref, cw_ref, yb_ref, yc_ref, yd_ref,
                        w_ref, g_ref, wgu_ref, wd_ref, o_ref, h_scr, acc_scr, *, ts, nt, nch):
    t = pl.program_id(1)
    u = u_ref[0]
    prev_row = jnp.where(t == 0, 0.0, up_ref[0, 7:8, :])
    next_row = jnp.where(t == nt - 1, 0.0, un_ref[0, 0:1, :])
    rid = lax.broadcasted_iota(jnp.int32, u.shape, 0)
    u_m1 = jnp.where(rid == 0, prev_row, pltpu.roll(u, 1, 0))
    u_p1 = jnp.where(rid == ts - 1, next_row, pltpu.roll(u, ts - 1, 0))
    conv = u_m1 * cw_ref[0:1, :] + u * cw_ref[1:2, :] + u_p1 * cw_ref[2:3, :]
    ya = (ab_ref[0] * conv).astype(BF16)
    tn_dims = (((0,), (0,)), ((), ()))
    acc = jnp.dot(ya, w_ref[0:GROUP], preferred_element_type=F32)
    acc += lax.dot_general(yb_ref[0], w_ref[GROUP:2 * GROUP], tn_dims, preferred_element_type=F32)
    acc += lax.dot_general(yc_ref[0], w_ref[2 * GROUP:3 * GROUP], tn_dims, preferred_element_type=F32)
    acc += jnp.dot(yd_ref[0], w_ref[3 * GROUP:4 * GROUP], preferred_element_type=F32)
    x_mid = x_ref[0] + m_ref[0, 5:6, :] * acc
    o_ref[0] = _ffn_tile(x_mid, m_ref, g_ref, wgu_ref, wd_ref, h_scr, acc_scr, 2, nch)


def _outproj_ffn(x, m, ab, u, conv_w, ybT, ycT, yd, w_out, g, wgu, wd):
    B, S, D = x.shape
    ts, _, _ = _tiles(S)
    nt = S // ts
    r8 = ts // 8
    tok = lambda b, t: (b, t, 0)
    const3 = lambda b, t: (0, 0)
    return pl.pallas_call(
        functools.partial(_outproj_ffn_kernel, ts=ts, nt=nt, nch=wd.shape[0] // FF_CHUNK),
        grid=(B, nt),
        in_specs=[pl.BlockSpec((1, ts, D), tok),
                  pl.BlockSpec((1, N_MOD, D), lambda b, t: (b, 0, 0)),
                  pl.BlockSpec((1, ts, GROUP), tok),
                  pl.BlockSpec((1, ts, GROUP), tok),
                  pl.BlockSpec((1, 8, GROUP), lambda b, t: (b, jnp.maximum(t * r8 - 1, 0), 0)),
                  pl.BlockSpec((1, 8, GROUP), lambda b, t: (b, jnp.minimum((t + 1) * r8, S // 8 - 1), 0)),
                  pl.BlockSpec(conv_w.shape, lambda b, t: (0, 0)),
                  pl.BlockSpec((1, GROUP, ts), lambda b, t: (b, 0, t)),
                  pl.BlockSpec((1, GROUP, ts), lambda b, t: (b, 0, t)),
                  pl.BlockSpec((1, ts, GROUP), tok),
                  pl.BlockSpec(w_out.shape, lambda b, t: (0, 0), pipeline_mode=pl.Buffered(1)),
                  pl.BlockSpec((1, D), lambda b, t: (0, 0)),
                  pl.BlockSpec(wgu.shape, const3, pipeline_mode=pl.Buffered(1)),
                  pl.BlockSpec(wd.shape, const3, pipeline_mode=pl.Buffered(1))],
        out_specs=pl.BlockSpec((1, ts, D), tok),
        out_shape=jax.ShapeDtypeStruct((B, S, D), F32),
        scratch_shapes=[pltpu.VMEM((ts, D), BF16), pltpu.VMEM((ts, D), F32)],
        compiler_params=_cparams(("arbitrary", "arbitrary")),
        name="outproj_ffn",
    )(x, m, ab, u, u, u, conv_w, ybT, ycT, yd, w_out, g, wgu, wd)


def _rope_tables(S):
    inv = ROPE_THETA ** (-jnp.arange(0, MLA_ROPE, 2, dtype=F32) / MLA_ROPE)
    ang = inv[:, None] * jnp.arange(S, dtype=F32)[None, :]
    return jnp.cos(ang), jnp.sin(ang)


def _alibi_slopes():
    return tuple(float(2.0 ** (-8.0 * (i + 1) / HEADS)) for i in range(HEADS))


def _prep_layer(p, l):
    col = lambda v: v.astype(F32).reshape(-1, 1)
    w_in = p['w_in'][l].astype(BF16)

    def ffn_w(wgu, wd):
        return wgu.astype(BF16), wd.astype(BF16)

    return dict(
        ffn1=ffn_w(p['ffn1_w_gu'][l], p['ffn1_w_down'][l]),
        ffn2=ffn_w(p['ffn2_w_gu'][l], p['ffn2_w_down'][l]),
        w_nat=jnp.concatenate([w_in[:, :NAT_LO], w_in[:, ATT_HI:]], axis=1),
        w_T=w_in[:, NAT_LO:ATT_HI].T,
        wuqT=p['mla_w_uq'][l].astype(BF16).T,
        wukvT=p['mla_w_ukv'][l].astype(BF16).T,
        gdq=col(p['diff_q_g'][l]), gdk=col(p['diff_k_g'][l]),
        gqa=col(p['mla_q_a_g'][l]), gkva=col(p['mla_kv_a_g'][l]),
        gmq=col(p['mla_q_g'][l]), gmk=col(p['mla_k_g'][l]),
        gsub=col(p['diff_subln_g'][l]),
        lv=p['diff_lambda'][l].astype(F32),
        conv_w=p['conv_w'][l].astype(F32),
        w_out=p['w_out'][l].astype(BF16),
        norm_g=p['norm_g'][l].astype(F32),
    )


def _layer(x, m, w, l, consts):
    rope_cos, rope_sin, ftables = consts
    ng = w['norm_g']
    x = _ffn(x, m, ng[0:1], *w['ffn1'], 0)
    slope_parts = tuple(_bf16_parts(np.float32(s * LOG2E)) for s in _alibi_slopes())
    ab, u, fu, qd, kd, vd, qm, km, vm = _inproj(x, m, ng[1:2], w, rope_cos, rope_sin, slope_parts)
    lambda_init = 0.8 - 0.6 * math.exp(-0.3 * l)
    ybT = _attention(qd, kd, vd, w['gdq'], w['gdk'], w['gsub'], w['lv'],
                     width=DIFF_QK, n_maps=2, alibi=True, lambda_init=lambda_init)
    ycT = _attention(qm, km, vm, w['gmq'], w['gmk'], w['gsub'], w['lv'],
                     width=MLA_QK, n_maps=1, alibi=False, lambda_init=0.0)
    yd = _fourier(fu, ftables)
    return _outproj_ffn(x, m, ab, u, w['conv_w'], ybT, ycT, yd, w['w_out'], ng[2:3], *w['ffn2'])


def _trunk(groups, c_all, p):
    depth = p['w_mod'].shape[0]
    mods = _modulation(c_all, p['w_mod'], p['b_mod'])
    outs = []
    consts = []
    for x, _ in groups:
        S = x.shape[1]
        consts.append(_rope_tables(S) + (_fourier_tables(S),))
    xs = [x for x, _ in groups]
    for l in range(depth):
        w = _prep_layer(p, l)
        for gi, (x0, off) in enumerate(groups):
            B, _, D = x0.shape
            m = mods[l, off:off + B].reshape(B, N_MOD, D)
            xs[gi] = _layer(xs[gi], m, w, l, consts[gi])
    return xs


def kernel(x_prompt, x_sample, c_prompt, c_sample, w_mod, b_mod, norm_g, ffn1_w_gu, ffn1_w_down,
           w_in, conv_w, diff_lambda, diff_q_g, diff_k_g, diff_subln_g, mla_q_a_g, mla_w_uq,
           mla_kv_a_g, mla_w_ukv, mla_q_g, mla_k_g, w_out, ffn2_w_gu, ffn2_w_down):
    p = dict(w_mod=w_mod, b_mod=b_mod, norm_g=norm_g, ffn1_w_gu=ffn1_w_gu, ffn1_w_down=ffn1_w_down,
             w_in=w_in, conv_w=conv_w, diff_lambda=diff_lambda, diff_q_g=diff_q_g, diff_k_g=diff_k_g,
             diff_subln_g=diff_subln_g, mla_q_a_g=mla_q_a_g, mla_w_uq=mla_w_uq, mla_kv_a_g=mla_kv_a_g,
             mla_w_ukv=mla_w_ukv, mla_q_g=mla_q_g, mla_k_g=mla_k_g, w_out=w_out,
             ffn2_w_gu=ffn2_w_gu, ffn2_w_down=ffn2_w_down)
    nb = c_prompt.shape[0] + c_sample.shape[0]
    pad = (-nb) % 8
    c_all = jnp.concatenate([c_prompt, c_sample, jnp.zeros((pad, c_prompt.shape[1]), c_prompt.dtype)])
    y_prompt, y_sample = _trunk([(x_prompt, 0), (x_sample, c_prompt.shape[0])], c_all, p)
    return (y_prompt, y_sample)
```

```python
import functools
import math

import numpy as np
import jax
import jax.numpy as jnp
from jax import lax
from jax.experimental import pallas as pl
from jax.experimental.pallas import tpu as pltpu

F32 = jnp.float32
BF16 = jnp.bfloat16

HEADS = 4
GROUP = 256
DIFF_QK = 32
MLA_NOPE = 64
MLA_ROPE = 32
MLA_QK = MLA_NOPE + MLA_ROPE
MLA_Q_RANK = 256
MLA_KV_RANK = 128
HEAD_V = 64
FOURIER_GROUPS = 4
ROPE_THETA = 10000.0
EPS = 1e-6
N_MOD = 9
ATT_ROWS = 3 * GROUP + MLA_Q_RANK + MLA_KV_RANK + MLA_ROPE
NAT_LO = 3 * GROUP
ATT_HI = NAT_LO + ATT_ROWS

BF16_SUBLANES = 16
KPAD = 128
V_ROWS = HEAD_V + BF16_SUBLANES
AUG_ROWS = BF16_SUBLANES
FF_CHUNK = 256
FFN_TOKENS = 512
TOKEN_TILE = 512
MOD_COLS = 1536
INPROJ_BLOCKS = 2
ATTN_GROUP = 8
VMEM_LIMIT = 52 * 1024 * 1024
NEG_BIG = -1e30
LOG2E = 1.4426950408889634
FAST_SOFTMAX_BOUND = 25.0
BOUND_MARGIN = 1.02


def _mod3(i):
    three = lambda c: jnp.where(c, 3, 0)
    return i - three(i >= 3) - three(i >= 6) - three(i >= 9)


def _bf16_parts(x):
    parts, r = [], np.float32(x)
    for _ in range(3):
        p = np.float32(np.asarray(r, dtype=jnp.bfloat16))
        parts.append(float(p))
        r = np.float32(r - p)
    return tuple(parts)


def _cparams(sem):
    return pltpu.CompilerParams(dimension_semantics=sem, vmem_limit_bytes=VMEM_LIMIT)


def _tiles(S):
    t = min(TOKEN_TILE, S)
    return t, t, t


def _fourier_split(S):
    n1 = 1 << ((int(math.log2(S)) + 1) // 2)
    return n1, S // n1


def _mod_kernel(c_ref, w_ref, b_ref, o_ref):
    c = c_ref[...]
    s = (c * jax.nn.sigmoid(c)).astype(BF16)
    o_ref[0] = jnp.dot(s, w_ref[0].astype(BF16), preferred_element_type=F32) + b_ref[0]


def _modulation(c_all, w_mod, b_mod):
    L, D, N = w_mod.shape
    bp = c_all.shape[0]
    tn = MOD_COLS
    assert N % tn == 0
    return pl.pallas_call(
        _mod_kernel,
        grid=(L, N // tn),
        in_specs=[pl.BlockSpec((bp, D), lambda l, j: (0, 0)),
                  pl.BlockSpec((1, D, tn), lambda l, j: (l, 0, j)),
                  pl.BlockSpec((1, 1, tn), lambda l, j: (l, 0, j))],
        out_specs=pl.BlockSpec((1, bp, tn), lambda l, j: (l, 0, j)),
        out_shape=jax.ShapeDtypeStruct((L, bp, N), F32),
        compiler_params=_cparams(("arbitrary", "arbitrary")),
        name="modulation",
    )(c_all, w_mod, b_mod.reshape(L, 1, N))


def _modulated_norm(x, m_ref, g_ref, sub):
    shift = m_ref[0, 3 * sub:3 * sub + 1, :]
    scale = m_ref[0, 3 * sub + 1:3 * sub + 2, :]
    ms = jnp.mean(x * x, axis=-1, keepdims=True)
    y = x * lax.rsqrt(ms + EPS) * g_ref[...]
    return y * (1.0 + scale) + shift


def _ffn_tile(x, m_ref, g_ref, wgu_ref, wd_ref, h_scr, acc_scr, sub, nch):
    h_scr[...] = _modulated_norm(x, m_ref, g_ref, sub).astype(BF16)
    acc_scr[...] = jnp.zeros_like(acc_scr)

    hidden = nch * FF_CHUNK
    for c in range(nch):
        cols = slice(c * FF_CHUNK, (c + 1) * FF_CHUNK)
        g = jnp.dot(h_scr[...], wgu_ref[:, cols], preferred_element_type=F32)
        u = jnp.dot(h_scr[...], wgu_ref[:, hidden + c * FF_CHUNK:hidden + (c + 1) * FF_CHUNK],
                    preferred_element_type=F32)
        a = (g * jax.nn.sigmoid(g) * u).astype(BF16)
        acc_scr[...] += jnp.dot(a, wd_ref[cols, :], preferred_element_type=F32)
    gate = m_ref[0, 3 * sub + 2:3 * sub + 3, :]
    return x + 0.5 * gate * acc_scr[...]


def _ffn_kernel(x_ref, m_ref, g_ref, wgu_ref, wd_ref, o_ref, h_scr, acc_scr, *, sub, nch):
    o_ref[0] = _ffn_tile(x_ref[0], m_ref, g_ref, wgu_ref, wd_ref, h_scr, acc_scr, sub, nch)


def _ffn(x, m, g, wgu, wd, sub):
    B, S, D = x.shape
    ts = min(FFN_TOKENS, S)
    nch = wd.shape[0] // FF_CHUNK
    whole = lambda b, t: (0, 0)
    return pl.pallas_call(
        functools.partial(_ffn_kernel, sub=sub, nch=nch),
        grid=(B, S // ts),
        in_specs=[pl.BlockSpec((1, ts, D), lambda b, t: (b, t, 0)),
                  pl.BlockSpec((1, N_MOD, D), lambda b, t: (b, 0, 0)),
                  pl.BlockSpec((1, D), lambda b, t: (0, 0)),
                  pl.BlockSpec(wgu.shape, whole, pipeline_mode=pl.Buffered(1)),
                  pl.BlockSpec(wd.shape, whole, pipeline_mode=pl.Buffered(1))],
        out_specs=pl.BlockSpec((1, ts, D), lambda b, t: (b, t, 0)),
        out_shape=jax.ShapeDtypeStruct((B, S, D), F32),
        scratch_shapes=[pltpu.VMEM((ts, D), BF16), pltpu.VMEM((ts, D), F32)],
        compiler_params=_cparams(("arbitrary", "arbitrary")),
        name="ffn",
    )(x, m, g, wgu, wd)


def _rms_rows(v, gain):
    inv = lax.rsqrt(jnp.mean(v * v, axis=0, keepdims=True) + EPS)
    return v * inv * gain


def _rope_rows(t1, t2, cos, sin):
    return t1 * cos - t2 * sin, t1 * sin + t2 * cos


def _ones_row_tile(tk):
    rid = lax.broadcasted_iota(jnp.int32, (V_ROWS - HEAD_V, tk), 0)
    return jnp.where(rid == 0, 1.0, 0.0).astype(BF16)


def _inproj_kernel(x_ref, m_ref, g_ref, wnat_ref, wT_ref, wuqT_ref, wukvT_ref,
                   gdq_ref, gdk_ref, gqa_ref, gkva_ref, gmq_ref, gmk_ref, cos_ref, sin_ref,
                   ab_ref, u_ref, fu_ref, qd_ref, kd_ref, vd_ref, qm_ref, km_ref, vm_ref,
                   *, ts, tk, slope_parts):
    sub = range(ts // tk)
    projected = [_inproj_project(si, x_ref, m_ref, g_ref, wnat_ref, wT_ref, ab_ref, u_ref, fu_ref, tk)
                 for si in sub]
    for si in sub:
        _inproj_operands(si, projected[si], wuqT_ref, wukvT_ref,
                         gdq_ref, gdk_ref, gqa_ref, gkva_ref, gmq_ref, gmk_ref, cos_ref, sin_ref,
                         qd_ref, kd_ref, vd_ref, qm_ref, km_ref, vm_ref,
                         ts=ts, tk=tk, slope_parts=slope_parts)


def _inproj_project(si, x_ref, m_ref, g_ref, wnat_ref, wT_ref, ab_ref, u_ref, fu_ref, tk):
    rows = slice(si * tk, (si + 1) * tk)
    hb = _modulated_norm(x_ref[0, rows], m_ref, g_ref, 1).astype(BF16)

    nat = jnp.dot(hb, wnat_ref[...], preferred_element_type=F32)
    ab_ref[0, rows] = nat[:, 0:GROUP]
    u_ref[0, rows] = nat[:, GROUP:2 * GROUP] * nat[:, 2 * GROUP:3 * GROUP]
    fu_ref[0, rows] = nat[:, 3 * GROUP:4 * GROUP].astype(BF16)

    return lax.dot_general(wT_ref[...], hb, (((1,), (1,)), ((), ())), preferred_element_type=F32)


def _inproj_operands(si, pT, wuqT_ref, wukvT_ref,
                     gdq_ref, gdk_ref, gqa_ref, gkva_ref, gmq_ref, gmk_ref, cos_ref, sin_ref,
                     qd_ref, kd_ref, vd_ref, qm_ref, km_ref, vm_ref, *, ts, tk, slope_parts):
    t = pl.program_id(1)
    rows = slice(si * tk, (si + 1) * tk)
    ones_tile = _ones_row_tile(tk)
    pos = t * ts + si * tk + lax.broadcasted_iota(jnp.int32, (1, tk), 1)
    c = pos & (tk - 1)
    c_hi = ((c >> 8) << 8).astype(F32)
    c_lo = (c & 255).astype(F32)
    rid = lax.broadcasted_iota(jnp.int32, (AUG_ROWS, tk), 0)
    pad_d = jnp.zeros((KPAD - 2 * DIFF_QK - AUG_ROWS, tk), F32)
    pad_m = jnp.zeros((KPAD - MLA_QK, tk), F32)
    gdq = gdq_ref[...] * (DIFF_QK ** -0.5 * LOG2E)
    third = _mod3(rid)
    gdk = gdk_ref[...]

    for h in range(HEADS):
        base = 2 * DIFF_QK * h
        ks = []
        for j in range(2):
            r0 = base + DIFF_QK * j
            qd_ref[0, h, DIFF_QK * j:DIFF_QK * (j + 1), rows] = _rms_rows(pT[r0:r0 + DIFF_QK], gdq).astype(BF16)
            ks.append(_rms_rows(pT[GROUP + r0:GROUP + r0 + DIFF_QK], gdk))
        s0, s1, s2 = slope_parts[h]
        part = jnp.where(third == 0, s0, jnp.where(third == 1, s1, s2))
        aug = jnp.where(rid < 6, part, jnp.where(rid < 9, c_hi, jnp.where(rid < 12, c_lo, 0.0)))
        kext = jnp.concatenate(ks + [aug, pad_d], axis=0)
        kd_ref[0, h, rows] = kext.T.astype(BF16)
        vd_ref[0, h, si, 0:HEAD_V, :] = pT[2 * GROUP + HEAD_V * h:2 * GROUP + HEAD_V * (h + 1)].astype(BF16)
        vd_ref[0, h, si, HEAD_V:V_ROWS, :] = ones_tile

    o_cq = 3 * GROUP
    o_ckv = o_cq + MLA_Q_RANK
    o_kpe = o_ckv + MLA_KV_RANK
    cqn = _rms_rows(pT[o_cq:o_ckv], gqa_ref[...]).astype(BF16)
    qmT = jnp.dot(wuqT_ref[...], cqn, preferred_element_type=F32)
    ckvn = _rms_rows(pT[o_ckv:o_kpe], gkva_ref[...]).astype(BF16)
    kvT = jnp.dot(wukvT_ref[...], ckvn, preferred_element_type=F32)
    kpe = pT[o_kpe:o_kpe + MLA_ROPE]
    cos = cos_ref[:, rows]
    sin = sin_ref[:, rows]
    gmq = gmq_ref[...] * (MLA_QK ** -0.5 * LOG2E)
    gmk = gmk_ref[...]
    half = MLA_ROPE // 2
    for h in range(HEADS):
        qn = _rms_rows(qmT[MLA_QK * h:MLA_QK * (h + 1)], gmq)
        q1, q2 = _rope_rows(qn[MLA_NOPE:MLA_NOPE + half], qn[MLA_NOPE + half:MLA_QK], cos, sin)
        qm_ref[0, h, 0:MLA_NOPE, rows] = qn[0:MLA_NOPE].astype(BF16)
        qm_ref[0, h, MLA_NOPE:MLA_NOPE + half, rows] = q1.astype(BF16)
        qm_ref[0, h, MLA_NOPE + half:MLA_QK, rows] = q2.astype(BF16)
        kv0 = (MLA_NOPE + HEAD_V) * h
        kn = _rms_rows(jnp.concatenate([kvT[kv0:kv0 + MLA_NOPE], kpe], axis=0), gmk)
        k1, k2 = _rope_rows(kn[MLA_NOPE:MLA_NOPE + half], kn[MLA_NOPE + half:MLA_QK], cos, sin)
        kext = jnp.concatenate([kn[0:MLA_NOPE], k1, k2, pad_m], axis=0)
        km_ref[0, h, rows] = kext.T.astype(BF16)
        vm_ref[0, h, si, 0:HEAD_V, :] = kvT[kv0 + MLA_NOPE:kv0 + MLA_NOPE + HEAD_V].astype(BF16)
        vm_ref[0, h, si, HEAD_V:V_ROWS, :] = ones_tile


def _inproj(x, m, g, w, rope_cos, rope_sin, slope_parts):
    B, S, D = x.shape
    _, _, tk = _tiles(S)
    ts = min(INPROJ_BLOCKS * tk, S)
    nkb = S // tk
    c2 = lambda b, t: (0, 0)
    tok = lambda b, t: (b, t, 0)
    head_T = lambda b, t: (b, 0, 0, t)
    head_N = lambda b, t: (b, 0, t, 0)
    head_V = lambda b, t: (b, 0, t, 0, 0)
    small = [w['gdq'], w['gdk'], w['gqa'], w['gkva'], w['gmq'], w['gmk']]
    in_specs = ([pl.BlockSpec((1, ts, D), tok),
                 pl.BlockSpec((1, N_MOD, D), lambda b, t: (b, 0, 0)),
                 pl.BlockSpec((1, D), c2),
                 pl.BlockSpec(w['w_nat'].shape, c2, pipeline_mode=pl.Buffered(1)),
                 pl.BlockSpec(w['w_T'].shape, c2, pipeline_mode=pl.Buffered(1)),
                 pl.BlockSpec(w['wuqT'].shape, c2, pipeline_mode=pl.Buffered(1)),
                 pl.BlockSpec(w['wukvT'].shape, c2, pipeline_mode=pl.Buffered(1))]
                + [pl.BlockSpec(a.shape, c2) for a in small]
                + [pl.BlockSpec((MLA_ROPE // 2, ts), lambda b, t: (0, t))] * 2)
    out_shape = [jax.ShapeDtypeStruct((B, S, GROUP), F32),
                 jax.ShapeDtypeStruct((B, S, GROUP), F32),
                 jax.ShapeDtypeStruct((B, S, GROUP), BF16),
                 jax.ShapeDtypeStruct((B, HEADS, 2 * DIFF_QK, S), BF16),
                 jax.ShapeDtypeStruct((B, HEADS, S, KPAD), BF16),
                 jax.ShapeDtypeStruct((B, HEADS, nkb, V_ROWS, tk), BF16),
                 jax.ShapeDtypeStruct((B, HEADS, MLA_QK, S), BF16),
                 jax.ShapeDtypeStruct((B, HEADS, S, KPAD), BF16),
                 jax.ShapeDtypeStruct((B, HEADS, nkb, V_ROWS, tk), BF16)]
    out_specs = [pl.BlockSpec((1, ts, GROUP), tok),
                 pl.BlockSpec((1, ts, GROUP), tok),
                 pl.BlockSpec((1, ts, GROUP), tok),
                 pl.BlockSpec((1, HEADS, 2 * DIFF_QK, ts), head_T),
                 pl.BlockSpec((1, HEADS, ts, KPAD), head_N),
                 pl.BlockSpec((1, HEADS, ts // tk, V_ROWS, tk), head_V),
                 pl.BlockSpec((1, HEADS, MLA_QK, ts), head_T),
                 pl.BlockSpec((1, HEADS, ts, KPAD), head_N),
                 pl.BlockSpec((1, HEADS, ts // tk, V_ROWS, tk), head_V)]
    return pl.pallas_call(
        functools.partial(_inproj_kernel, ts=ts, tk=tk, slope_parts=slope_parts),
        grid=(B, S // ts),
        in_specs=in_specs,
        out_specs=out_specs,
        out_shape=out_shape,
        compiler_params=_cparams(("arbitrary", "arbitrary")),
        name="inproj",
    )(x, m, g, w['w_nat'], w['w_T'], w['wuqT'], w['wukvT'], *small, rope_cos, rope_sin)


def _attn_query_tile(qi, sc_ref, qT_ref, k_ref, vT_ref, gsub_ref, lv_ref, o_ref, rhs_scr, p_scr,
                     *, n_maps, alibi, online, tq, tk, nkb, group_size, lambda_init):
    h = pl.program_id(1)
    q0 = pl.multiple_of(qi * tq, tq)
    q = qT_ref[0, 0, :, pl.ds(q0, tq)]
    shift = sc_ref[h, 4]

    def key_rows(kb, n):
        return k_ref[0, 0, pl.ds(pl.multiple_of(kb * tk, tk), n * tk), :]

    if alibi:
        slope2 = sc_ref[h, 0]
        lane = lax.broadcasted_iota(jnp.int32, (1, tq), 1)
        r_hi = ((lane >> 8) << 8).astype(F32)
        r_lo = (lane & 255).astype(F32)
        rid = lax.broadcasted_iota(jnp.int32, (AUG_ROWS, tq), 0)
        third = _mod3(rid)
        part = jnp.where(third == 0, sc_ref[h, 1], jnp.where(third == 1, sc_ref[h, 2], sc_ref[h, 3]))
        aug = jnp.where(rid < 3, -r_hi, jnp.where(rid < 6, -r_lo, jnp.where(rid < 12, part, 0.0))).astype(BF16)
        zq = jnp.zeros((DIFF_QK, tq), BF16)
        zp = jnp.zeros((KPAD - 2 * DIFF_QK - AUG_ROWS, tq), BF16)
        rhs_scr[0] = jnp.concatenate([q[0:DIFF_QK], zq, aug, zp], axis=0)
        rhs_scr[1] = jnp.concatenate([zq, q[DIFF_QK:2 * DIFF_QK], aug, zp], axis=0)
        col = lax.broadcasted_iota(jnp.int32, (1, KPAD), 1)
        is_aug = (col >= 2 * DIFF_QK) & (col < 2 * DIFF_QK + AUG_ROWS)
        flip = jnp.where(is_aug, -1.0, 1.0).astype(BF16)
        keep = jnp.ones((1, KPAD), BF16)
        d_idx = lax.shift_right_logical(q0, int(math.log2(tk)))
    else:
        rhs_scr[0] = jnp.concatenate([q, jnp.zeros((KPAD - q.shape[0], tq), BF16)], axis=0)

    def accumulate(j, u, kb, s, cb, accs, ms):
        vblk = vT_ref[0, 0, kb]
        if online:
            m_new = jnp.maximum(ms[j], jnp.max(s, axis=0, keepdims=True) + cb)
            p = jnp.exp2(s + (cb - m_new)).astype(BF16)
            return (jnp.exp2(ms[j] - m_new) * accs[j] + jnp.dot(vblk, p, preferred_element_type=F32), m_new)
        p_scr[j, u] = jnp.exp2(s + (cb - shift)).astype(BF16)
        return accs[j] + jnp.dot(vblk, p_scr[j, u], preferred_element_type=F32), ms[j]

    def group(g, carry, first):
        accs, ms = carry
        accs, ms = list(accs), list(ms)
        base = g * group_size
        if alibi:
            kbs = [(d_idx + base + u) & (nkb - 1) for u in range(group_size)]
            parts = [key_rows(kb, 1) * jnp.where(kb > d_idx, flip, keep) for kb in kbs]
            if first:
                parts = [parts[0] * flip] + parts
            kgrp = jnp.concatenate(parts, axis=0)
        else:
            kbs = [base + u for u in range(group_size)]
            kgrp = key_rows(base, group_size)
        off = tk if (alibi and first) else 0
        for j in range(n_maps):
            s = jnp.dot(kgrp, rhs_scr[j], preferred_element_type=F32)
            for u, kb in enumerate(kbs):
                s_u = s[off + u * tk:off + (u + 1) * tk]
                if alibi and first and u == 0:
                    s_u = jnp.minimum(s_u, s[0:tk])
                cb = -slope2 * jnp.abs(q0 - kb * tk).astype(F32) if alibi else 0.0
                accs[j], ms[j] = accumulate(j, u, kb, s_u, cb, accs, ms)
        return tuple(accs), tuple(ms)

    carry = (tuple(jnp.zeros((V_ROWS, tq), F32) for _ in range(n_maps)),
             tuple(jnp.full((1, tq), NEG_BIG, F32) for _ in range(n_maps)))
    first_rest = 0
    if alibi:
        carry = group(0, carry, True)
        first_rest = 1
    carry = lax.fori_loop(first_rest, nkb // group_size, lambda g, c: group(g, c, False), carry)
    accs, _ = carry

    a0 = accs[0]
    o = a0[0:HEAD_V] / a0[HEAD_V:HEAD_V + 1]
    if n_maps == 2:
        a1 = accs[1]
        lv = lv_ref[...]
        lam = (jnp.exp(jnp.sum(lv[0:1] * lv[1:2], axis=1, keepdims=True))
               - jnp.exp(jnp.sum(lv[2:3] * lv[3:4], axis=1, keepdims=True)) + lambda_init)
        o = o - lam * (a1[0:HEAD_V] / a1[HEAD_V:HEAD_V + 1])
        o = _rms_rows(o, gsub_ref[...]) * (1.0 - lambda_init)
    o_ref[0, :, pl.ds(q0, tq)] = o.astype(BF16)


def _attn_kernel(*refs, nq, **kw):
    def tile(qi, carry):
        _attn_query_tile(qi, *refs, **kw)
        return carry
    lax.fori_loop(0, nq, tile, 0)


def _attention_call(sc, qT, k, vT, gsub, lv, *, n_maps, alibi, online, lambda_init):
    B, H, kq, S = qT.shape
    _, tq, tk = _tiles(S)
    nkb = S // tk
    assert tq == tk, "one key block per query tile sits on the diagonal"
    group_size = 1 if online else min(ATTN_GROUP, nkb)
    assert nkb % group_size == 0 and nkb & (nkb - 1) == 0, "rotated block order wraps with a mask"
    kern = functools.partial(_attn_kernel, n_maps=n_maps, alibi=alibi, online=online, tq=tq, tk=tk,
                             nkb=nkb, group_size=group_size, lambda_init=lambda_init, nq=S // tq)
    n_rhs = n_maps
    name = ("attn_diff" if alibi else "attn_mla") + ("_online" if online else "")
    return pl.pallas_call(
        kern,
        grid=(B, H),
        in_specs=[pl.BlockSpec(memory_space=pltpu.SMEM),
                  pl.BlockSpec((1, 1, kq, S), lambda b, h: (b, h, 0, 0)),
                  pl.BlockSpec((1, 1, S, KPAD), lambda b, h: (b, h, 0, 0)),
                  pl.BlockSpec((1, 1, nkb, V_ROWS, tk), lambda b, h: (b, h, 0, 0, 0)),
                  pl.BlockSpec(gsub.shape, lambda b, h: (0, 0)),
                  pl.BlockSpec(lv.shape, lambda b, h: (0, 0))],
        out_specs=pl.BlockSpec((1, HEAD_V, S), lambda b, h: (b, h, 0)),
        out_shape=jax.ShapeDtypeStruct((B, H * HEAD_V, S), BF16),
        scratch_shapes=[pltpu.VMEM((n_rhs, KPAD, tq), BF16),
                        pltpu.VMEM((n_maps, group_size, tk, tq), BF16)],
        compiler_params=_cparams(("arbitrary", "arbitrary")),
        name=name,
    )(sc, qT, k, vT, gsub, lv)


def _attention(qT, k, vT, gq, gk, gsub, lv, *, width, n_maps, alibi, lambda_init):
    bound = math.sqrt(width) * BOUND_MARGIN * jnp.max(jnp.abs(gq)) * jnp.max(jnp.abs(gk))
    slopes2 = [np.float32(s * LOG2E) for s in _alibi_slopes()]
    static = jnp.asarray([[s, *_bf16_parts(s)] for s in slopes2], F32)
    sc = jnp.concatenate([static, jnp.full((HEADS, 1), LOG2E, F32) * bound,
                          jnp.zeros((HEADS, 3), F32)], axis=1)
    call = functools.partial(_attention_call, n_maps=n_maps, alibi=alibi, lambda_init=lambda_init)
    return lax.cond(bound <= FAST_SOFTMAX_BOUND,
                    functools.partial(call, online=False),
                    functools.partial(call, online=True),
                    sc, qT, k, vT, gsub, lv)


def _f1_kernel(t_ref, x_ref, o_ref):
    o_ref[0] = jnp.dot(t_ref[...], x_ref[0], preferred_element_type=F32).astype(BF16)


def _f2_kernel(g_ref, a_ref, cs_ref, o_ref, *, tk1, n2, ch, norm):
    xr, xi = [], []
    for j in range(tk1):
        z = jnp.concatenate([a_ref[0, 0, j], a_ref[0, 1, j]], axis=0)
        x = jnp.dot(g_ref[j], z, preferred_element_type=F32)
        xr.append(x[:n2].astype(BF16))
        xi.append(x[n2:].astype(BF16))
    y = (jnp.dot(jnp.concatenate(xr, axis=0), cs_ref[0:ch], preferred_element_type=F32)
         + jnp.dot(jnp.concatenate(xi, axis=0), cs_ref[ch:2 * ch], preferred_element_type=F32))
    for j in range(tk1):
        o_ref[0, :, j, :] = y[j * n2:(j + 1) * n2] * norm


def _fourier_tables(S):
    n1, n2 = _fourier_split(S)
    a = jnp.arange(n1, dtype=jnp.int32)
    ang1 = (2.0 * math.pi / n1) * ((a[:, None] * a[None, :]) % n1).astype(F32)
    t1 = jnp.concatenate([jnp.cos(ang1), -jnp.sin(ang1)], axis=0).astype(BF16)
    k = a[:, None, None] + n1 * jnp.arange(n2, dtype=jnp.int32)[None, :, None]
    b = jnp.arange(n2, dtype=jnp.int32)[None, None, :]
    ang = (2.0 * math.pi / S) * ((k * b) % S).astype(F32)
    cg, sg = jnp.cos(ang), jnp.sin(ang)
    g = jnp.concatenate([jnp.concatenate([cg, sg], axis=2),
                         jnp.concatenate([-sg, cg], axis=2)], axis=1).astype(BF16)
    gw = GROUP // FOURIER_GROUPS
    c = jnp.arange(GROUP, dtype=jnp.int32)
    same = (c[:, None] // gw) == (c[None, :] // gw)
    angc = (2.0 * math.pi / gw) * (((c[:, None] % gw) * (c[None, :] % gw)) % gw).astype(F32)
    cs = jnp.concatenate([jnp.where(same, jnp.cos(angc), 0.0),
                          jnp.where(same, jnp.sin(angc), 0.0)], axis=0).astype(BF16)
    return t1, g, cs


def _fourier(fu, tables):
    B, S, C = fu.shape
    n1, n2 = _fourier_split(S)
    t1, g, cs = tables
    tn = min(8192, n2 * C)
    a = pl.pallas_call(
        _f1_kernel,
        grid=(B, n2 * C // tn),
        in_specs=[pl.BlockSpec(t1.shape, lambda b, j: (0, 0)),
                  pl.BlockSpec((1, n1, tn), lambda b, j: (b, 0, j))],
        out_specs=pl.BlockSpec((1, 2 * n1, tn), lambda b, j: (b, 0, j)),
        out_shape=jax.ShapeDtypeStruct((B, 2 * n1, n2 * C), BF16),
        compiler_params=_cparams(("arbitrary", "arbitrary")),
        name="fourier_stage1",
    )(t1, fu.reshape(B, n1, n2 * C))
    tk1 = min(16, n1)
    norm = 1.0 / math.sqrt(S * (GROUP // FOURIER_GROUPS))
    y = pl.pallas_call(
        functools.partial(_f2_kernel, tk1=tk1, n2=n2, ch=C, norm=norm),
        grid=(B, n1 // tk1),
        in_specs=[pl.BlockSpec((tk1, 2 * n2, 2 * n2), lambda b, j: (j, 0, 0)),
                  pl.BlockSpec((1, 2, tk1, n2, C), lambda b, j: (b, 0, j, 0, 0)),
                  pl.BlockSpec(cs.shape, lambda b, j: (0, 0))],
        out_specs=pl.BlockSpec((1, n2, tk1, C), lambda b, j: (b, 0, j, 0)),
        out_shape=jax.ShapeDtypeStruct((B, n2, n1, C), F32),
        compiler_params=_cparams(("arbitrary", "arbitrary")),
        name="fourier_stage2",
    )(g, a.reshape(B, 2, n1, n2, C), cs)
    return y.reshape(B, S, C)


def _outproj_ffn_kernel(x_ref, m_ref, ab_ref, u_ref, up_ref, un_ref, cw_ref, yb_ref, yc_ref, yd_ref,
                        w_ref, g_ref, wgu_ref, wd_ref, o_ref, h_scr, acc_scr, *, ts, nt, nch):
    t = pl.program_id(1)
    u = u_ref[0]
    prev_row = jnp.where(t == 0, 0.0, up_ref[0, 7:8, :])
    next_row = jnp.where(t == nt - 1, 0.0, un_ref[0, 0:1, :])
    rid = lax.broadcasted_iota(jnp.int32, u.shape, 0)
    u_m1 = jnp.where(rid == 0, prev_row, pltpu.roll(u, 1, 0))
    u_p1 = jnp.where(rid == ts - 1, next_row, pltpu.roll(u, ts - 1, 0))
    conv = u_m1 * cw_ref[0:1, :] + u * cw_ref[1:2, :] + u_p1 * cw_ref[2:3, :]
    ya = (ab_ref[0] * conv).astype(BF16)
    tn_dims = (((0,), (0,)), ((), ()))
    acc = jnp.dot(ya, w_ref[0:GROUP], preferred_element_type=F32)
    acc += lax.dot_general(yb_ref[0], w_ref[GROUP:2 * GROUP], tn_dims, preferred_element_type=F32)
    acc += lax.dot_general(yc_ref[0], w_ref[2 * GROUP:3 * GROUP], tn_dims, preferred_element_type=F32)
    acc += jnp.dot(yd_ref[0].astype(BF16), w_ref[3 * GROUP:4 * GROUP], preferred_element_type=F32)
    x_mid = x_ref[0] + m_ref[0, 5:6, :] * acc
    o_ref[0] = _ffn_tile(x_mid, m_ref, g_ref, wgu_ref, wd_ref, h_scr, acc_scr, 2, nch)


def _outproj_ffn(x, m, ab, u, conv_w, ybT, ycT, yd, w_out, g, wgu, wd):
    B, S, D = x.shape
    ts, _, _ = _tiles(S)
    nt = S // ts
    r8 = ts // 8
    tok = lambda b, t: (b, t, 0)
    whole = lambda b, t: (0, 0)
    return pl.pallas_call(
        functools.partial(_outproj_ffn_kernel, ts=ts, nt=nt, nch=wd.shape[0] // FF_CHUNK),
        grid=(B, nt),
        in_specs=[pl.BlockSpec((1, ts, D), tok),
                  pl.BlockSpec((1, N_MOD, D), lambda b, t: (b, 0, 0)),
                  pl.BlockSpec((1, ts, GROUP), tok),
                  pl.BlockSpec((1, ts, GROUP), tok),
                  pl.BlockSpec((1, 8, GROUP), lambda b, t: (b, jnp.maximum(t * r8 - 1, 0), 0)),
                  pl.BlockSpec((1, 8, GROUP), lambda b, t: (b, jnp.minimum((t + 1) * r8, S // 8 - 1), 0)),
                  pl.BlockSpec(conv_w.shape, lambda b, t: (0, 0)),
                  pl.BlockSpec((1, GROUP, ts), lambda b, t: (b, 0, t)),
                  pl.BlockSpec((1, GROUP, ts), lambda b, t: (b, 0, t)),
                  pl.BlockSpec((1, ts, GROUP), tok),
                  pl.BlockSpec(w_out.shape, lambda b, t: (0, 0), pipeline_mode=pl.Buffered(1)),
                  pl.BlockSpec((1, D), lambda b, t: (0, 0)),
                  pl.BlockSpec(wgu.shape, whole, pipeline_mode=pl.Buffered(1)),
                  pl.BlockSpec(wd.shape, whole, pipeline_mode=pl.Buffered(1))],
        out_specs=pl.BlockSpec((1, ts, D), tok),
        out_shape=jax.ShapeDtypeStruct((B, S, D), F32),
        scratch_shapes=[pltpu.VMEM((ts, D), BF16), pltpu.VMEM((ts, D), F32)],
        compiler_params=_cparams(("arbitrary", "arbitrary")),
        name="outproj_ffn",
    )(x, m, ab, u, u, u, conv_w, ybT, ycT, yd, w_out, g, wgu, wd)


def _rope_tables(S):
    inv = ROPE_THETA ** (-jnp.arange(0, MLA_ROPE, 2, dtype=F32) / MLA_ROPE)
    ang = inv[:, None] * jnp.arange(S, dtype=F32)[None, :]
    return jnp.cos(ang), jnp.sin(ang)


def _alibi_slopes():
    return tuple(float(2.0 ** (-8.0 * (i + 1) / HEADS)) for i in range(HEADS))


def _prep_layer(p, l):
    col = lambda v: v.astype(F32).reshape(-1, 1)
    w_in = p['w_in'][l].astype(BF16)
    return dict(
        ffn1=(p['ffn1_w_gu'][l].astype(BF16), p['ffn1_w_down'][l].astype(BF16)),
        ffn2=(p['ffn2_w_gu'][l].astype(BF16), p['ffn2_w_down'][l].astype(BF16)),
        w_nat=jnp.concatenate([w_in[:, :NAT_LO], w_in[:, ATT_HI:]], axis=1),
        w_T=w_in[:, NAT_LO:ATT_HI].T,
        wuqT=p['mla_w_uq'][l].astype(BF16).T,
        wukvT=p['mla_w_ukv'][l].astype(BF16).T,
        gdq=col(p['diff_q_g'][l]), gdk=col(p['diff_k_g'][l]),
        gqa=col(p['mla_q_a_g'][l]), gkva=col(p['mla_kv_a_g'][l]),
        gmq=col(p['mla_q_g'][l]), gmk=col(p['mla_k_g'][l]),
        gsub=col(p['diff_subln_g'][l]),
        lv=p['diff_lambda'][l].astype(F32),
        conv_w=p['conv_w'][l].astype(F32),
        w_out=p['w_out'][l].astype(BF16),
        norm_g=p['norm_g'][l].astype(F32),
    )


def _layer(x, m, w, l, consts):
    rope_cos, rope_sin, ftables = consts
    ng = w['norm_g']
    x = _ffn(x, m, ng[0:1], *w['ffn1'], 0)
    slope_parts = tuple(_bf16_parts(np.float32(s * LOG2E)) for s in _alibi_slopes())
    ab, u, fu, qd, kd, vd, qm, km, vm = _inproj(x, m, ng[1:2], w, rope_cos, rope_sin, slope_parts)
    lambda_init = 0.8 - 0.6 * math.exp(-0.3 * l)
    ybT = _attention(qd, kd, vd, w['gdq'], w['gdk'], w['gsub'], w['lv'],
                     width=DIFF_QK, n_maps=2, alibi=True, lambda_init=lambda_init)
    ycT = _attention(qm, km, vm, w['gmq'], w['gmk'], w['gsub'], w['lv'],
                     width=MLA_QK, n_maps=1, alibi=False, lambda_init=0.0)
    yd = _fourier(fu, ftables)
    return _outproj_ffn(x, m, ab, u, w['conv_w'], ybT, ycT, yd, w['w_out'], ng[2:3], *w['ffn2'])


def _trunk(groups, c_all, p):
    depth = p['w_mod'].shape[0]
    mods = _modulation(c_all, p['w_mod'], p['b_mod'])
    consts = []
    for x, _ in groups:
        S = x.shape[1]
        consts.append(_rope_tables(S) + (_fourier_tables(S),))
    xs = [x for x, _ in groups]
    for l in range(depth):
        w = _prep_layer(p, l)
        for gi, (x0, off) in enumerate(groups):
            B, _, D = x0.shape
            m = mods[l, off:off + B].reshape(B, N_MOD, D)
            xs[gi] = _layer(xs[gi], m, w, l, consts[gi])
    return xs


def kernel(x_prompt, x_sample, c_prompt, c_sample, w_mod, b_mod, norm_g, ffn1_w_gu, ffn1_w_down,
           w_in, conv_w, diff_lambda, diff_q_g, diff_k_g, diff_subln_g, mla_q_a_g, mla_w_uq,
           mla_kv_a_g, mla_w_ukv, mla_q_g, mla_k_g, w_out, ffn2_w_gu, ffn2_w_down):
    p = dict(w_mod=w_mod, b_mod=b_mod, norm_g=norm_g, ffn1_w_gu=ffn1_w_gu, ffn1_w_down=ffn1_w_down,
             w_in=w_in, conv_w=conv_w, diff_lambda=diff_lambda, diff_q_g=diff_q_g, diff_k_g=diff_k_g,
             diff_subln_g=diff_subln_g, mla_q_a_g=mla_q_a_g, mla_w_uq=mla_w_uq, mla_kv_a_g=mla_kv_a_g,
             mla_w_ukv=mla_w_ukv, mla_q_g=mla_q_g, mla_k_g=mla_k_g, w_out=w_out,
             ffn2_w_gu=ffn2_w_gu, ffn2_w_down=ffn2_w_down)
    nb = c_prompt.shape[0] + c_sample.shape[0]
    pad = (-nb) % 8
    c_all = jnp.concatenate([c_prompt, c_sample, jnp.zeros((pad, c_prompt.shape[1]), c_prompt.dtype)])
    y_prompt, y_sample = _trunk([(x_prompt, 0), (x_sample, c_prompt.shape[0])], c_all, p)
    return (y_prompt, y_sample)
```

```python
import functools
import math

import numpy as np
import jax
import jax.numpy as jnp
from jax import lax
from jax.experimental import pallas as pl
from jax.experimental.pallas import tpu as pltpu

F32 = jnp.float32
BF16 = jnp.bfloat16

HEADS = 4
GROUP = 256
DIFF_QK = 32
MLA_NOPE = 64
MLA_ROPE = 32
MLA_QK = MLA_NOPE + MLA_ROPE
MLA_Q_RANK = 256
MLA_KV_RANK = 128
HEAD_V = 64
FOURIER_GROUPS = 4
ROPE_THETA = 10000.0
EPS = 1e-6
N_MOD = 9
ATT_ROWS = 3 * GROUP + MLA_Q_RANK + MLA_KV_RANK + MLA_ROPE
NAT_LO = 3 * GROUP
ATT_HI = NAT_LO + ATT_ROWS

BF16_SUBLANES = 16
KPAD = 128
V_ROWS = 128
AUG_ROWS = BF16_SUBLANES
FF_CHUNK = 256
FFN_TOKENS = 512
TOKEN_TILE = 512
MOD_COLS = 1536
INPROJ_BLOCKS = 2
ATTN_GROUP = 8
VMEM_LIMIT = 52 * 1024 * 1024
NEG_BIG = -1e30
LOG2E = 1.4426950408889634
FAST_SOFTMAX_BOUND = 25.0
BOUND_MARGIN = 1.02


def _mod3(i):
    three = lambda c: jnp.where(c, 3, 0)
    return i - three(i >= 3) - three(i >= 6) - three(i >= 9)


def _bf16_parts(x):
    parts, r = [], np.float32(x)
    for _ in range(3):
        p = np.float32(np.asarray(r, dtype=jnp.bfloat16))
        parts.append(float(p))
        r = np.float32(r - p)
    return tuple(parts)


def _cparams(sem):
    return pltpu.CompilerParams(dimension_semantics=sem, vmem_limit_bytes=VMEM_LIMIT)


def _tiles(S):
    t = min(TOKEN_TILE, S)
    return t, t, t


def _fourier_split(S):
    n1 = 1 << ((int(math.log2(S)) + 1) // 2)
    return n1, S // n1


def _mod_kernel(c_ref, w_ref, b_ref, o_ref):
    c = c_ref[...]
    s = (c * jax.nn.sigmoid(c)).astype(BF16)
    o_ref[0] = jnp.dot(s, w_ref[0].astype(BF16), preferred_element_type=F32) + b_ref[0]


def _modulation(c_all, w_mod, b_mod):
    L, D, N = w_mod.shape
    bp = c_all.shape[0]
    tn = MOD_COLS
    assert N % tn == 0
    return pl.pallas_call(
        _mod_kernel,
        grid=(L, N // tn),
        in_specs=[pl.BlockSpec((bp, D), lambda l, j: (0, 0)),
                  pl.BlockSpec((1, D, tn), lambda l, j: (l, 0, j)),
                  pl.BlockSpec((1, 1, tn), lambda l, j: (l, 0, j))],
        out_specs=pl.BlockSpec((1, bp, tn), lambda l, j: (l, 0, j)),
        out_shape=jax.ShapeDtypeStruct((L, bp, N), F32),
        compiler_params=_cparams(("arbitrary", "arbitrary")),
        name="modulation",
    )(c_all, w_mod, b_mod.reshape(L, 1, N))


def _modulated_norm(x, m_ref, g_ref, sub):
    shift = m_ref[0, 3 * sub:3 * sub + 1, :]
    scale = m_ref[0, 3 * sub + 1:3 * sub + 2, :]
    ms = jnp.mean(x * x, axis=-1, keepdims=True)
    y = x * lax.rsqrt(ms + EPS) * g_ref[...]
    return y * (1.0 + scale) + shift


def _ffn_tile(x, m_ref, g_ref, wgu_ref, wd_ref, h_scr, acc_scr, sub, nch):
    h_scr[...] = _modulated_norm(x, m_ref, g_ref, sub).astype(BF16)
    acc_scr[...] = jnp.zeros_like(acc_scr)

    hidden = nch * FF_CHUNK
    for c in range(nch):
        cols = slice(c * FF_CHUNK, (c + 1) * FF_CHUNK)
        g = jnp.dot(h_scr[...], wgu_ref[:, cols], preferred_element_type=F32)
        u = jnp.dot(h_scr[...], wgu_ref[:, hidden + c * FF_CHUNK:hidden + (c + 1) * FF_CHUNK],
                    preferred_element_type=F32)
        a = (g * jax.nn.sigmoid(g) * u).astype(BF16)
        acc_scr[...] += jnp.dot(a, wd_ref[cols, :], preferred_element_type=F32)
    gate = m_ref[0, 3 * sub + 2:3 * sub + 3, :]
    return x + 0.5 * gate * acc_scr[...]


def _ffn_kernel(x_ref, m_ref, g_ref, wgu_ref, wd_ref, o_ref, h_scr, acc_scr, *, sub, nch):
    o_ref[0] = _ffn_tile(x_ref[0], m_ref, g_ref, wgu_ref, wd_ref, h_scr, acc_scr, sub, nch)


def _ffn(x, m, g, wgu, wd, sub):
    B, S, D = x.shape
    ts = min(FFN_TOKENS, S)
    nch = wd.shape[0] // FF_CHUNK
    whole = lambda b, t: (0, 0)
    return pl.pallas_call(
        functools.partial(_ffn_kernel, sub=sub, nch=nch),
        grid=(B, S // ts),
        in_specs=[pl.BlockSpec((1, ts, D), lambda b, t: (b, t, 0)),
                  pl.BlockSpec((1, N_MOD, D), lambda b, t: (b, 0, 0)),
                  pl.BlockSpec((1, D), lambda b, t: (0, 0)),
                  pl.BlockSpec(wgu.shape, whole, pipeline_mode=pl.Buffered(1)),
                  pl.BlockSpec(wd.shape, whole, pipeline_mode=pl.Buffered(1))],
        out_specs=pl.BlockSpec((1, ts, D), lambda b, t: (b, t, 0)),
        out_shape=jax.ShapeDtypeStruct((B, S, D), F32),
        scratch_shapes=[pltpu.VMEM((ts, D), BF16), pltpu.VMEM((ts, D), F32)],
        compiler_params=_cparams(("arbitrary", "arbitrary")),
        name="ffn",
    )(x, m, g, wgu, wd)


def _rms_rows(v, gain):
    inv = lax.rsqrt(jnp.mean(v * v, axis=0, keepdims=True) + EPS)
    return v * inv * gain


def _rope_rows(t1, t2, cos, sin):
    return t1 * cos - t2 * sin, t1 * sin + t2 * cos


def _ones_row_tile(tk):
    rid = lax.broadcasted_iota(jnp.int32, (V_ROWS - HEAD_V, tk), 0)
    return jnp.where(rid == 0, 1.0, 0.0).astype(BF16)


def _inproj_kernel(x_ref, m_ref, g_ref, wnat_ref, wT_ref, wuqT_ref, wukvT_ref,
                   gdq_ref, gdk_ref, gqa_ref, gkva_ref, gmq_ref, gmk_ref, cos_ref, sin_ref,
                   ab_ref, u_ref, fu_ref, qd_ref, kd_ref, vd_ref, qm_ref, km_ref, vm_ref,
                   *, ts, tk, slope_parts):
    sub = range(ts // tk)
    projected = [_inproj_project(si, x_ref, m_ref, g_ref, wnat_ref, wT_ref, ab_ref, u_ref, fu_ref, tk)
                 for si in sub]
    for si in sub:
        _inproj_operands(si, projected[si], wuqT_ref, wukvT_ref,
                         gdq_ref, gdk_ref, gqa_ref, gkva_ref, gmq_ref, gmk_ref, cos_ref, sin_ref,
                         qd_ref, kd_ref, vd_ref, qm_ref, km_ref, vm_ref,
                         ts=ts, tk=tk, slope_parts=slope_parts)


def _inproj_project(si, x_ref, m_ref, g_ref, wnat_ref, wT_ref, ab_ref, u_ref, fu_ref, tk):
    rows = slice(si * tk, (si + 1) * tk)
    hb = _modulated_norm(x_ref[0, rows], m_ref, g_ref, 1).astype(BF16)

    nat = jnp.dot(hb, wnat_ref[...], preferred_element_type=F32)
    ab_ref[0, rows] = nat[:, 0:GROUP]
    u_ref[0, rows] = nat[:, GROUP:2 * GROUP] * nat[:, 2 * GROUP:3 * GROUP]
    fu_ref[0, rows] = nat[:, 3 * GROUP:4 * GROUP].astype(BF16)

    return lax.dot_general(wT_ref[...], hb, (((1,), (1,)), ((), ())), preferred_element_type=F32)


def _inproj_operands(si, pT, wuqT_ref, wukvT_ref,
                     gdq_ref, gdk_ref, gqa_ref, gkva_ref, gmq_ref, gmk_ref, cos_ref, sin_ref,
                     qd_ref, kd_ref, vd_ref, qm_ref, km_ref, vm_ref, *, ts, tk, slope_parts):
    t = pl.program_id(1)
    rows = slice(si * tk, (si + 1) * tk)
    ones_tile = _ones_row_tile(tk)
    pos = t * ts + si * tk + lax.broadcasted_iota(jnp.int32, (1, tk), 1)
    c = pos & (tk - 1)
    c_hi = ((c >> 8) << 8).astype(F32)
    c_lo = (c & 255).astype(F32)
    rid = lax.broadcasted_iota(jnp.int32, (AUG_ROWS, tk), 0)
    pad_d = jnp.zeros((KPAD - 2 * DIFF_QK - AUG_ROWS, tk), F32)
    pad_m = jnp.zeros((KPAD - MLA_QK, tk), F32)
    gdq = gdq_ref[...] * (DIFF_QK ** -0.5 * LOG2E)
    third = _mod3(rid)
    gdk = gdk_ref[...]

    for h in range(HEADS):
        base = 2 * DIFF_QK * h
        ks = []
        for j in range(2):
            r0 = base + DIFF_QK * j
            qd_ref[0, h, DIFF_QK * j:DIFF_QK * (j + 1), rows] = _rms_rows(pT[r0:r0 + DIFF_QK], gdq).astype(BF16)
            ks.append(_rms_rows(pT[GROUP + r0:GROUP + r0 + DIFF_QK], gdk))
        s0, s1, s2 = slope_parts[h]
        part = jnp.where(third == 0, s0, jnp.where(third == 1, s1, s2))
        aug = jnp.where(rid < 6, part, jnp.where(rid < 9, c_hi, jnp.where(rid < 12, c_lo, 0.0)))
        kext = jnp.concatenate(ks + [aug, pad_d], axis=0)
        kd_ref[0, h, rows] = kext.T.astype(BF16)
        vd_ref[0, h, si, 0:HEAD_V, :] = pT[2 * GROUP + HEAD_V * h:2 * GROUP + HEAD_V * (h + 1)].astype(BF16)
        vd_ref[0, h, si, HEAD_V:V_ROWS, :] = ones_tile

    o_cq = 3 * GROUP
    o_ckv = o_cq + MLA_Q_RANK
    o_kpe = o_ckv + MLA_KV_RANK
    cqn = _rms_rows(pT[o_cq:o_ckv], gqa_ref[...]).astype(BF16)
    qmT = jnp.dot(wuqT_ref[...], cqn, preferred_element_type=F32)
    ckvn = _rms_rows(pT[o_ckv:o_kpe], gkva_ref[...]).astype(BF16)
    kvT = jnp.dot(wukvT_ref[...], ckvn, preferred_element_type=F32)
    kpe = pT[o_kpe:o_kpe + MLA_ROPE]
    cos = cos_ref[:, rows]
    sin = sin_ref[:, rows]
    gmq = gmq_ref[...] * (MLA_QK ** -0.5 * LOG2E)
    gmk = gmk_ref[...]
    half = MLA_ROPE // 2
    for h in range(HEADS):
        qn = _rms_rows(qmT[MLA_QK * h:MLA_QK * (h + 1)], gmq)
        q1, q2 = _rope_rows(qn[MLA_NOPE:MLA_NOPE + half], qn[MLA_NOPE + half:MLA_QK], cos, sin)
        qm_ref[0, h, 0:MLA_NOPE, rows] = qn[0:MLA_NOPE].astype(BF16)
        qm_ref[0, h, MLA_NOPE:MLA_NOPE + half, rows] = q1.astype(BF16)
        qm_ref[0, h, MLA_NOPE + half:MLA_QK, rows] = q2.astype(BF16)
        kv0 = (MLA_NOPE + HEAD_V) * h
        kn = _rms_rows(jnp.concatenate([kvT[kv0:kv0 + MLA_NOPE], kpe], axis=0), gmk)
        k1, k2 = _rope_rows(kn[MLA_NOPE:MLA_NOPE + half], kn[MLA_NOPE + half:MLA_QK], cos, sin)
        kext = jnp.concatenate([kn[0:MLA_NOPE], k1, k2, pad_m], axis=0)
        km_ref[0, h, rows] = kext.T.astype(BF16)
        vm_ref[0, h, si, 0:HEAD_V, :] = kvT[kv0 + MLA_NOPE:kv0 + MLA_NOPE + HEAD_V].astype(BF16)
        vm_ref[0, h, si, HEAD_V:V_ROWS, :] = ones_tile


def _inproj(x, m, g, w, rope_cos, rope_sin, slope_parts):
    B, S, D = x.shape
    _, _, tk = _tiles(S)
    ts = min(INPROJ_BLOCKS * tk, S)
    nkb = S // tk
    c2 = lambda b, t: (0, 0)
    tok = lambda b, t: (b, t, 0)
    head_T = lambda b, t: (b, 0, 0, t)
    head_N = lambda b, t: (b, 0, t, 0)
    head_V = lambda b, t: (b, 0, t, 0, 0)
    small = [w['gdq'], w['gdk'], w['gqa'], w['gkva'], w['gmq'], w['gmk']]
    in_specs = ([pl.BlockSpec((1, ts, D), tok),
                 pl.BlockSpec((1, N_MOD, D), lambda b, t: (b, 0, 0)),
                 pl.BlockSpec((1, D), c2),
                 pl.BlockSpec(w['w_nat'].shape, c2, pipeline_mode=pl.Buffered(1)),
                 pl.BlockSpec(w['w_T'].shape, c2, pipeline_mode=pl.Buffered(1)),
                 pl.BlockSpec(w['wuqT'].shape, c2, pipeline_mode=pl.Buffered(1)),
                 pl.BlockSpec(w['wukvT'].shape, c2, pipeline_mode=pl.Buffered(1))]
                + [pl.BlockSpec(a.shape, c2) for a in small]
                + [pl.BlockSpec((MLA_ROPE // 2, ts), lambda b, t: (0, t))] * 2)
    out_shape = [jax.ShapeDtypeStruct((B, S, GROUP), F32),
                 jax.ShapeDtypeStruct((B, S, GROUP), F32),
                 jax.ShapeDtypeStruct((B, S, GROUP), BF16),
                 jax.ShapeDtypeStruct((B, HEADS, 2 * DIFF_QK, S), BF16),
                 jax.ShapeDtypeStruct((B, HEADS, S, KPAD), BF16),
                 jax.ShapeDtypeStruct((B, HEADS, nkb, V_ROWS, tk), BF16),
                 jax.ShapeDtypeStruct((B, HEADS, MLA_QK, S), BF16),
                 jax.ShapeDtypeStruct((B, HEADS, S, KPAD), BF16),
                 jax.ShapeDtypeStruct((B, HEADS, nkb, V_ROWS, tk), BF16)]
    out_specs = [pl.BlockSpec((1, ts, GROUP), tok),
                 pl.BlockSpec((1, ts, GROUP), tok),
                 pl.BlockSpec((1, ts, GROUP), tok),
                 pl.BlockSpec((1, HEADS, 2 * DIFF_QK, ts), head_T),
                 pl.BlockSpec((1, HEADS, ts, KPAD), head_N),
                 pl.BlockSpec((1, HEADS, ts // tk, V_ROWS, tk), head_V),
                 pl.BlockSpec((1, HEADS, MLA_QK, ts), head_T),
                 pl.BlockSpec((1, HEADS, ts, KPAD), head_N),
                 pl.BlockSpec((1, HEADS, ts // tk, V_ROWS, tk), head_V)]
    return pl.pallas_call(
        functools.partial(_inproj_kernel, ts=ts, tk=tk, slope_parts=slope_parts),
        grid=(B, S // ts),
        in_specs=in_specs,
        out_specs=out_specs,
        out_shape=out_shape,
        compiler_params=_cparams(("arbitrary", "arbitrary")),
        name="inproj",
    )(x, m, g, w['w_nat'], w['w_T'], w['wuqT'], w['wukvT'], *small, rope_cos, rope_sin)


def _attn_query_tile(qi, sc_ref, qT_ref, k_ref, vT_ref, gsub_ref, lv_ref, o_ref, rhs_scr, p_scr,
                     *, n_maps, alibi, online, tq, tk, nkb, group_size, lambda_init):
    h = pl.program_id(1)
    q0 = pl.multiple_of(qi * tq, tq)
    q = qT_ref[0, 0, :, pl.ds(q0, tq)]
    shift = sc_ref[h, 4]

    def key_rows(kb, n):
        return k_ref[0, 0, pl.ds(pl.multiple_of(kb * tk, tk), n * tk), :]

    if alibi:
        slope2 = sc_ref[h, 0]
        lane = lax.broadcasted_iota(jnp.int32, (1, tq), 1)
        r_hi = ((lane >> 8) << 8).astype(F32)
        r_lo = (lane & 255).astype(F32)
        rid = lax.broadcasted_iota(jnp.int32, (AUG_ROWS, tq), 0)
        third = _mod3(rid)
        part = jnp.where(third == 0, sc_ref[h, 1], jnp.where(third == 1, sc_ref[h, 2], sc_ref[h, 3]))
        aug = jnp.where(rid < 3, -r_hi, jnp.where(rid < 6, -r_lo, jnp.where(rid < 12, part, 0.0))).astype(BF16)
        zq = jnp.zeros((DIFF_QK, tq), BF16)
        zp = jnp.zeros((KPAD - 2 * DIFF_QK - AUG_ROWS, tq), BF16)
        rhs_scr[0] = jnp.concatenate([q[0:DIFF_QK], zq, aug, zp], axis=0)
        rhs_scr[1] = jnp.concatenate([zq, q[DIFF_QK:2 * DIFF_QK], aug, zp], axis=0)
        col = lax.broadcasted_iota(jnp.int32, (1, KPAD), 1)
        is_aug = (col >= 2 * DIFF_QK) & (col < 2 * DIFF_QK + AUG_ROWS)
        flip = jnp.where(is_aug, -1.0, 1.0).astype(BF16)
        keep = jnp.ones((1, KPAD), BF16)
        d_idx = lax.shift_right_logical(q0, int(math.log2(tk)))
    else:
        rhs_scr[0] = jnp.concatenate([q, jnp.zeros((KPAD - q.shape[0], tq), BF16)], axis=0)

    def accumulate(j, u, kb, s, cb, accs, ms):
        vblk = vT_ref[0, 0, kb]
        if online:
            m_new = jnp.maximum(ms[j], jnp.max(s, axis=0, keepdims=True) + cb)
            p = jnp.exp2(s + (cb - m_new)).astype(BF16)
            return (jnp.exp2(ms[j] - m_new) * accs[j] + jnp.dot(vblk, p, preferred_element_type=F32), m_new)
        p_scr[j, u] = jnp.exp2(s + (cb - shift)).astype(BF16)
        return accs[j] + jnp.dot(vblk, p_scr[j, u], preferred_element_type=F32), ms[j]

    def group(g, carry, first):
        accs, ms = carry
        accs, ms = list(accs), list(ms)
        base = g * group_size
        if alibi:
            kbs = [(d_idx + base + u) & (nkb - 1) for u in range(group_size)]
            parts = [key_rows(kb, 1) * jnp.where(kb > d_idx, flip, keep) for kb in kbs]
            if first:
                parts = [parts[0] * flip] + parts
            kgrp = jnp.concatenate(parts, axis=0)
        else:
            kbs = [base + u for u in range(group_size)]
            kgrp = key_rows(base, group_size)
        off = tk if (alibi and first) else 0
        for j in range(n_maps):
            s = jnp.dot(kgrp, rhs_scr[j], preferred_element_type=F32)
            for u, kb in enumerate(kbs):
                s_u = s[off + u * tk:off + (u + 1) * tk]
                if alibi and first and u == 0:
                    s_u = jnp.minimum(s_u, s[0:tk])
                cb = -slope2 * jnp.abs(q0 - kb * tk).astype(F32) if alibi else 0.0
                accs[j], ms[j] = accumulate(j, u, kb, s_u, cb, accs, ms)
        return tuple(accs), tuple(ms)

    carry = (tuple(jnp.zeros((V_ROWS, tq), F32) for _ in range(n_maps)),
             tuple(jnp.full((1, tq), NEG_BIG, F32) for _ in range(n_maps)))
    first_rest = 0
    if alibi:
        carry = group(0, carry, True)
        first_rest = 1
    carry = lax.fori_loop(first_rest, nkb // group_size, lambda g, c: group(g, c, False), carry)
    accs, _ = carry

    a0 = accs[0]
    o = a0[0:HEAD_V] / a0[HEAD_V:HEAD_V + 1]
    if n_maps == 2:
        a1 = accs[1]
        lv = lv_ref[...]
        lam = (jnp.exp(jnp.sum(lv[0:1] * lv[1:2], axis=1, keepdims=True))
               - jnp.exp(jnp.sum(lv[2:3] * lv[3:4], axis=1, keepdims=True)) + lambda_init)
        o = o - lam * (a1[0:HEAD_V] / a1[HEAD_V:HEAD_V + 1])
        o = _rms_rows(o, gsub_ref[...]) * (1.0 - lambda_init)
    o_ref[0, :, pl.ds(q0, tq)] = o.astype(BF16)


def _attn_kernel(*refs, nq, **kw):
    def tile(qi, carry):
        _attn_query_tile(qi, *refs, **kw)
        return carry
    lax.fori_loop(0, nq, tile, 0)


def _attention_call(sc, qT, k, vT, gsub, lv, *, n_maps, alibi, online, lambda_init):
    B, H, kq, S = qT.shape
    _, tq, tk = _tiles(S)
    nkb = S // tk
    assert tq == tk, "one key block per query tile sits on the diagonal"
    group_size = 1 if online else min(ATTN_GROUP, nkb)
    assert nkb % group_size == 0 and nkb & (nkb - 1) == 0, "rotated block order wraps with a mask"
    kern = functools.partial(_attn_kernel, n_maps=n_maps, alibi=alibi, online=online, tq=tq, tk=tk,
                             nkb=nkb, group_size=group_size, lambda_init=lambda_init, nq=S // tq)
    n_rhs = n_maps
    name = ("attn_diff" if alibi else "attn_mla") + ("_online" if online else "")
    return pl.pallas_call(
        kern,
        grid=(B, H),
        in_specs=[pl.BlockSpec(memory_space=pltpu.SMEM),
                  pl.BlockSpec((1, 1, kq, S), lambda b, h: (b, h, 0, 0)),
                  pl.BlockSpec((1, 1, S, KPAD), lambda b, h: (b, h, 0, 0)),
                  pl.BlockSpec((1, 1, nkb, V_ROWS, tk), lambda b, h: (b, h, 0, 0, 0)),
                  pl.BlockSpec(gsub.shape, lambda b, h: (0, 0)),
                  pl.BlockSpec(lv.shape, lambda b, h: (0, 0))],
        out_specs=pl.BlockSpec((1, HEAD_V, S), lambda b, h: (b, h, 0)),
        out_shape=jax.ShapeDtypeStruct((B, H * HEAD_V, S), BF16),
        scratch_shapes=[pltpu.VMEM((n_rhs, KPAD, tq), BF16),
                        pltpu.VMEM((n_maps, group_size, tk, tq), BF16)],
        compiler_params=_cparams(("arbitrary", "arbitrary")),
        name=name,
    )(sc, qT, k, vT, gsub, lv)


def _attention(qT, k, vT, gq, gk, gsub, lv, *, width, n_maps, alibi, lambda_init):
    bound = math.sqrt(width) * BOUND_MARGIN * jnp.max(jnp.abs(gq)) * jnp.max(jnp.abs(gk))
    slopes2 = [np.float32(s * LOG2E) for s in _alibi_slopes()]
    static = jnp.asarray([[s, *_bf16_parts(s)] for s in slopes2], F32)
    sc = jnp.concatenate([static, jnp.full((HEADS, 1), LOG2E, F32) * bound,
                          jnp.zeros((HEADS, 3), F32)], axis=1)
    call = functools.partial(_attention_call, n_maps=n_maps, alibi=alibi, lambda_init=lambda_init)
    return lax.cond(bound <= FAST_SOFTMAX_BOUND,
                    functools.partial(call, online=False),
                    functools.partial(call, online=True),
                    sc, qT, k, vT, gsub, lv)


def _f1_kernel(t_ref, x_ref, o_ref):
    o_ref[0] = jnp.dot(t_ref[...], x_ref[0], preferred_element_type=F32).astype(BF16)


def _f2_kernel(g_ref, a_ref, cs_ref, o_ref, *, tk1, n2, ch, norm):
    xr, xi = [], []
    for j in range(tk1):
        z = jnp.concatenate([a_ref[0, 0, j], a_ref[0, 1, j]], axis=0)
        x = jnp.dot(g_ref[j], z, preferred_element_type=F32)
        xr.append(x[:n2].astype(BF16))
        xi.append(x[n2:].astype(BF16))
    y = (jnp.dot(jnp.concatenate(xr, axis=0), cs_ref[0:ch], preferred_element_type=F32)
         + jnp.dot(jnp.concatenate(xi, axis=0), cs_ref[ch:2 * ch], preferred_element_type=F32))
    for j in range(tk1):
        o_ref[0, :, j, :] = y[j * n2:(j + 1) * n2] * norm


def _fourier_tables(S):
    n1, n2 = _fourier_split(S)
    a = jnp.arange(n1, dtype=jnp.int32)
    ang1 = (2.0 * math.pi / n1) * ((a[:, None] * a[None, :]) % n1).astype(F32)
    t1 = jnp.concatenate([jnp.cos(ang1), -jnp.sin(ang1)], axis=0).astype(BF16)
    k = a[:, None, None] + n1 * jnp.arange(n2, dtype=jnp.int32)[None, :, None]
    b = jnp.arange(n2, dtype=jnp.int32)[None, None, :]
    ang = (2.0 * math.pi / S) * ((k * b) % S).astype(F32)
    cg, sg = jnp.cos(ang), jnp.sin(ang)
    g = jnp.concatenate([jnp.concatenate([cg, sg], axis=2),
                         jnp.concatenate([-sg, cg], axis=2)], axis=1).astype(BF16)
    gw = GROUP // FOURIER_GROUPS
    c = jnp.arange(GROUP, dtype=jnp.int32)
    same = (c[:, None] // gw) == (c[None, :] // gw)
    angc = (2.0 * math.pi / gw) * (((c[:, None] % gw) * (c[None, :] % gw)) % gw).astype(F32)
    cs = jnp.concatenate([jnp.where(same, jnp.cos(angc), 0.0),
                          jnp.where(same, jnp.sin(angc), 0.0)], axis=0).astype(BF16)
    return t1, g, cs


def _fourier(fu, tables):
    B, S, C = fu.shape
    n1, n2 = _fourier_split(S)
    t1, g, cs = tables
    tn = min(8192, n2 * C)
    a = pl.pallas_call(
        _f1_kernel,
        grid=(B, n2 * C // tn),
        in_specs=[pl.BlockSpec(t1.shape, lambda b, j: (0, 0)),
                  pl.BlockSpec((1, n1, tn), lambda b, j: (b, 0, j))],
        out_specs=pl.BlockSpec((1, 2 * n1, tn), lambda b, j: (b, 0, j)),
        out_shape=jax.ShapeDtypeStruct((B, 2 * n1, n2 * C), BF16),
        compiler_params=_cparams(("arbitrary", "arbitrary")),
        name="fourier_stage1",
    )(t1, fu.reshape(B, n1, n2 * C))
    tk1 = min(16, n1)
    norm = 1.0 / math.sqrt(S * (GROUP // FOURIER_GROUPS))
    y = pl.pallas_call(
        functools.partial(_f2_kernel, tk1=tk1, n2=n2, ch=C, norm=norm),
        grid=(B, n1 // tk1),
        in_specs=[pl.BlockSpec((tk1, 2 * n2, 2 * n2), lambda b, j: (j, 0, 0)),
                  pl.BlockSpec((1, 2, tk1, n2, C), lambda b, j: (b, 0, j, 0, 0)),
                  pl.BlockSpec(cs.shape, lambda b, j: (0, 0))],
        out_specs=pl.BlockSpec((1, n2, tk1, C), lambda b, j: (b, 0, j, 0)),
        out_shape=jax.ShapeDtypeStruct((B, n2, n1, C), F32),
        compiler_params=_cparams(("arbitrary", "arbitrary")),
        name="fourier_stage2",
    )(g, a.reshape(B, 2, n1, n2, C), cs)
    return y.reshape(B, S, C)


def _outproj_ffn_kernel(x_ref, m_ref, ab_ref, u_ref, up_ref, un_ref, cw_ref, yb_ref, yc_ref, yd_ref,
                        w_ref, g_ref, wgu_ref, wd_ref, o_ref, h_scr, acc_scr, *, ts, nt, nch):
    t = pl.program_id(1)
    u = u_ref[0]
    prev_row = jnp.where(t == 0, 0.0, up_ref[0, 7:8, :])
    next_row = jnp.where(t == nt - 1, 0.0, un_ref[0, 0:1, :])
    rid = lax.broadcasted_iota(jnp.int32, u.shape, 0)
    u_m1 = jnp.where(rid == 0, prev_row, pltpu.roll(u, 1, 0))
    u_p1 = jnp.where(rid == ts - 1, next_row, pltpu.roll(u, ts - 1, 0))
    conv = u_m1 * cw_ref[0:1, :] + u * cw_ref[1:2, :] + u_p1 * cw_ref[2:3, :]
    ya = (ab_ref[0] * conv).astype(BF16)
    tn_dims = (((0,), (0,)), ((), ()))
    acc = jnp.dot(ya, w_ref[0:GROUP], preferred_element_type=F32)
    acc += lax.dot_general(yb_ref[0], w_ref[GROUP:2 * GROUP], tn_dims, preferred_element_type=F32)
    acc += lax.dot_general(yc_ref[0], w_ref[2 * GROUP:3 * GROUP], tn_dims, preferred_element_type=F32)
    acc += jnp.dot(yd_ref[0].astype(BF16), w_ref[3 * GROUP:4 * GROUP], preferred_element_type=F32)
    x_mid = x_ref[0] + m_ref[0, 5:6, :] * acc
    o_ref[0] = _ffn_tile(x_mid, m_ref, g_ref, wgu_ref, wd_ref, h_scr, acc_scr, 2, nch)


def _outproj_ffn(x, m, ab, u, conv_w, ybT, ycT, yd, w_out, g, wgu, wd):
    B, S, D = x.shape
    ts, _, _ = _tiles(S)
    nt = S // ts
    r8 = ts // 8
    tok = lambda b, t: (b, t, 0)
    whole = lambda b, t: (0, 0)
    return pl.pallas_call(
        functools.partial(_outproj_ffn_kernel, ts=ts, nt=nt, nch=wd.shape[0] // FF_CHUNK),
        grid=(B, nt),
        in_specs=[pl.BlockSpec((1, ts, D), tok),
                  pl.BlockSpec((1, N_MOD, D), lambda b, t: (b, 0, 0)),
                  pl.BlockSpec((1, ts, GROUP), tok),
                  pl.BlockSpec((1, ts, GROUP), tok),
                  pl.BlockSpec((1, 8, GROUP), lambda b, t: (b, jnp.maximum(t * r8 - 1, 0), 0)),
                  pl.BlockSpec((1, 8, GROUP), lambda b, t: (b, jnp.minimum((t + 1) * r8, S // 8 - 1), 0)),
                  pl.BlockSpec(conv_w.shape, lambda b, t: (0, 0)),
                  pl.BlockSpec((1, GROUP, ts), lambda b, t: (b, 0, t)),
                  pl.BlockSpec((1, GROUP, ts), lambda b, t: (b, 0, t)),
                  pl.BlockSpec((1, ts, GROUP), tok),
                  pl.BlockSpec(w_out.shape, lambda b, t: (0, 0), pipeline_mode=pl.Buffered(1)),
                  pl.BlockSpec((1, D), lambda b, t: (0, 0)),
                  pl.BlockSpec(wgu.shape, whole, pipeline_mode=pl.Buffered(1)),
                  pl.BlockSpec(wd.shape, whole, pipeline_mode=pl.Buffered(1))],
        out_specs=pl.BlockSpec((1, ts, D), tok),
        out_shape=jax.ShapeDtypeStruct((B, S, D), F32),
        scratch_shapes=[pltpu.VMEM((ts, D), BF16), pltpu.VMEM((ts, D), F32)],
        compiler_params=_cparams(("arbitrary", "arbitrary")),
        name="outproj_ffn",
    )(x, m, ab, u, u, u, conv_w, ybT, ycT, yd, w_out, g, wgu, wd)


def _rope_tables(S):
    inv = ROPE_THETA ** (-jnp.arange(0, MLA_ROPE, 2, dtype=F32) / MLA_ROPE)
    ang = inv[:, None] * jnp.arange(S, dtype=F32)[None, :]
    return jnp.cos(ang), jnp.sin(ang)


def _alibi_slopes():
    return tuple(float(2.0 ** (-8.0 * (i + 1) / HEADS)) for i in range(HEADS))


def _prep_layer(p, l):
    col = lambda v: v.astype(F32).reshape(-1, 1)
    w_in = p['w_in'][l].astype(BF16)
    return dict(
        ffn1=(p['ffn1_w_gu'][l].astype(BF16), p['ffn1_w_down'][l].astype(BF16)),
        ffn2=(p['ffn2_w_gu'][l].astype(BF16), p['ffn2_w_down'][l].astype(BF16)),
        w_nat=jnp.concatenate([w_in[:, :NAT_LO], w_in[:, ATT_HI:]], axis=1),
        w_T=w_in[:, NAT_LO:ATT_HI].T,
        wuqT=p['mla_w_uq'][l].astype(BF16).T,
        wukvT=p['mla_w_ukv'][l].astype(BF16).T,
        gdq=col(p['diff_q_g'][l]), gdk=col(p['diff_k_g'][l]),
        gqa=col(p['mla_q_a_g'][l]), gkva=col(p['mla_kv_a_g'][l]),
        gmq=col(p['mla_q_g'][l]), gmk=col(p['mla_k_g'][l]),
        gsub=col(p['diff_subln_g'][l]),
        lv=p['diff_lambda'][l].astype(F32),
        conv_w=p['conv_w'][l].astype(F32),
        w_out=p['w_out'][l].astype(BF16),
        norm_g=p['norm_g'][l].astype(F32),
    )


def _layer(x, m, w, l, consts):
    rope_cos, rope_sin, ftables = consts
    ng = w['norm_g']
    x = _ffn(x, m, ng[0:1], *w['ffn1'], 0)
    slope_parts = tuple(_bf16_parts(np.float32(s * LOG2E)) for s in _alibi_slopes())
    ab, u, fu, qd, kd, vd, qm, km, vm = _inproj(x, m, ng[1:2], w, rope_cos, rope_sin, slope_parts)
    lambda_init = 0.8 - 0.6 * math.exp(-0.3 * l)
    ybT = _attention(qd, kd, vd, w['gdq'], w['gdk'], w['gsub'], w['lv'],
                     width=DIFF_QK, n_maps=2, alibi=True, lambda_init=lambda_init)
    ycT = _attention(qm, km, vm, w['gmq'], w['gmk'], w['gsub'], w['lv'],
                     width=MLA_QK, n_maps=1, alibi=False, lambda_init=0.0)
    yd = _fourier(fu, ftables)
    return _outproj_ffn(x, m, ab, u, w['conv_w'], ybT, ycT, yd, w['w_out'], ng[2:3], *w['ffn2'])


def _trunk(groups, c_all, p):
    depth = p['w_mod'].shape[0]
    mods = _modulation(c_all, p['w_mod'], p['b_mod'])
    consts = []
    for x, _ in groups:
        S = x.shape[1]
        consts.append(_rope_tables(S) + (_fourier_tables(S),))
    xs = [x for x, _ in groups]
    for l in range(depth):
        w = _prep_layer(p, l)
        for gi, (x0, off) in enumerate(groups):
            B, _, D = x0.shape
            m = mods[l, off:off + B].reshape(B, N_MOD, D)
            xs[gi] = _layer(xs[gi], m, w, l, consts[gi])
    return xs


def kernel(x_prompt, x_sample, c_prompt, c_sample, w_mod, b_mod, norm_g, ffn1_w_gu, ffn1_w_down,
           w_in, conv_w, diff_lambda, diff_q_g, diff_k_g, diff_subln_g, mla_q_a_g, mla_w_uq,
           mla_kv_a_g, mla_w_ukv, mla_q_g, mla_k_g, w_out, ffn2_w_gu, ffn2_w_down):
    p = dict(w_mod=w_mod, b_mod=b_mod, norm_g=norm_g, ffn1_w_gu=ffn1_w_gu, ffn1_w_down=ffn1_w_down,
             w_in=w_in, conv_w=conv_w, diff_lambda=diff_lambda, diff_q_g=diff_q_g, diff_k_g=diff_k_g,
             diff_subln_g=diff_subln_g, mla_q_a_g=mla_q_a_g, mla_w_uq=mla_w_uq, mla_kv_a_g=mla_kv_a_g,
             mla_w_ukv=mla_w_ukv, mla_q_g=mla_q_g, mla_k_g=mla_k_g, w_out=w_out,
             ffn2_w_gu=ffn2_w_gu, ffn2_w_down=ffn2_w_down)
    nb = c_prompt.shape[0] + c_sample.shape[0]
    pad = (-nb) % 8
    c_all = jnp.concatenate([c_prompt, c_sample, jnp.zeros((pad, c_prompt.shape[1]), c_prompt.dtype)])
    y_prompt, y_sample = _trunk([(x_prompt, 0), (x_sample, c_prompt.shape[0])], c_all, p)
    return (y_prompt, y_sample)
```

```python
import functools
import math

import numpy as np
import jax
import jax.numpy as jnp
from jax import lax
from jax.experimental import pallas as pl
from jax.experimental.pallas import tpu as pltpu

F32 = jnp.float32
BF16 = jnp.bfloat16

HEADS = 4
GROUP = 256
DIFF_QK = 32
MLA_NOPE = 64
MLA_ROPE = 32
MLA_QK = MLA_NOPE + MLA_ROPE
MLA_Q_RANK = 256
MLA_KV_RANK = 128
HEAD_V = 64
FOURIER_GROUPS = 4
ROPE_THETA = 10000.0
EPS = 1e-6
N_MOD = 9
ATT_ROWS = 3 * GROUP + MLA_Q_RANK + MLA_KV_RANK + MLA_ROPE
NAT_LO = 3 * GROUP
ATT_HI = NAT_LO + ATT_ROWS

BF16_SUBLANES = 16
KPAD = 128
V_ROWS = 128
AUG_ROWS = BF16_SUBLANES
FF_CHUNK = 256
FFN_TOKENS = 512
TOKEN_TILE = 512
MOD_COLS = 1536
INPROJ_BLOCKS = 2
ATTN_GROUP = 8
VMEM_LIMIT = 52 * 1024 * 1024
ATTN_VMEM_BUDGET = 40 * 1024 * 1024
NEG_BIG = -1e30
LOG2E = 1.4426950408889634
FAST_SOFTMAX_BOUND = 25.0
BOUND_MARGIN = 1.02


def _mod3(i):
    three = lambda c: jnp.where(c, 3, 0)
    return i - three(i >= 3) - three(i >= 6) - three(i >= 9)


def _bf16_parts(x):
    parts, r = [], np.float32(x)
    for _ in range(3):
        p = np.float32(np.asarray(r, dtype=jnp.bfloat16))
        parts.append(float(p))
        r = np.float32(r - p)
    return tuple(parts)


def _cparams(sem):
    return pltpu.CompilerParams(dimension_semantics=sem, vmem_limit_bytes=VMEM_LIMIT)


def _tiles(S):
    t = min(TOKEN_TILE, S)
    return t, t, t


def _fourier_split(S):
    n1 = 1 << ((int(math.log2(S)) + 1) // 2)
    return n1, S // n1


def _mod_kernel(c_ref, w_ref, b_ref, o_ref):
    c = c_ref[...]
    s = (c * jax.nn.sigmoid(c)).astype(BF16)
    o_ref[0] = jnp.dot(s, w_ref[0].astype(BF16), preferred_element_type=F32) + b_ref[0]


def _modulation(c_all, w_mod, b_mod):
    L, D, N = w_mod.shape
    bp = c_all.shape[0]
    tn = MOD_COLS
    assert N % tn == 0
    return pl.pallas_call(
        _mod_kernel,
        grid=(L, N // tn),
        in_specs=[pl.BlockSpec((bp, D), lambda l, j: (0, 0)),
                  pl.BlockSpec((1, D, tn), lambda l, j: (l, 0, j)),
                  pl.BlockSpec((1, 1, tn), lambda l, j: (l, 0, j))],
        out_specs=pl.BlockSpec((1, bp, tn), lambda l, j: (l, 0, j)),
        out_shape=jax.ShapeDtypeStruct((L, bp, N), F32),
        compiler_params=_cparams(("arbitrary", "arbitrary")),
        name="modulation",
    )(c_all, w_mod, b_mod.reshape(L, 1, N))


def _modulated_norm(x, m_ref, g_ref, sub):
    shift = m_ref[0, 3 * sub:3 * sub + 1, :]
    scale = m_ref[0, 3 * sub + 1:3 * sub + 2, :]
    ms = jnp.mean(x * x, axis=-1, keepdims=True)
    y = x * lax.rsqrt(ms + EPS) * g_ref[...]
    return y * (1.0 + scale) + shift


def _ffn_tile(x, m_ref, g_ref, wgu_ref, wd_ref, h_scr, acc_scr, sub, nch):
    h_scr[...] = _modulated_norm(x, m_ref, g_ref, sub).astype(BF16)
    acc_scr[...] = jnp.zeros_like(acc_scr)

    hidden = nch * FF_CHUNK
    for c in range(nch):
        cols = slice(c * FF_CHUNK, (c + 1) * FF_CHUNK)
        g = jnp.dot(h_scr[...], wgu_ref[:, cols], preferred_element_type=F32)
        u = jnp.dot(h_scr[...], wgu_ref[:, hidden + c * FF_CHUNK:hidden + (c + 1) * FF_CHUNK],
                    preferred_element_type=F32)
        a = (g * jax.nn.sigmoid(g) * u).astype(BF16)
        acc_scr[...] += jnp.dot(a, wd_ref[cols, :], preferred_element_type=F32)
    gate = m_ref[0, 3 * sub + 2:3 * sub + 3, :]
    return x + 0.5 * gate * acc_scr[...]


def _ffn_kernel(x_ref, m_ref, g_ref, wgu_ref, wd_ref, o_ref, h_scr, acc_scr, *, sub, nch):
    o_ref[0] = _ffn_tile(x_ref[0], m_ref, g_ref, wgu_ref, wd_ref, h_scr, acc_scr, sub, nch)


def _ffn(x, m, g, wgu, wd, sub):
    B, S, D = x.shape
    ts = min(FFN_TOKENS, S)
    nch = wd.shape[0] // FF_CHUNK
    whole = lambda b, t: (0, 0)
    return pl.pallas_call(
        functools.partial(_ffn_kernel, sub=sub, nch=nch),
        grid=(B, S // ts),
        in_specs=[pl.BlockSpec((1, ts, D), lambda b, t: (b, t, 0)),
                  pl.BlockSpec((1, N_MOD, D), lambda b, t: (b, 0, 0)),
                  pl.BlockSpec((1, D), lambda b, t: (0, 0)),
                  pl.BlockSpec(wgu.shape, whole, pipeline_mode=pl.Buffered(1)),
                  pl.BlockSpec(wd.shape, whole, pipeline_mode=pl.Buffered(1))],
        out_specs=pl.BlockSpec((1, ts, D), lambda b, t: (b, t, 0)),
        out_shape=jax.ShapeDtypeStruct((B, S, D), F32),
        scratch_shapes=[pltpu.VMEM((ts, D), BF16), pltpu.VMEM((ts, D), F32)],
        compiler_params=_cparams(("arbitrary", "arbitrary")),
        name="ffn",
    )(x, m, g, wgu, wd)


def _rms_rows(v, gain):
    inv = lax.rsqrt(jnp.mean(v * v, axis=0, keepdims=True) + EPS)
    return v * inv * gain


def _rope_rows(t1, t2, cos, sin):
    return t1 * cos - t2 * sin, t1 * sin + t2 * cos


def _ones_row_tile(tk):
    rid = lax.broadcasted_iota(jnp.int32, (V_ROWS - HEAD_V, tk), 0)
    return jnp.where(rid == 0, 1.0, 0.0).astype(BF16)


def _inproj_kernel(x_ref, m_ref, g_ref, wnat_ref, wT_ref, wuqT_ref, wukvT_ref,
                   gdq_ref, gdk_ref, gqa_ref, gkva_ref, gmq_ref, gmk_ref, cos_ref, sin_ref,
                   ab_ref, u_ref, fu_ref, qd_ref, kd_ref, vd_ref, qm_ref, km_ref, vm_ref,
                   *, ts, tk, slope_parts):
    sub = range(ts // tk)
    projected = [_inproj_project(si, x_ref, m_ref, g_ref, wnat_ref, wT_ref, ab_ref, u_ref, fu_ref, tk)
                 for si in sub]
    for si in sub:
        _inproj_operands(si, projected[si], wuqT_ref, wukvT_ref,
                         gdq_ref, gdk_ref, gqa_ref, gkva_ref, gmq_ref, gmk_ref, cos_ref, sin_ref,
                         qd_ref, kd_ref, vd_ref, qm_ref, km_ref, vm_ref,
                         ts=ts, tk=tk, slope_parts=slope_parts)


def _inproj_project(si, x_ref, m_ref, g_ref, wnat_ref, wT_ref, ab_ref, u_ref, fu_ref, tk):
    rows = slice(si * tk, (si + 1) * tk)
    hb = _modulated_norm(x_ref[0, rows], m_ref, g_ref, 1).astype(BF16)

    nat = jnp.dot(hb, wnat_ref[...], preferred_element_type=F32)
    ab_ref[0, rows] = nat[:, 0:GROUP]
    u_ref[0, rows] = nat[:, GROUP:2 * GROUP] * nat[:, 2 * GROUP:3 * GROUP]
    fu_ref[0, rows] = nat[:, 3 * GROUP:4 * GROUP].astype(BF16)

    return lax.dot_general(wT_ref[...], hb, (((1,), (1,)), ((), ())), preferred_element_type=F32)


def _inproj_operands(si, pT, wuqT_ref, wukvT_ref,
                     gdq_ref, gdk_ref, gqa_ref, gkva_ref, gmq_ref, gmk_ref, cos_ref, sin_ref,
                     qd_ref, kd_ref, vd_ref, qm_ref, km_ref, vm_ref, *, ts, tk, slope_parts):
    t = pl.program_id(1)
    rows = slice(si * tk, (si + 1) * tk)
    ones_tile = _ones_row_tile(tk)
    pos = t * ts + si * tk + lax.broadcasted_iota(jnp.int32, (1, tk), 1)
    c = pos & (tk - 1)
    c_hi = ((c >> 8) << 8).astype(F32)
    c_lo = (c & 255).astype(F32)
    rid = lax.broadcasted_iota(jnp.int32, (AUG_ROWS, tk), 0)
    pad_d = jnp.zeros((KPAD - 2 * DIFF_QK - AUG_ROWS, tk), F32)
    pad_m = jnp.zeros((KPAD - MLA_QK, tk), F32)
    gdq = gdq_ref[...] * (DIFF_QK ** -0.5 * LOG2E)
    third = _mod3(rid)
    gdk = gdk_ref[...]

    for h in range(HEADS):
        base = 2 * DIFF_QK * h
        ks = []
        for j in range(2):
            r0 = base + DIFF_QK * j
            qd_ref[0, h, DIFF_QK * j:DIFF_QK * (j + 1), rows] = _rms_rows(pT[r0:r0 + DIFF_QK], gdq).astype(BF16)
            ks.append(_rms_rows(pT[GROUP + r0:GROUP + r0 + DIFF_QK], gdk))
        s0, s1, s2 = slope_parts[h]
        part = jnp.where(third == 0, s0, jnp.where(third == 1, s1, s2))
        aug = jnp.where(rid < 6, part, jnp.where(rid < 9, c_hi, jnp.where(rid < 12, c_lo, 0.0)))
        kext = jnp.concatenate(ks + [aug, pad_d], axis=0)
        kd_ref[0, h, rows] = kext.T.astype(BF16)
        vd_ref[0, h, si, 0:HEAD_V, :] = pT[2 * GROUP + HEAD_V * h:2 * GROUP + HEAD_V * (h + 1)].astype(BF16)
        vd_ref[0, h, si, HEAD_V:V_ROWS, :] = ones_tile

    o_cq = 3 * GROUP
    o_ckv = o_cq + MLA_Q_RANK
    o_kpe = o_ckv + MLA_KV_RANK
    cqn = _rms_rows(pT[o_cq:o_ckv], gqa_ref[...]).astype(BF16)
    qmT = jnp.dot(wuqT_ref[...], cqn, preferred_element_type=F32)
    ckvn = _rms_rows(pT[o_ckv:o_kpe], gkva_ref[...]).astype(BF16)
    kvT = jnp.dot(wukvT_ref[...], ckvn, preferred_element_type=F32)
    kpe = pT[o_kpe:o_kpe + MLA_ROPE]
    cos = cos_ref[:, rows]
    sin = sin_ref[:, rows]
    gmq = gmq_ref[...] * (MLA_QK ** -0.5 * LOG2E)
    gmk = gmk_ref[...]
    half = MLA_ROPE // 2
    for h in range(HEADS):
        qn = _rms_rows(qmT[MLA_QK * h:MLA_QK * (h + 1)], gmq)
        q1, q2 = _rope_rows(qn[MLA_NOPE:MLA_NOPE + half], qn[MLA_NOPE + half:MLA_QK], cos, sin)
        qm_ref[0, h, 0:MLA_NOPE, rows] = qn[0:MLA_NOPE].astype(BF16)
        qm_ref[0, h, MLA_NOPE:MLA_NOPE + half, rows] = q1.astype(BF16)
        qm_ref[0, h, MLA_NOPE + half:MLA_QK, rows] = q2.astype(BF16)
        kv0 = (MLA_NOPE + HEAD_V) * h
        kn = _rms_rows(jnp.concatenate([kvT[kv0:kv0 + MLA_NOPE], kpe], axis=0), gmk)
        k1, k2 = _rope_rows(kn[MLA_NOPE:MLA_NOPE + half], kn[MLA_NOPE + half:MLA_QK], cos, sin)
        kext = jnp.concatenate([kn[0:MLA_NOPE], k1, k2, pad_m], axis=0)
        km_ref[0, h, rows] = kext.T.astype(BF16)
        vm_ref[0, h, si, 0:HEAD_V, :] = kvT[kv0 + MLA_NOPE:kv0 + MLA_NOPE + HEAD_V].astype(BF16)
        vm_ref[0, h, si, HEAD_V:V_ROWS, :] = ones_tile


def _inproj(x, m, g, w, rope_cos, rope_sin, slope_parts):
    B, S, D = x.shape
    _, _, tk = _tiles(S)
    ts = min(INPROJ_BLOCKS * tk, S)
    nkb = S // tk
    c2 = lambda b, t: (0, 0)
    tok = lambda b, t: (b, t, 0)
    head_T = lambda b, t: (b, 0, 0, t)
    head_N = lambda b, t: (b, 0, t, 0)
    head_V = lambda b, t: (b, 0, t, 0, 0)
    small = [w['gdq'], w['gdk'], w['gqa'], w['gkva'], w['gmq'], w['gmk']]
    in_specs = ([pl.BlockSpec((1, ts, D), tok),
                 pl.BlockSpec((1, N_MOD, D), lambda b, t: (b, 0, 0)),
                 pl.BlockSpec((1, D), c2),
                 pl.BlockSpec(w['w_nat'].shape, c2, pipeline_mode=pl.Buffered(1)),
                 pl.BlockSpec(w['w_T'].shape, c2, pipeline_mode=pl.Buffered(1)),
                 pl.BlockSpec(w['wuqT'].shape, c2, pipeline_mode=pl.Buffered(1)),
                 pl.BlockSpec(w['wukvT'].shape, c2, pipeline_mode=pl.Buffered(1))]
                + [pl.BlockSpec(a.shape, c2) for a in small]
                + [pl.BlockSpec((MLA_ROPE // 2, ts), lambda b, t: (0, t))] * 2)
    out_shape = [jax.ShapeDtypeStruct((B, S, GROUP), F32),
                 jax.ShapeDtypeStruct((B, S, GROUP), F32),
                 jax.ShapeDtypeStruct((B, S, GROUP), BF16),
                 jax.ShapeDtypeStruct((B, HEADS, 2 * DIFF_QK, S), BF16),
                 jax.ShapeDtypeStruct((B, HEADS, S, KPAD), BF16),
                 jax.ShapeDtypeStruct((B, HEADS, nkb, V_ROWS, tk), BF16),
                 jax.ShapeDtypeStruct((B, HEADS, MLA_QK, S), BF16),
                 jax.ShapeDtypeStruct((B, HEADS, S, KPAD), BF16),
                 jax.ShapeDtypeStruct((B, HEADS, nkb, V_ROWS, tk), BF16)]
    out_specs = [pl.BlockSpec((1, ts, GROUP), tok),
                 pl.BlockSpec((1, ts, GROUP), tok),
                 pl.BlockSpec((1, ts, GROUP), tok),
                 pl.BlockSpec((1, HEADS, 2 * DIFF_QK, ts), head_T),
                 pl.BlockSpec((1, HEADS, ts, KPAD), head_N),
                 pl.BlockSpec((1, HEADS, ts // tk, V_ROWS, tk), head_V),
                 pl.BlockSpec((1, HEADS, MLA_QK, ts), head_T),
                 pl.BlockSpec((1, HEADS, ts, KPAD), head_N),
                 pl.BlockSpec((1, HEADS, ts // tk, V_ROWS, tk), head_V)]
    return pl.pallas_call(
        functools.partial(_inproj_kernel, ts=ts, tk=tk, slope_parts=slope_parts),
        grid=(B, S // ts),
        in_specs=in_specs,
        out_specs=out_specs,
        out_shape=out_shape,
        compiler_params=_cparams(("arbitrary", "arbitrary")),
        name="inproj",
    )(x, m, g, w['w_nat'], w['w_T'], w['wuqT'], w['wukvT'], *small, rope_cos, rope_sin)


def _attn_query_tile(qi, sc_ref, qT_ref, k_ref, vT_ref, gsub_ref, lv_ref, o_ref, rhs_scr, p_scr,
                     *, n_maps, alibi, online, tq, tk, nkb, group_size, merge_maps, lambda_init):
    h = pl.program_id(1)
    q0 = pl.multiple_of(qi * tq, tq)
    q = qT_ref[0, 0, :, pl.ds(q0, tq)]
    shift = sc_ref[h, 4]

    def key_rows(kb, n):
        return k_ref[0, 0, pl.ds(pl.multiple_of(kb * tk, tk), n * tk), :]

    if alibi:
        slope2 = sc_ref[h, 0]
        lane = lax.broadcasted_iota(jnp.int32, (1, tq), 1)
        r_hi = ((lane >> 8) << 8).astype(F32)
        r_lo = (lane & 255).astype(F32)
        rid = lax.broadcasted_iota(jnp.int32, (AUG_ROWS, tq), 0)
        third = _mod3(rid)
        part = jnp.where(third == 0, sc_ref[h, 1], jnp.where(third == 1, sc_ref[h, 2], sc_ref[h, 3]))
        aug = jnp.where(rid < 3, -r_hi, jnp.where(rid < 6, -r_lo, jnp.where(rid < 12, part, 0.0))).astype(BF16)
        zq = jnp.zeros((DIFF_QK, tq), BF16)
        zp = jnp.zeros((KPAD - 2 * DIFF_QK - AUG_ROWS, tq), BF16)
        rhs_scr[:, 0:tq] = jnp.concatenate([q[0:DIFF_QK], zq, aug, zp], axis=0)
        rhs_scr[:, tq:2 * tq] = jnp.concatenate([zq, q[DIFF_QK:2 * DIFF_QK], aug, zp], axis=0)
        col = lax.broadcasted_iota(jnp.int32, (1, KPAD), 1)
        is_aug = (col >= 2 * DIFF_QK) & (col < 2 * DIFF_QK + AUG_ROWS)
        flip = jnp.where(is_aug, -1.0, 1.0).astype(BF16)
        keep = jnp.ones((1, KPAD), BF16)
        d_idx = lax.shift_right_logical(q0, int(math.log2(tk)))
    else:
        rhs_scr[...] = jnp.concatenate([q, jnp.zeros((KPAD - q.shape[0], tq), BF16)], axis=0)

    def accumulate(j, u, kb, s, cb, accs, ms):
        vblk = vT_ref[0, 0, kb]
        if online:
            m_new = jnp.maximum(ms[j], jnp.max(s, axis=0, keepdims=True) + cb)
            p = jnp.exp2(s + (cb - m_new)).astype(BF16)
            return (jnp.exp2(ms[j] - m_new) * accs[j] + jnp.dot(vblk, p, preferred_element_type=F32), m_new)
        p_scr[j, u] = jnp.exp2(s + (cb - shift)).astype(BF16)
        return accs[j] + jnp.dot(vblk, p_scr[j, u], preferred_element_type=F32), ms[j]

    def group(g, carry, first):
        accs, ms = carry
        accs, ms = list(accs), list(ms)
        base = g * group_size
        if alibi:
            kbs = [(d_idx + base + u) & (nkb - 1) for u in range(group_size)]
            parts = [key_rows(kb, 1) * jnp.where(kb > d_idx, flip, keep) for kb in kbs]
            if first:
                parts = [parts[0] * flip] + parts
            kgrp = jnp.concatenate(parts, axis=0)
        else:
            kbs = [base + u for u in range(group_size)]
            kgrp = key_rows(base, group_size)
        off = tk if (alibi and first) else 0
        if merge_maps:
            s_all = jnp.dot(kgrp, rhs_scr[...], preferred_element_type=F32)
        for j in range(n_maps):
            if merge_maps:
                s = s_all[:, j * tq:(j + 1) * tq]
            else:
                s = jnp.dot(kgrp, rhs_scr[:, j * tq:(j + 1) * tq], preferred_element_type=F32)
            for u, kb in enumerate(kbs):
                s_u = s[off + u * tk:off + (u + 1) * tk]
                if alibi and first and u == 0:
                    s_u = jnp.minimum(s_u, s[0:tk])
                cb = -slope2 * jnp.abs(q0 - kb * tk).astype(F32) if alibi else 0.0
                accs[j], ms[j] = accumulate(j, u, kb, s_u, cb, accs, ms)
        return tuple(accs), tuple(ms)

    carry = (tuple(jnp.zeros((V_ROWS, tq), F32) for _ in range(n_maps)),
             tuple(jnp.full((1, tq), NEG_BIG, F32) for _ in range(n_maps)))
    first_rest = 0
    if alibi:
        carry = group(0, carry, True)
        first_rest = 1
    carry = lax.fori_loop(first_rest, nkb // group_size, lambda g, c: group(g, c, False), carry)
    accs, _ = carry

    a0 = accs[0]
    o = a0[0:HEAD_V] / a0[HEAD_V:HEAD_V + 1]
    if n_maps == 2:
        a1 = accs[1]
        lv = lv_ref[...]
        lam = (jnp.exp(jnp.sum(lv[0:1] * lv[1:2], axis=1, keepdims=True))
               - jnp.exp(jnp.sum(lv[2:3] * lv[3:4], axis=1, keepdims=True)) + lambda_init)
        o = o - lam * (a1[0:HEAD_V] / a1[HEAD_V:HEAD_V + 1])
        o = _rms_rows(o, gsub_ref[...]) * (1.0 - lambda_init)
    o_ref[0, :, pl.ds(q0, tq)] = o.astype(BF16)


def _attn_kernel(*refs, nq, **kw):
    def tile(qi, carry):
        _attn_query_tile(qi, *refs, **kw)
        return carry
    lax.fori_loop(0, nq, tile, 0)


def _attention_call(sc, qT, k, vT, gsub, lv, *, n_maps, alibi, online, lambda_init):
    B, H, kq, S = qT.shape
    _, tq, tk = _tiles(S)
    nkb = S // tk
    assert tq == tk, "one key block per query tile sits on the diagonal"
    group_size = 1 if online else min(ATTN_GROUP, nkb)
    assert nkb % group_size == 0 and nkb & (nkb - 1) == 0, "rotated block order wraps with a mask"
    resident = 2 * 2 * S * (kq + KPAD + V_ROWS + HEAD_V)
    tiles = (group_size + 1) * tk * n_maps * tq * (4 + 2)
    merge_maps = n_maps > 1 and not online and resident + tiles <= ATTN_VMEM_BUDGET
    kern = functools.partial(_attn_kernel, n_maps=n_maps, alibi=alibi, online=online, tq=tq, tk=tk,
                             nkb=nkb, group_size=group_size, merge_maps=merge_maps,
                             lambda_init=lambda_init, nq=S // tq)
    n_rhs = n_maps
    name = ("attn_diff" if alibi else "attn_mla") + ("_online" if online else "")
    return pl.pallas_call(
        kern,
        grid=(B, H),
        in_specs=[pl.BlockSpec(memory_space=pltpu.SMEM),
                  pl.BlockSpec((1, 1, kq, S), lambda b, h: (b, h, 0, 0)),
                  pl.BlockSpec((1, 1, S, KPAD), lambda b, h: (b, h, 0, 0)),
                  pl.BlockSpec((1, 1, nkb, V_ROWS, tk), lambda b, h: (b, h, 0, 0, 0)),
                  pl.BlockSpec(gsub.shape, lambda b, h: (0, 0)),
                  pl.BlockSpec(lv.shape, lambda b, h: (0, 0))],
        out_specs=pl.BlockSpec((1, HEAD_V, S), lambda b, h: (b, h, 0)),
        out_shape=jax.ShapeDtypeStruct((B, H * HEAD_V, S), BF16),
        scratch_shapes=[pltpu.VMEM((KPAD, n_rhs * tq), BF16),
                        pltpu.VMEM((n_maps, group_size, tk, tq), BF16)],
        compiler_params=_cparams(("arbitrary", "arbitrary")),
        name=name,
    )(sc, qT, k, vT, gsub, lv)


def _attention(qT, k, vT, gq, gk, gsub, lv, *, width, n_maps, alibi, lambda_init):
    bound = math.sqrt(width) * BOUND_MARGIN * jnp.max(jnp.abs(gq)) * jnp.max(jnp.abs(gk))
    slopes2 = [np.float32(s * LOG2E) for s in _alibi_slopes()]
    static = jnp.asarray([[s, *_bf16_parts(s)] for s in slopes2], F32)
    sc = jnp.concatenate([static, jnp.full((HEADS, 1), LOG2E, F32) * bound,
                          jnp.zeros((HEADS, 3), F32)], axis=1)
    call = functools.partial(_attention_call, n_maps=n_maps, alibi=alibi, lambda_init=lambda_init)
    return lax.cond(bound <= FAST_SOFTMAX_BOUND,
                    functools.partial(call, online=False),
                    functools.partial(call, online=True),
                    sc, qT, k, vT, gsub, lv)


def _f1_kernel(t_ref, x_ref, o_ref):
    o_ref[0] = jnp.dot(t_ref[...], x_ref[0], preferred_element_type=F32).astype(BF16)


def _f2_kernel(g_ref, a_ref, cs_ref, o_ref, *, tk1, n2, ch, norm):
    xr, xi = [], []
    for j in range(tk1):
        z = jnp.concatenate([a_ref[0, 0, j], a_ref[0, 1, j]], axis=0)
        x = jnp.dot(g_ref[j], z, preferred_element_type=F32)
        xr.append(x[:n2].astype(BF16))
        xi.append(x[n2:].astype(BF16))
    y = (jnp.dot(jnp.concatenate(xr, axis=0), cs_ref[0:ch], preferred_element_type=F32)
         + jnp.dot(jnp.concatenate(xi, axis=0), cs_ref[ch:2 * ch], preferred_element_type=F32))
    for j in range(tk1):
        o_ref[0, :, j, :] = y[j * n2:(j + 1) * n2] * norm


def _fourier_tables(S):
    n1, n2 = _fourier_split(S)
    a = jnp.arange(n1, dtype=jnp.int32)
    ang1 = (2.0 * math.pi / n1) * ((a[:, None] * a[None, :]) % n1).astype(F32)
    t1 = jnp.concatenate([jnp.cos(ang1), -jnp.sin(ang1)], axis=0).astype(BF16)
    k = a[:, None, None] + n1 * jnp.arange(n2, dtype=jnp.int32)[None, :, None]
    b = jnp.arange(n2, dtype=jnp.int32)[None, None, :]
    ang = (2.0 * math.pi / S) * ((k * b) % S).astype(F32)
    cg, sg = jnp.cos(ang), jnp.sin(ang)
    g = jnp.concatenate([jnp.concatenate([cg, sg], axis=2),
                         jnp.concatenate([-sg, cg], axis=2)], axis=1).astype(BF16)
    gw = GROUP // FOURIER_GROUPS
    c = jnp.arange(GROUP, dtype=jnp.int32)
    same = (c[:, None] // gw) == (c[None, :] // gw)
    angc = (2.0 * math.pi / gw) * (((c[:, None] % gw) * (c[None, :] % gw)) % gw).astype(F32)
    cs = jnp.concatenate([jnp.where(same, jnp.cos(angc), 0.0),
                          jnp.where(same, jnp.sin(angc), 0.0)], axis=0).astype(BF16)
    return t1, g, cs


def _fourier(fu, tables):
    B, S, C = fu.shape
    n1, n2 = _fourier_split(S)
    t1, g, cs = tables
    tn = min(8192, n2 * C)
    a = pl.pallas_call(
        _f1_kernel,
        grid=(B, n2 * C // tn),
        in_specs=[pl.BlockSpec(t1.shape, lambda b, j: (0, 0)),
                  pl.BlockSpec((1, n1, tn), lambda b, j: (b, 0, j))],
        out_specs=pl.BlockSpec((1, 2 * n1, tn), lambda b, j: (b, 0, j)),
        out_shape=jax.ShapeDtypeStruct((B, 2 * n1, n2 * C), BF16),
        compiler_params=_cparams(("arbitrary", "arbitrary")),
        name="fourier_stage1",
    )(t1, fu.reshape(B, n1, n2 * C))
    tk1 = min(16, n1)
    norm = 1.0 / math.sqrt(S * (GROUP // FOURIER_GROUPS))
    y = pl.pallas_call(
        functools.partial(_f2_kernel, tk1=tk1, n2=n2, ch=C, norm=norm),
        grid=(B, n1 // tk1),
        in_specs=[pl.BlockSpec((tk1, 2 * n2, 2 * n2), lambda b, j: (j, 0, 0)),
                  pl.BlockSpec((1, 2, tk1, n2, C), lambda b, j: (b, 0, j, 0, 0)),
                  pl.BlockSpec(cs.shape, lambda b, j: (0, 0))],
        out_specs=pl.BlockSpec((1, n2, tk1, C), lambda b, j: (b, 0, j, 0)),
        out_shape=jax.ShapeDtypeStruct((B, n2, n1, C), F32),
        compiler_params=_cparams(("arbitrary", "arbitrary")),
        name="fourier_stage2",
    )(g, a.reshape(B, 2, n1, n2, C), cs)
    return y.reshape(B, S, C)


def _outproj_ffn_kernel(x_ref, m_ref, ab_ref, u_ref, up_ref, un_ref, cw_ref, yb_ref, yc_ref, yd_ref,
                        w_ref, g_ref, wgu_ref, wd_ref, o_ref, h_scr, acc_scr, *, ts, nt, nch):
    t = pl.program_id(1)
    u = u_ref[0]
    prev_row = jnp.where(t == 0, 0.0, up_ref[0, 7:8, :])
    next_row = jnp.where(t == nt - 1, 0.0, un_ref[0, 0:1, :])
    rid = lax.broadcasted_iota(jnp.int32, u.shape, 0)
    u_m1 = jnp.where(rid == 0, prev_row, pltpu.roll(u, 1, 0))
    u_p1 = jnp.where(rid == ts - 1, next_row, pltpu.roll(u, ts - 1, 0))
    conv = u_m1 * cw_ref[0:1, :] + u * cw_ref[1:2, :] + u_p1 * cw_ref[2:3, :]
    ya = (ab_ref[0] * conv).astype(BF16)
    tn_dims = (((0,), (0,)), ((), ()))
    acc = jnp.dot(ya, w_ref[0:GROUP], preferred_element_type=F32)
    acc += lax.dot_general(yb_ref[0], w_ref[GROUP:2 * GROUP], tn_dims, preferred_element_type=F32)
    acc += lax.dot_general(yc_ref[0], w_ref[2 * GROUP:3 * GROUP], tn_dims, preferred_element_type=F32)
    acc += jnp.dot(yd_ref[0].astype(BF16), w_ref[3 * GROUP:4 * GROUP], preferred_element_type=F32)
    x_mid = x_ref[0] + m_ref[0, 5:6, :] * acc
    o_ref[0] = _ffn_tile(x_mid, m_ref, g_ref, wgu_ref, wd_ref, h_scr, acc_scr, 2, nch)


def _outproj_ffn(x, m, ab, u, conv_w, ybT, ycT, yd, w_out, g, wgu, wd):
    B, S, D = x.shape
    ts, _, _ = _tiles(S)
    nt = S // ts
    r8 = ts // 8
    tok = lambda b, t: (b, t, 0)
    whole = lambda b, t: (0, 0)
    return pl.pallas_call(
        functools.partial(_outproj_ffn_kernel, ts=ts, nt=nt, nch=wd.shape[0] // FF_CHUNK),
        grid=(B, nt),
        in_specs=[pl.BlockSpec((1, ts, D), tok),
                  pl.BlockSpec((1, N_MOD, D), lambda b, t: (b, 0, 0)),
                  pl.BlockSpec((1, ts, GROUP), tok),
                  pl.BlockSpec((1, ts, GROUP), tok),
                  pl.BlockSpec((1, 8, GROUP), lambda b, t: (b, jnp.maximum(t * r8 - 1, 0), 0)),
                  pl.BlockSpec((1, 8, GROUP), lambda b, t: (b, jnp.minimum((t + 1) * r8, S // 8 - 1), 0)),
                  pl.BlockSpec(conv_w.shape, lambda b, t: (0, 0)),
                  pl.BlockSpec((1, GROUP, ts), lambda b, t: (b, 0, t)),
                  pl.BlockSpec((1, GROUP, ts), lambda b, t: (b, 0, t)),
                  pl.BlockSpec((1, ts, GROUP), tok),
                  pl.BlockSpec(w_out.shape, lambda b, t: (0, 0), pipeline_mode=pl.Buffered(1)),
                  pl.BlockSpec((1, D), lambda b, t: (0, 0)),
                  pl.BlockSpec(wgu.shape, whole, pipeline_mode=pl.Buffered(1)),
                  pl.BlockSpec(wd.shape, whole, pipeline_mode=pl.Buffered(1))],
        out_specs=pl.BlockSpec((1, ts, D), tok),
        out_shape=jax.ShapeDtypeStruct((B, S, D), F32),
        scratch_shapes=[pltpu.VMEM((ts, D), BF16), pltpu.VMEM((ts, D), F32)],
        compiler_params=_cparams(("arbitrary", "arbitrary")),
        name="outproj_ffn",
    )(x, m, ab, u, u, u, conv_w, ybT, ycT, yd, w_out, g, wgu, wd)


def _rope_tables(S):
    inv = ROPE_THETA ** (-jnp.arange(0, MLA_ROPE, 2, dtype=F32) / MLA_ROPE)
    ang = inv[:, None] * jnp.arange(S, dtype=F32)[None, :]
    return jnp.cos(ang), jnp.sin(ang)


def _alibi_slopes():
    return tuple(float(2.0 ** (-8.0 * (i + 1) / HEADS)) for i in range(HEADS))


def _prep_layer(p, l):
    col = lambda v: v.astype(F32).reshape(-1, 1)
    w_in = p['w_in'][l].astype(BF16)
    return dict(
        ffn1=(p['ffn1_w_gu'][l].astype(BF16), p['ffn1_w_down'][l].astype(BF16)),
        ffn2=(p['ffn2_w_gu'][l].astype(BF16), p['ffn2_w_down'][l].astype(BF16)),
        w_nat=jnp.concatenate([w_in[:, :NAT_LO], w_in[:, ATT_HI:]], axis=1),
        w_T=w_in[:, NAT_LO:ATT_HI].T,
        wuqT=p['mla_w_uq'][l].astype(BF16).T,
        wukvT=p['mla_w_ukv'][l].astype(BF16).T,
        gdq=col(p['diff_q_g'][l]), gdk=col(p['diff_k_g'][l]),
        gqa=col(p['mla_q_a_g'][l]), gkva=col(p['mla_kv_a_g'][l]),
        gmq=col(p['mla_q_g'][l]), gmk=col(p['mla_k_g'][l]),
        gsub=col(p['diff_subln_g'][l]),
        lv=p['diff_lambda'][l].astype(F32),
        conv_w=p['conv_w'][l].astype(F32),
        w_out=p['w_out'][l].astype(BF16),
        norm_g=p['norm_g'][l].astype(F32),
    )


def _layer(x, m, w, l, consts):
    rope_cos, rope_sin, ftables = consts
    ng = w['norm_g']
    x = _ffn(x, m, ng[0:1], *w['ffn1'], 0)
    slope_parts = tuple(_bf16_parts(np.float32(s * LOG2E)) for s in _alibi_slopes())
    ab, u, fu, qd, kd, vd, qm, km, vm = _inproj(x, m, ng[1:2], w, rope_cos, rope_sin, slope_parts)
    lambda_init = 0.8 - 0.6 * math.exp(-0.3 * l)
    ybT = _attention(qd, kd, vd, w['gdq'], w['gdk'], w['gsub'], w['lv'],
                     width=DIFF_QK, n_maps=2, alibi=True, lambda_init=lambda_init)
    ycT = _attention(qm, km, vm, w['gmq'], w['gmk'], w['gsub'], w['lv'],
                     width=MLA_QK, n_maps=1, alibi=False, lambda_init=0.0)
    yd = _fourier(fu, ftables)
    return _outproj_ffn(x, m, ab, u, w['conv_w'], ybT, ycT, yd, w['w_out'], ng[2:3], *w['ffn2'])


def _trunk(groups, c_all, p):
    depth = p['w_mod'].shape[0]
    mods = _modulation(c_all, p['w_mod'], p['b_mod'])
    consts = []
    for x, _ in groups:
        S = x.shape[1]
        consts.append(_rope_tables(S) + (_fourier_tables(S),))
    xs = [x for x, _ in groups]
    for l in range(depth):
        w = _prep_layer(p, l)
        for gi, (x0, off) in enumerate(groups):
            B, _, D = x0.shape
            m = mods[l, off:off + B].reshape(B, N_MOD, D)
            xs[gi] = _layer(xs[gi], m, w, l, consts[gi])
    return xs


def kernel(x_prompt, x_sample, c_prompt, c_sample, w_mod, b_mod, norm_g, ffn1_w_gu, ffn1_w_down,
           w_in, conv_w, diff_lambda, diff_q_g, diff_k_g, diff_subln_g, mla_q_a_g, mla_w_uq,
           mla_kv_a_g, mla_w_ukv, mla_q_g, mla_k_g, w_out, ffn2_w_gu, ffn2_w_down):
    p = dict(w_mod=w_mod, b_mod=b_mod, norm_g=norm_g, ffn1_w_gu=ffn1_w_gu, ffn1_w_down=ffn1_w_down,
             w_in=w_in, conv_w=conv_w, diff_lambda=diff_lambda, diff_q_g=diff_q_g, diff_k_g=diff_k_g,
             diff_subln_g=diff_subln_g, mla_q_a_g=mla_q_a_g, mla_w_uq=mla_w_uq, mla_kv_a_g=mla_kv_a_g,
             mla_w_ukv=mla_w_ukv, mla_q_g=mla_q_g, mla_k_g=mla_k_g, w_out=w_out,
             ffn2_w_gu=ffn2_w_gu, ffn2_w_down=ffn2_w_down)
    nb = c_prompt.shape[0] + c_sample.shape[0]
    pad = (-nb) % 8
    c_all = jnp.concatenate([c_prompt, c_sample, jnp.zeros((pad, c_prompt.shape[1]), c_prompt.dtype)])
    y_prompt, y_sample = _trunk([(x_prompt, 0), (x_sample, c_prompt.shape[0])], c_all, p)
    return (y_prompt, y_sample)
```

```python
import functools
import math

import numpy as np
import jax
import jax.numpy as jnp
from jax import lax
from jax.experimental import pallas as pl
from jax.experimental.pallas import tpu as pltpu

F32 = jnp.float32
BF16 = jnp.bfloat16

HEADS = 4
GROUP = 256
DIFF_QK = 32
MLA_NOPE = 64
MLA_ROPE = 32
MLA_QK = MLA_NOPE + MLA_ROPE
MLA_Q_RANK = 256
MLA_KV_RANK = 128
HEAD_V = 64
FOURIER_GROUPS = 4
ROPE_THETA = 10000.0
EPS = 1e-6
N_MOD = 9
ATT_ROWS = 3 * GROUP + MLA_Q_RANK + MLA_KV_RANK + MLA_ROPE
NAT_LO = 3 * GROUP
ATT_HI = NAT_LO + ATT_ROWS

BF16_SUBLANES = 16
KPAD = 128
V_ROWS = 128
AUG_ROWS = BF16_SUBLANES
FF_CHUNK = 256
FFN_TOKENS = 512
TOKEN_TILE = 512
MOD_COLS = 1536
INPROJ_BLOCKS = 2
ATTN_GROUP = 8
VMEM_LIMIT = 52 * 1024 * 1024
ATTN_VMEM_BUDGET = 40 * 1024 * 1024
NEG_BIG = -1e30
LOG2E = 1.4426950408889634
FAST_SOFTMAX_BOUND = 25.0
BOUND_MARGIN = 1.02


def _mod3(i):
    three = lambda c: jnp.where(c, 3, 0)
    return i - three(i >= 3) - three(i >= 6) - three(i >= 9)


def _bf16_parts(x):
    parts, r = [], np.float32(x)
    for _ in range(3):
        p = np.float32(np.asarray(r, dtype=jnp.bfloat16))
        parts.append(float(p))
        r = np.float32(r - p)
    return tuple(parts)


def _cparams(sem):
    return pltpu.CompilerParams(dimension_semantics=sem, vmem_limit_bytes=VMEM_LIMIT)


def _tiles(S):
    t = min(TOKEN_TILE, S)
    return t, t, t


def _fourier_split(S):
    n1 = 1 << ((int(math.log2(S)) + 1) // 2)
    return n1, S // n1


def _mod_kernel(c_ref, w_ref, b_ref, o_ref):
    c = c_ref[...]
    s = (c * jax.nn.sigmoid(c)).astype(BF16)
    o_ref[0] = jnp.dot(s, w_ref[0].astype(BF16), preferred_element_type=F32) + b_ref[0]


def _modulation(c_all, w_mod, b_mod):
    L, D, N = w_mod.shape
    bp = c_all.shape[0]
    tn = MOD_COLS
    assert N % tn == 0
    return pl.pallas_call(
        _mod_kernel,
        grid=(L, N // tn),
        in_specs=[pl.BlockSpec((bp, D), lambda l, j: (0, 0)),
                  pl.BlockSpec((1, D, tn), lambda l, j: (l, 0, j)),
                  pl.BlockSpec((1, 1, tn), lambda l, j: (l, 0, j))],
        out_specs=pl.BlockSpec((1, bp, tn), lambda l, j: (l, 0, j)),
        out_shape=jax.ShapeDtypeStruct((L, bp, N), F32),
        compiler_params=_cparams(("arbitrary", "arbitrary")),
        name="modulation",
    )(c_all, w_mod, b_mod.reshape(L, 1, N))


def _modulated_norm(x, m_ref, g_ref, sub):
    shift = m_ref[0, 3 * sub:3 * sub + 1, :]
    scale = m_ref[0, 3 * sub + 1:3 * sub + 2, :]
    ms = jnp.mean(x * x, axis=-1, keepdims=True)
    y = x * lax.rsqrt(ms + EPS) * g_ref[...]
    return y * (1.0 + scale) + shift


def _ffn_tile(x, m_ref, g_ref, wgu_ref, wd_ref, h_scr, acc_scr, sub, nch):
    h_scr[...] = _modulated_norm(x, m_ref, g_ref, sub).astype(BF16)
    acc_scr[...] = jnp.zeros_like(acc_scr)

    hidden = nch * FF_CHUNK
    for c in range(nch):
        cols = slice(c * FF_CHUNK, (c + 1) * FF_CHUNK)
        g = jnp.dot(h_scr[...], wgu_ref[:, cols], preferred_element_type=F32)
        u = jnp.dot(h_scr[...], wgu_ref[:, hidden + c * FF_CHUNK:hidden + (c + 1) * FF_CHUNK],
                    preferred_element_type=F32)
        a = (g * jax.nn.sigmoid(g) * u).astype(BF16)
        acc_scr[...] += jnp.dot(a, wd_ref[cols, :], preferred_element_type=F32)
    gate = m_ref[0, 3 * sub + 2:3 * sub + 3, :]
    return x + 0.5 * gate * acc_scr[...]


def _ffn_kernel(x_ref, m_ref, g_ref, wgu_ref, wd_ref, o_ref, h_scr, acc_scr, *, sub, nch):
    o_ref[0] = _ffn_tile(x_ref[0], m_ref, g_ref, wgu_ref, wd_ref, h_scr, acc_scr, sub, nch)


def _ffn(x, m, g, wgu, wd, sub):
    B, S, D = x.shape
    ts = min(FFN_TOKENS, S)
    nch = wd.shape[0] // FF_CHUNK
    whole = lambda b, t: (0, 0)
    return pl.pallas_call(
        functools.partial(_ffn_kernel, sub=sub, nch=nch),
        grid=(B, S // ts),
        in_specs=[pl.BlockSpec((1, ts, D), lambda b, t: (b, t, 0)),
                  pl.BlockSpec((1, N_MOD, D), lambda b, t: (b, 0, 0)),
                  pl.BlockSpec((1, D), lambda b, t: (0, 0)),
                  pl.BlockSpec(wgu.shape, whole, pipeline_mode=pl.Buffered(1)),
                  pl.BlockSpec(wd.shape, whole, pipeline_mode=pl.Buffered(1))],
        out_specs=pl.BlockSpec((1, ts, D), lambda b, t: (b, t, 0)),
        out_shape=jax.ShapeDtypeStruct((B, S, D), F32),
        scratch_shapes=[pltpu.VMEM((ts, D), BF16), pltpu.VMEM((ts, D), F32)],
        compiler_params=_cparams(("arbitrary", "arbitrary")),
        name="ffn",
    )(x, m, g, wgu, wd)


def _rms_rows(v, gain):
    inv = lax.rsqrt(jnp.mean(v * v, axis=0, keepdims=True) + EPS)
    return v * inv * gain


def _rope_rows(t1, t2, cos, sin):
    return t1 * cos - t2 * sin, t1 * sin + t2 * cos


def _ones_row_tile(tk):
    rid = lax.broadcasted_iota(jnp.int32, (V_ROWS - HEAD_V, tk), 0)
    return jnp.where(rid == 0, 1.0, 0.0).astype(BF16)


def _inproj_kernel(x_ref, m_ref, g_ref, wnat_ref, wT_ref, wuqT_ref, wukvT_ref,
                   gdq_ref, gdk_ref, gqa_ref, gkva_ref, gmq_ref, gmk_ref, cos_ref, sin_ref,
                   ab_ref, u_ref, fu_ref, qd_ref, kd_ref, vd_ref, qm_ref, km_ref, vm_ref,
                   *, ts, tk, slope_parts):
    sub = range(ts // tk)
    projected = [_inproj_project(si, x_ref, m_ref, g_ref, wnat_ref, wT_ref, ab_ref, u_ref, fu_ref, tk)
                 for si in sub]
    for si in sub:
        _inproj_operands(si, projected[si], wuqT_ref, wukvT_ref,
                         gdq_ref, gdk_ref, gqa_ref, gkva_ref, gmq_ref, gmk_ref, cos_ref, sin_ref,
                         qd_ref, kd_ref, vd_ref, qm_ref, km_ref, vm_ref,
                         ts=ts, tk=tk, slope_parts=slope_parts)


def _inproj_project(si, x_ref, m_ref, g_ref, wnat_ref, wT_ref, ab_ref, u_ref, fu_ref, tk):
    rows = slice(si * tk, (si + 1) * tk)
    hb = _modulated_norm(x_ref[0, rows], m_ref, g_ref, 1).astype(BF16)

    nat = jnp.dot(hb, wnat_ref[...], preferred_element_type=F32)
    ab_ref[0, rows] = nat[:, 0:GROUP]
    u_ref[0, rows] = nat[:, GROUP:2 * GROUP] * nat[:, 2 * GROUP:3 * GROUP]
    fu_ref[0, rows] = nat[:, 3 * GROUP:4 * GROUP].astype(BF16)

    return lax.dot_general(wT_ref[...], hb, (((1,), (1,)), ((), ())), preferred_element_type=F32)


def _inproj_operands(si, pT, wuqT_ref, wukvT_ref,
                     gdq_ref, gdk_ref, gqa_ref, gkva_ref, gmq_ref, gmk_ref, cos_ref, sin_ref,
                     qd_ref, kd_ref, vd_ref, qm_ref, km_ref, vm_ref, *, ts, tk, slope_parts):
    t = pl.program_id(1)
    rows = slice(si * tk, (si + 1) * tk)
    ones_tile = _ones_row_tile(tk)
    pos = t * ts + si * tk + lax.broadcasted_iota(jnp.int32, (1, tk), 1)
    c = pos & (tk - 1)
    c_hi = ((c >> 8) << 8).astype(F32)
    c_lo = (c & 255).astype(F32)
    rid = lax.broadcasted_iota(jnp.int32, (AUG_ROWS, tk), 0)
    pad_d = jnp.zeros((KPAD - 2 * DIFF_QK - AUG_ROWS, tk), F32)
    pad_m = jnp.zeros((KPAD - MLA_QK, tk), F32)
    gdq = gdq_ref[...] * (DIFF_QK ** -0.5 * LOG2E)
    third = _mod3(rid)
    gdk = gdk_ref[...]

    for h in range(HEADS):
        base = 2 * DIFF_QK * h
        ks = []
        for j in range(2):
            r0 = base + DIFF_QK * j
            qd_ref[0, h, DIFF_QK * j:DIFF_QK * (j + 1), rows] = _rms_rows(pT[r0:r0 + DIFF_QK], gdq).astype(BF16)
            ks.append(_rms_rows(pT[GROUP + r0:GROUP + r0 + DIFF_QK], gdk))
        s0, s1, s2 = slope_parts[h]
        part = jnp.where(third == 0, s0, jnp.where(third == 1, s1, s2))
        aug = jnp.where(rid < 6, part, jnp.where(rid < 9, c_hi, jnp.where(rid < 12, c_lo, 0.0)))
        kext = jnp.concatenate(ks + [aug, pad_d], axis=0)
        kd_ref[0, h, rows] = kext.T.astype(BF16)
        vd_ref[0, h, si, 0:HEAD_V, :] = pT[2 * GROUP + HEAD_V * h:2 * GROUP + HEAD_V * (h + 1)].astype(BF16)
        vd_ref[0, h, si, HEAD_V:V_ROWS, :] = ones_tile

    o_cq = 3 * GROUP
    o_ckv = o_cq + MLA_Q_RANK
    o_kpe = o_ckv + MLA_KV_RANK
    cqn = _rms_rows(pT[o_cq:o_ckv], gqa_ref[...]).astype(BF16)
    qmT = jnp.dot(wuqT_ref[...], cqn, preferred_element_type=F32)
    ckvn = _rms_rows(pT[o_ckv:o_kpe], gkva_ref[...]).astype(BF16)
    kvT = jnp.dot(wukvT_ref[...], ckvn, preferred_element_type=F32)
    kpe = pT[o_kpe:o_kpe + MLA_ROPE]
    cos = cos_ref[:, rows]
    sin = sin_ref[:, rows]
    gmq = gmq_ref[...] * (MLA_QK ** -0.5 * LOG2E)
    gmk = gmk_ref[...]
    half = MLA_ROPE // 2
    for h in range(HEADS):
        qn = _rms_rows(qmT[MLA_QK * h:MLA_QK * (h + 1)], gmq)
        q1, q2 = _rope_rows(qn[MLA_NOPE:MLA_NOPE + half], qn[MLA_NOPE + half:MLA_QK], cos, sin)
        qm_ref[0, h, 0:MLA_NOPE, rows] = qn[0:MLA_NOPE].astype(BF16)
        qm_ref[0, h, MLA_NOPE:MLA_NOPE + half, rows] = q1.astype(BF16)
        qm_ref[0, h, MLA_NOPE + half:MLA_QK, rows] = q2.astype(BF16)
        kv0 = (MLA_NOPE + HEAD_V) * h
        kn = _rms_rows(jnp.concatenate([kvT[kv0:kv0 + MLA_NOPE], kpe], axis=0), gmk)
        k1, k2 = _rope_rows(kn[MLA_NOPE:MLA_NOPE + half], kn[MLA_NOPE + half:MLA_QK], cos, sin)
        kext = jnp.concatenate([kn[0:MLA_NOPE], k1, k2, pad_m], axis=0)
        km_ref[0, h, rows] = kext.T.astype(BF16)
        vm_ref[0, h, si, 0:HEAD_V, :] = kvT[kv0 + MLA_NOPE:kv0 + MLA_NOPE + HEAD_V].astype(BF16)
        vm_ref[0, h, si, HEAD_V:V_ROWS, :] = ones_tile


def _inproj(x, m, g, w, rope_cos, rope_sin, slope_parts):
    B, S, D = x.shape
    _, _, tk = _tiles(S)
    ts = min(INPROJ_BLOCKS * tk, S)
    nkb = S // tk
    c2 = lambda b, t: (0, 0)
    tok = lambda b, t: (b, t, 0)
    head_T = lambda b, t: (b, 0, 0, t)
    head_N = lambda b, t: (b, 0, t, 0)
    head_V = lambda b, t: (b, 0, t, 0, 0)
    small = [w['gdq'], w['gdk'], w['gqa'], w['gkva'], w['gmq'], w['gmk']]
    in_specs = ([pl.BlockSpec((1, ts, D), tok),
                 pl.BlockSpec((1, N_MOD, D), lambda b, t: (b, 0, 0)),
                 pl.BlockSpec((1, D), c2),
                 pl.BlockSpec(w['w_nat'].shape, c2, pipeline_mode=pl.Buffered(1)),
                 pl.BlockSpec(w['w_T'].shape, c2, pipeline_mode=pl.Buffered(1)),
                 pl.BlockSpec(w['wuqT'].shape, c2, pipeline_mode=pl.Buffered(1)),
                 pl.BlockSpec(w['wukvT'].shape, c2, pipeline_mode=pl.Buffered(1))]
                + [pl.BlockSpec(a.shape, c2) for a in small]
                + [pl.BlockSpec((MLA_ROPE // 2, ts), lambda b, t: (0, t))] * 2)
    out_shape = [jax.ShapeDtypeStruct((B, S, GROUP), F32),
                 jax.ShapeDtypeStruct((B, S, GROUP), F32),
                 jax.ShapeDtypeStruct((B, S, GROUP), BF16),
                 jax.ShapeDtypeStruct((B, HEADS, 2 * DIFF_QK, S), BF16),
                 jax.ShapeDtypeStruct((B, HEADS, S, KPAD), BF16),
                 jax.ShapeDtypeStruct((B, HEADS, nkb, V_ROWS, tk), BF16),
                 jax.ShapeDtypeStruct((B, HEADS, MLA_QK, S), BF16),
                 jax.ShapeDtypeStruct((B, HEADS, S, KPAD), BF16),
                 jax.ShapeDtypeStruct((B, HEADS, nkb, V_ROWS, tk), BF16)]
    out_specs = [pl.BlockSpec((1, ts, GROUP), tok),
                 pl.BlockSpec((1, ts, GROUP), tok),
                 pl.BlockSpec((1, ts, GROUP), tok),
                 pl.BlockSpec((1, HEADS, 2 * DIFF_QK, ts), head_T),
                 pl.BlockSpec((1, HEADS, ts, KPAD), head_N),
                 pl.BlockSpec((1, HEADS, ts // tk, V_ROWS, tk), head_V),
                 pl.BlockSpec((1, HEADS, MLA_QK, ts), head_T),
                 pl.BlockSpec((1, HEADS, ts, KPAD), head_N),
                 pl.BlockSpec((1, HEADS, ts // tk, V_ROWS, tk), head_V)]
    return pl.pallas_call(
        functools.partial(_inproj_kernel, ts=ts, tk=tk, slope_parts=slope_parts),
        grid=(B, S // ts),
        in_specs=in_specs,
        out_specs=out_specs,
        out_shape=out_shape,
        compiler_params=_cparams(("arbitrary", "arbitrary")),
        name="inproj",
    )(x, m, g, w['w_nat'], w['w_T'], w['wuqT'], w['wukvT'], *small, rope_cos, rope_sin)


def _attn_query_tile(qi, sc_ref, qT_ref, k_ref, vT_ref, gsub_ref, lv_ref, o_ref, rhs_scr, p_scr,
                     *, n_maps, alibi, online, tq, tk, nkb, group_size, merge_maps, lambda_init):
    h = pl.program_id(1)
    q0 = pl.multiple_of(qi * tq, tq)
    q = qT_ref[0, 0, :, pl.ds(q0, tq)]
    shift = sc_ref[h, 4]

    def key_rows(kb, n):
        return k_ref[0, 0, pl.ds(pl.multiple_of(kb * tk, tk), n * tk), :]

    if alibi:
        slope2 = sc_ref[h, 0]
        lane = lax.broadcasted_iota(jnp.int32, (1, tq), 1)
        r_hi = ((lane >> 8) << 8).astype(F32)
        r_lo = (lane & 255).astype(F32)
        rid = lax.broadcasted_iota(jnp.int32, (AUG_ROWS, tq), 0)
        third = _mod3(rid)
        part = jnp.where(third == 0, sc_ref[h, 1], jnp.where(third == 1, sc_ref[h, 2], sc_ref[h, 3]))
        aug = jnp.where(rid < 3, -r_hi, jnp.where(rid < 6, -r_lo, jnp.where(rid < 12, part, 0.0))).astype(BF16)
        zq = jnp.zeros((DIFF_QK, tq), BF16)
        zp = jnp.zeros((KPAD - 2 * DIFF_QK - AUG_ROWS, tq), BF16)
        rhs_scr[:, 0:tq] = jnp.concatenate([q[0:DIFF_QK], zq, aug, zp], axis=0)
        rhs_scr[:, tq:2 * tq] = jnp.concatenate([zq, q[DIFF_QK:2 * DIFF_QK], aug, zp], axis=0)
        col = lax.broadcasted_iota(jnp.int32, (1, KPAD), 1)
        is_aug = (col >= 2 * DIFF_QK) & (col < 2 * DIFF_QK + AUG_ROWS)
        flip = jnp.where(is_aug, -1.0, 1.0).astype(BF16)
        keep = jnp.ones((1, KPAD), BF16)
        d_idx = lax.shift_right_logical(q0, int(math.log2(tk)))
    else:
        rhs_scr[...] = jnp.concatenate([q, jnp.zeros((KPAD - q.shape[0], tq), BF16)], axis=0)

    def accumulate(j, u, kb, s, cb, accs, ms):
        vblk = vT_ref[0, 0, kb]
        if online:
            m_new = jnp.maximum(ms[j], jnp.max(s, axis=0, keepdims=True) + cb)
            p = jnp.exp2(s + (cb - m_new)).astype(BF16)
            return (jnp.exp2(ms[j] - m_new) * accs[j] + jnp.dot(vblk, p, preferred_element_type=F32), m_new)
        p_scr[j, u] = jnp.exp2(s + (cb - shift)).astype(BF16)
        return accs[j] + jnp.dot(vblk, p_scr[j, u], preferred_element_type=F32), ms[j]

    def group(g, carry, first):
        accs, ms = carry
        accs, ms = list(accs), list(ms)
        base = g * group_size
        if alibi:
            kbs = [(d_idx + base + u) & (nkb - 1) for u in range(group_size)]
            parts = [key_rows(kb, 1) * jnp.where(kb > d_idx, flip, keep) for kb in kbs]
            if first:
                parts = [parts[0] * flip] + parts
            kgrp = jnp.concatenate(parts, axis=0)
        else:
            kbs = [base + u for u in range(group_size)]
            kgrp = key_rows(base, group_size)
        off = tk if (alibi and first) else 0
        if merge_maps:
            s_all = jnp.dot(kgrp, rhs_scr[...], preferred_element_type=F32)
        for j in range(n_maps):
            if merge_maps:
                s = s_all[:, j * tq:(j + 1) * tq]
            else:
                s = jnp.dot(kgrp, rhs_scr[:, j * tq:(j + 1) * tq], preferred_element_type=F32)
            for u, kb in enumerate(kbs):
                s_u = s[off + u * tk:off + (u + 1) * tk]
                if alibi and first and u == 0:
                    s_u = jnp.minimum(s_u, s[0:tk])
                cb = -slope2 * jnp.abs(q0 - kb * tk).astype(F32) if alibi else 0.0
                accs[j], ms[j] = accumulate(j, u, kb, s_u, cb, accs, ms)
        return tuple(accs), tuple(ms)

    carry = (tuple(jnp.zeros((V_ROWS, tq), F32) for _ in range(n_maps)),
             tuple(jnp.full((1, tq), NEG_BIG, F32) for _ in range(n_maps)))
    first_rest = 0
    if alibi:
        carry = group(0, carry, True)
        first_rest = 1
    carry = lax.fori_loop(first_rest, nkb // group_size, lambda g, c: group(g, c, False), carry)
    accs, _ = carry

    a0 = accs[0]
    o = a0[0:HEAD_V] / a0[HEAD_V:HEAD_V + 1]
    if n_maps == 2:
        a1 = accs[1]
        lv = lv_ref[...]
        lam = (jnp.exp(jnp.sum(lv[0:1] * lv[1:2], axis=1, keepdims=True))
               - jnp.exp(jnp.sum(lv[2:3] * lv[3:4], axis=1, keepdims=True)) + lambda_init)
        o = o - lam * (a1[0:HEAD_V] / a1[HEAD_V:HEAD_V + 1])
        o = _rms_rows(o, gsub_ref[...]) * (1.0 - lambda_init)
    o_ref[0, :, pl.ds(q0, tq)] = o.astype(BF16)


def _attn_kernel(*refs, nq, **kw):
    def tile(qi, carry):
        _attn_query_tile(qi, *refs, **kw)
        return carry
    lax.fori_loop(0, nq, tile, 0)


def _attention_call(sc, qT, k, vT, gsub, lv, *, n_maps, alibi, online, lambda_init):
    B, H, kq, S = qT.shape
    _, tq, tk = _tiles(S)
    nkb = S // tk
    group_size = 1 if online else min(ATTN_GROUP, nkb)
    assert nkb % group_size == 0 and nkb & (nkb - 1) == 0, "rotated block order wraps with a mask"
    resident = 2 * 2 * S * (kq + KPAD + V_ROWS + HEAD_V)
    tiles = (group_size + 1) * tk * 2 * tq * (4 + 2)
    widen = not online and resident + tiles <= ATTN_VMEM_BUDGET
    merge_maps = widen and n_maps > 1
    if widen and not alibi and 2 * tq <= S:
        tq = 2 * tq
    assert not alibi or tq == tk, "one key block per query tile sits on the diagonal"
    kern = functools.partial(_attn_kernel, n_maps=n_maps, alibi=alibi, online=online, tq=tq, tk=tk,
                             nkb=nkb, group_size=group_size, merge_maps=merge_maps,
                             lambda_init=lambda_init, nq=S // tq)
    n_rhs = n_maps
    name = ("attn_diff" if alibi else "attn_mla") + ("_online" if online else "")
    return pl.pallas_call(
        kern,
        grid=(B, H),
        in_specs=[pl.BlockSpec(memory_space=pltpu.SMEM),
                  pl.BlockSpec((1, 1, kq, S), lambda b, h: (b, h, 0, 0)),
                  pl.BlockSpec((1, 1, S, KPAD), lambda b, h: (b, h, 0, 0)),
                  pl.BlockSpec((1, 1, nkb, V_ROWS, tk), lambda b, h: (b, h, 0, 0, 0)),
                  pl.BlockSpec(gsub.shape, lambda b, h: (0, 0)),
                  pl.BlockSpec(lv.shape, lambda b, h: (0, 0))],
        out_specs=pl.BlockSpec((1, HEAD_V, S), lambda b, h: (b, h, 0)),
        out_shape=jax.ShapeDtypeStruct((B, H * HEAD_V, S), BF16),
        scratch_shapes=[pltpu.VMEM((KPAD, n_rhs * tq), BF16),
                        pltpu.VMEM((n_maps, group_size, tk, tq), BF16)],
        compiler_params=_cparams(("arbitrary", "arbitrary")),
        name=name,
    )(sc, qT, k, vT, gsub, lv)


def _attention(qT, k, vT, gq, gk, gsub, lv, *, width, n_maps, alibi, lambda_init):
    bound = math.sqrt(width) * BOUND_MARGIN * jnp.max(jnp.abs(gq)) * jnp.max(jnp.abs(gk))
    slopes2 = [np.float32(s * LOG2E) for s in _alibi_slopes()]
    static = jnp.asarray([[s, *_bf16_parts(s)] for s in slopes2], F32)
    sc = jnp.concatenate([static, jnp.full((HEADS, 1), LOG2E, F32) * bound,
                          jnp.zeros((HEADS, 3), F32)], axis=1)
    call = functools.partial(_attention_call, n_maps=n_maps, alibi=alibi, lambda_init=lambda_init)
    return lax.cond(bound <= FAST_SOFTMAX_BOUND,
                    functools.partial(call, online=False),
                    functools.partial(call, online=True),
                    sc, qT, k, vT, gsub, lv)


def _f1_kernel(t_ref, x_ref, o_ref):
    o_ref[0] = jnp.dot(t_ref[...], x_ref[0], preferred_element_type=F32).astype(BF16)


def _f2_kernel(g_ref, a_ref, cs_ref, o_ref, *, tk1, n2, ch, norm):
    xr, xi = [], []
    for j in range(tk1):
        z = jnp.concatenate([a_ref[0, 0, j], a_ref[0, 1, j]], axis=0)
        x = jnp.dot(g_ref[j], z, preferred_element_type=F32)
        xr.append(x[:n2].astype(BF16))
        xi.append(x[n2:].astype(BF16))
    y = (jnp.dot(jnp.concatenate(xr, axis=0), cs_ref[0:ch], preferred_element_type=F32)
         + jnp.dot(jnp.concatenate(xi, axis=0), cs_ref[ch:2 * ch], preferred_element_type=F32))
    for j in range(tk1):
        o_ref[0, :, j, :] = y[j * n2:(j + 1) * n2] * norm


def _fourier_tables(S):
    n1, n2 = _fourier_split(S)
    a = jnp.arange(n1, dtype=jnp.int32)
    ang1 = (2.0 * math.pi / n1) * ((a[:, None] * a[None, :]) % n1).astype(F32)
    t1 = jnp.concatenate([jnp.cos(ang1), -jnp.sin(ang1)], axis=0).astype(BF16)
    k = a[:, None, None] + n1 * jnp.arange(n2, dtype=jnp.int32)[None, :, None]
    b = jnp.arange(n2, dtype=jnp.int32)[None, None, :]
    ang = (2.0 * math.pi / S) * ((k * b) % S).astype(F32)
    cg, sg = jnp.cos(ang), jnp.sin(ang)
    g = jnp.concatenate([jnp.concatenate([cg, sg], axis=2),
                         jnp.concatenate([-sg, cg], axis=2)], axis=1).astype(BF16)
    gw = GROUP // FOURIER_GROUPS
    c = jnp.arange(GROUP, dtype=jnp.int32)
    same = (c[:, None] // gw) == (c[None, :] // gw)
    angc = (2.0 * math.pi / gw) * (((c[:, None] % gw) * (c[None, :] % gw)) % gw).astype(F32)
    cs = jnp.concatenate([jnp.where(same, jnp.cos(angc), 0.0),
                          jnp.where(same, jnp.sin(angc), 0.0)], axis=0).astype(BF16)
    return t1, g, cs


def _fourier(fu, tables):
    B, S, C = fu.shape
    n1, n2 = _fourier_split(S)
    t1, g, cs = tables
    tn = min(8192, n2 * C)
    a = pl.pallas_call(
        _f1_kernel,
        grid=(B, n2 * C // tn),
        in_specs=[pl.BlockSpec(t1.shape, lambda b, j: (0, 0)),
                  pl.BlockSpec((1, n1, tn), lambda b, j: (b, 0, j))],
        out_specs=pl.BlockSpec((1, 2 * n1, tn), lambda b, j: (b, 0, j)),
        out_shape=jax.ShapeDtypeStruct((B, 2 * n1, n2 * C), BF16),
        compiler_params=_cparams(("arbitrary", "arbitrary")),
        name="fourier_stage1",
    )(t1, fu.reshape(B, n1, n2 * C))
    tk1 = min(16, n1)
    norm = 1.0 / math.sqrt(S * (GROUP // FOURIER_GROUPS))
    y = pl.pallas_call(
        functools.partial(_f2_kernel, tk1=tk1, n2=n2, ch=C, norm=norm),
        grid=(B, n1 // tk1),
        in_specs=[pl.BlockSpec((tk1, 2 * n2, 2 * n2), lambda b, j: (j, 0, 0)),
                  pl.BlockSpec((1, 2, tk1, n2, C), lambda b, j: (b, 0, j, 0, 0)),
                  pl.BlockSpec(cs.shape, lambda b, j: (0, 0))],
        out_specs=pl.BlockSpec((1, n2, tk1, C), lambda b, j: (b, 0, j, 0)),
        out_shape=jax.ShapeDtypeStruct((B, n2, n1, C), F32),
        compiler_params=_cparams(("arbitrary", "arbitrary")),
        name="fourier_stage2",
    )(g, a.reshape(B, 2, n1, n2, C), cs)
    return y.reshape(B, S, C)


def _outproj_ffn_kernel(x_ref, m_ref, ab_ref, u_ref, up_ref, un_ref, cw_ref, yb_ref, yc_ref, yd_ref,
                        w_ref, g_ref, wgu_ref, wd_ref, o_ref, h_scr, acc_scr, *, ts, nt, nch):
    t = pl.program_id(1)
    u = u_ref[0]
    prev_row = jnp.where(t == 0, 0.0, up_ref[0, 7:8, :])
    next_row = jnp.where(t == nt - 1, 0.0, un_ref[0, 0:1, :])
    rid = lax.broadcasted_iota(jnp.int32, u.shape, 0)
    u_m1 = jnp.where(rid == 0, prev_row, pltpu.roll(u, 1, 0))
    u_p1 = jnp.where(rid == ts - 1, next_row, pltpu.roll(u, ts - 1, 0))
    conv = u_m1 * cw_ref[0:1, :] + u * cw_ref[1:2, :] + u_p1 * cw_ref[2:3, :]
    ya = (ab_ref[0] * conv).astype(BF16)
    tn_dims = (((0,), (0,)), ((), ()))
    acc = jnp.dot(ya, w_ref[0:GROUP], preferred_element_type=F32)
    acc += lax.dot_general(yb_ref[0], w_ref[GROUP:2 * GROUP], tn_dims, preferred_element_type=F32)
    acc += lax.dot_general(yc_ref[0], w_ref[2 * GROUP:3 * GROUP], tn_dims, preferred_element_type=F32)
    acc += jnp.dot(yd_ref[0].astype(BF16), w_ref[3 * GROUP:4 * GROUP], preferred_element_type=F32)
    x_mid = x_ref[0] + m_ref[0, 5:6, :] * acc
    o_ref[0] = _ffn_tile(x_mid, m_ref, g_ref, wgu_ref, wd_ref, h_scr, acc_scr, 2, nch)


def _outproj_ffn(x, m, ab, u, conv_w, ybT, ycT, yd, w_out, g, wgu, wd):
    B, S, D = x.shape
    ts, _, _ = _tiles(S)
    nt = S // ts
    r8 = ts // 8
    tok = lambda b, t: (b, t, 0)
    whole = lambda b, t: (0, 0)
    return pl.pallas_call(
        functools.partial(_outproj_ffn_kernel, ts=ts, nt=nt, nch=wd.shape[0] // FF_CHUNK),
        grid=(B, nt),
        in_specs=[pl.BlockSpec((1, ts, D), tok),
                  pl.BlockSpec((1, N_MOD, D), lambda b, t: (b, 0, 0)),
                  pl.BlockSpec((1, ts, GROUP), tok),
                  pl.BlockSpec((1, ts, GROUP), tok),
                  pl.BlockSpec((1, 8, GROUP), lambda b, t: (b, jnp.maximum(t * r8 - 1, 0), 0)),
                  pl.BlockSpec((1, 8, GROUP), lambda b, t: (b, jnp.minimum((t + 1) * r8, S // 8 - 1), 0)),
                  pl.BlockSpec(conv_w.shape, lambda b, t: (0, 0)),
                  pl.BlockSpec((1, GROUP, ts), lambda b, t: (b, 0, t)),
                  pl.BlockSpec((1, GROUP, ts), lambda b, t: (b, 0, t)),
                  pl.BlockSpec((1, ts, GROUP), tok),
                  pl.BlockSpec(w_out.shape, lambda b, t: (0, 0), pipeline_mode=pl.Buffered(1)),
                  pl.BlockSpec((1, D), lambda b, t: (0, 0)),
                  pl.BlockSpec(wgu.shape, whole, pipeline_mode=pl.Buffered(1)),
                  pl.BlockSpec(wd.shape, whole, pipeline_mode=pl.Buffered(1))],
        out_specs=pl.BlockSpec((1, ts, D), tok),
        out_shape=jax.ShapeDtypeStruct((B, S, D), F32),
        scratch_shapes=[pltpu.VMEM((ts, D), BF16), pltpu.VMEM((ts, D), F32)],
        compiler_params=_cparams(("arbitrary", "arbitrary")),
        name="outproj_ffn",
    )(x, m, ab, u, u, u, conv_w, ybT, ycT, yd, w_out, g, wgu, wd)


def _rope_tables(S):
    inv = ROPE_THETA ** (-jnp.arange(0, MLA_ROPE, 2, dtype=F32) / MLA_ROPE)
    ang = inv[:, None] * jnp.arange(S, dtype=F32)[None, :]
    return jnp.cos(ang), jnp.sin(ang)


def _alibi_slopes():
    return tuple(float(2.0 ** (-8.0 * (i + 1) / HEADS)) for i in range(HEADS))


def _prep_layer(p, l):
    col = lambda v: v.astype(F32).reshape(-1, 1)
    w_in = p['w_in'][l].astype(BF16)
    return dict(
        ffn1=(p['ffn1_w_gu'][l].astype(BF16), p['ffn1_w_down'][l].astype(BF16)),
        ffn2=(p['ffn2_w_gu'][l].astype(BF16), p['ffn2_w_down'][l].astype(BF16)),
        w_nat=jnp.concatenate([w_in[:, :NAT_LO], w_in[:, ATT_HI:]], axis=1),
        w_T=w_in[:, NAT_LO:ATT_HI].T,
        wuqT=p['mla_w_uq'][l].astype(BF16).T,
        wukvT=p['mla_w_ukv'][l].astype(BF16).T,
        gdq=col(p['diff_q_g'][l]), gdk=col(p['diff_k_g'][l]),
        gqa=col(p['mla_q_a_g'][l]), gkva=col(p['mla_kv_a_g'][l]),
        gmq=col(p['mla_q_g'][l]), gmk=col(p['mla_k_g'][l]),
        gsub=col(p['diff_subln_g'][l]),
        lv=p['diff_lambda'][l].astype(F32),
        conv_w=p['conv_w'][l].astype(F32),
        w_out=p['w_out'][l].astype(BF16),
        norm_g=p['norm_g'][l].astype(F32),
    )


def _layer(x, m, w, l, consts):
    rope_cos, rope_sin, ftables = consts
    ng = w['norm_g']
    x = _ffn(x, m, ng[0:1], *w['ffn1'], 0)
    slope_parts = tuple(_bf16_parts(np.float32(s * LOG2E)) for s in _alibi_slopes())
    ab, u, fu, qd, kd, vd, qm, km, vm = _inproj(x, m, ng[1:2], w, rope_cos, rope_sin, slope_parts)
    lambda_init = 0.8 - 0.6 * math.exp(-0.3 * l)
    ybT = _attention(qd, kd, vd, w['gdq'], w['gdk'], w['gsub'], w['lv'],
                     width=DIFF_QK, n_maps=2, alibi=True, lambda_init=lambda_init)
    ycT = _attention(qm, km, vm, w['gmq'], w['gmk'], w['gsub'], w['lv'],
                     width=MLA_QK, n_maps=1, alibi=False, lambda_init=0.0)
    yd = _fourier(fu, ftables)
    return _outproj_ffn(x, m, ab, u, w['conv_w'], ybT, ycT, yd, w['w_out'], ng[2:3], *w['ffn2'])


def _trunk(groups, c_all, p):
    depth = p['w_mod'].shape[0]
    mods = _modulation(c_all, p['w_mod'], p['b_mod'])
    consts = []
    for x, _ in groups:
        S = x.shape[1]
        consts.append(_rope_tables(S) + (_fourier_tables(S),))
    xs = [x for x, _ in groups]
    for l in range(depth):
        w = _prep_layer(p, l)
        for gi, (x0, off) in enumerate(groups):
            B, _, D = x0.shape
            m = mods[l, off:off + B].reshape(B, N_MOD, D)
            xs[gi] = _layer(xs[gi], m, w, l, consts[gi])
    return xs


def kernel(x_prompt, x_sample, c_prompt, c_sample, w_mod, b_mod, norm_g, ffn1_w_gu, ffn1_w_down,
           w_in, conv_w, diff_lambda, diff_q_g, diff_k_g, diff_subln_g, mla_q_a_g, mla_w_uq,
           mla_kv_a_g, mla_w_ukv, mla_q_g, mla_k_g, w_out, ffn2_w_gu, ffn2_w_down):
    p = dict(w_mod=w_mod, b_mod=b_mod, norm_g=norm_g, ffn1_w_gu=ffn1_w_gu, ffn1_w_down=ffn1_w_down,
             w_in=w_in, conv_w=conv_w, diff_lambda=diff_lambda, diff_q_g=diff_q_g, diff_k_g=diff_k_g,
             diff_subln_g=diff_subln_g, mla_q_a_g=mla_q_a_g, mla_w_uq=mla_w_uq, mla_kv_a_g=mla_kv_a_g,
             mla_w_ukv=mla_w_ukv, mla_q_g=mla_q_g, mla_k_g=mla_k_g, w_out=w_out,
             ffn2_w_gu=ffn2_w_gu, ffn2_w_down=ffn2_w_down)
    nb = c_prompt.shape[0] + c_sample.shape[0]
    pad = (-nb) % 8
    c_all = jnp.concatenate([c_prompt, c_sample, jnp.zeros((pad, c_prompt.shape[1]), c_prompt.dtype)])
    y_prompt, y_sample = _trunk([(x_prompt, 0), (x_sample, c_prompt.shape[0])], c_all, p)
    return (y_prompt, y_sample)
```

```python
import functools
import math

import numpy as np
import jax
import jax.numpy as jnp
from jax import lax
from jax.experimental import pallas as pl
from jax.experimental.pallas import tpu as pltpu

F32 = jnp.float32
BF16 = jnp.bfloat16

HEADS = 4
GROUP = 256
DIFF_QK = 32
MLA_NOPE = 64
MLA_ROPE = 32
MLA_QK = MLA_NOPE + MLA_ROPE
MLA_Q_RANK = 256
MLA_KV_RANK = 128
HEAD_V = 64
FOURIER_GROUPS = 4
ROPE_THETA = 10000.0
EPS = 1e-6
N_MOD = 9
ATT_ROWS = 3 * GROUP + MLA_Q_RANK + MLA_KV_RANK + MLA_ROPE
NAT_LO = 3 * GROUP
ATT_HI = NAT_LO + ATT_ROWS

BF16_SUBLANES = 16
KPAD = 128
V_ROWS = 128
AUG_ROWS = BF16_SUBLANES
FF_CHUNK = 256
FFN_TOKENS = 512
TOKEN_TILE = 512
MOD_COLS = 1536
INPROJ_BLOCKS = 2
ATTN_GROUP = 8
VMEM_LIMIT = 52 * 1024 * 1024
ATTN_VMEM_BUDGET = 44 * 1024 * 1024
NEG_BIG = -1e30
LOG2E = 1.4426950408889634
FAST_SOFTMAX_BOUND = 25.0
BOUND_MARGIN = 1.02


def _mod3(i):
    three = lambda c: jnp.where(c, 3, 0)
    return i - three(i >= 3) - three(i >= 6) - three(i >= 9)


def _bf16_parts(x):
    parts, r = [], np.float32(x)
    for _ in range(3):
        p = np.float32(np.asarray(r, dtype=jnp.bfloat16))
        parts.append(float(p))
        r = np.float32(r - p)
    return tuple(parts)


def _cparams(sem):
    return pltpu.CompilerParams(dimension_semantics=sem, vmem_limit_bytes=VMEM_LIMIT)


def _tiles(S):
    t = min(TOKEN_TILE, S)
    return t, t, t


def _fourier_split(S):
    n1 = 1 << ((int(math.log2(S)) + 1) // 2)
    return n1, S // n1


def _mod_kernel(c_ref, w_ref, b_ref, o_ref):
    c = c_ref[...]
    s = (c * jax.nn.sigmoid(c)).astype(BF16)
    o_ref[0] = jnp.dot(s, w_ref[0].astype(BF16), preferred_element_type=F32) + b_ref[0]


def _modulation(c_all, w_mod, b_mod):
    L, D, N = w_mod.shape
    bp = c_all.shape[0]
    tn = MOD_COLS
    assert N % tn == 0
    return pl.pallas_call(
        _mod_kernel,
        grid=(L, N // tn),
        in_specs=[pl.BlockSpec((bp, D), lambda l, j: (0, 0)),
                  pl.BlockSpec((1, D, tn), lambda l, j: (l, 0, j)),
                  pl.BlockSpec((1, 1, tn), lambda l, j: (l, 0, j))],
        out_specs=pl.BlockSpec((1, bp, tn), lambda l, j: (l, 0, j)),
        out_shape=jax.ShapeDtypeStruct((L, bp, N), F32),
        compiler_params=_cparams(("arbitrary", "arbitrary")),
        name="modulation",
    )(c_all, w_mod, b_mod.reshape(L, 1, N))


def _modulated_norm(x, m_ref, g_ref, sub):
    shift = m_ref[0, 3 * sub:3 * sub + 1, :]
    scale = m_ref[0, 3 * sub + 1:3 * sub + 2, :]
    ms = jnp.mean(x * x, axis=-1, keepdims=True)
    y = x * lax.rsqrt(ms + EPS) * g_ref[...]
    return y * (1.0 + scale) + shift


def _ffn_tile(x, m_ref, g_ref, wgu_ref, wd_ref, h_scr, acc_scr, sub, nch):
    h_scr[...] = _modulated_norm(x, m_ref, g_ref, sub).astype(BF16)
    acc_scr[...] = jnp.zeros_like(acc_scr)

    hidden = nch * FF_CHUNK
    for c in range(nch):
        cols = slice(c * FF_CHUNK, (c + 1) * FF_CHUNK)
        g = jnp.dot(h_scr[...], wgu_ref[:, cols], preferred_element_type=F32)
        u = jnp.dot(h_scr[...], wgu_ref[:, hidden + c * FF_CHUNK:hidden + (c + 1) * FF_CHUNK],
                    preferred_element_type=F32)
        a = (g * jax.nn.sigmoid(g) * u).astype(BF16)
        acc_scr[...] += jnp.dot(a, wd_ref[cols, :], preferred_element_type=F32)
    gate = m_ref[0, 3 * sub + 2:3 * sub + 3, :]
    return x + 0.5 * gate * acc_scr[...]


def _ffn_kernel(x_ref, m_ref, g_ref, wgu_ref, wd_ref, o_ref, h_scr, acc_scr, *, sub, nch):
    o_ref[0] = _ffn_tile(x_ref[0], m_ref, g_ref, wgu_ref, wd_ref, h_scr, acc_scr, sub, nch)


def _ffn(x, m, g, wgu, wd, sub):
    B, S, D = x.shape
    ts = min(FFN_TOKENS, S)
    nch = wd.shape[0] // FF_CHUNK
    whole = lambda b, t: (0, 0)
    return pl.pallas_call(
        functools.partial(_ffn_kernel, sub=sub, nch=nch),
        grid=(B, S // ts),
        in_specs=[pl.BlockSpec((1, ts, D), lambda b, t: (b, t, 0)),
                  pl.BlockSpec((1, N_MOD, D), lambda b, t: (b, 0, 0)),
                  pl.BlockSpec((1, D), lambda b, t: (0, 0)),
                  pl.BlockSpec(wgu.shape, whole, pipeline_mode=pl.Buffered(1)),
                  pl.BlockSpec(wd.shape, whole, pipeline_mode=pl.Buffered(1))],
        out_specs=pl.BlockSpec((1, ts, D), lambda b, t: (b, t, 0)),
        out_shape=jax.ShapeDtypeStruct((B, S, D), F32),
        scratch_shapes=[pltpu.VMEM((ts, D), BF16), pltpu.VMEM((ts, D), F32)],
        compiler_params=_cparams(("arbitrary", "arbitrary")),
        name="ffn",
    )(x, m, g, wgu, wd)


def _rms_rows(v, gain):
    inv = lax.rsqrt(jnp.mean(v * v, axis=0, keepdims=True) + EPS)
    return v * inv * gain


def _rope_rows(t1, t2, cos, sin):
    return t1 * cos - t2 * sin, t1 * sin + t2 * cos


def _ones_row_tile(tk):
    rid = lax.broadcasted_iota(jnp.int32, (V_ROWS - HEAD_V, tk), 0)
    return jnp.where(rid == 0, 1.0, 0.0).astype(BF16)


def _inproj_kernel(x_ref, m_ref, g_ref, wnat_ref, wT_ref, wuqT_ref, wukvT_ref,
                   gdq_ref, gdk_ref, gqa_ref, gkva_ref, gmq_ref, gmk_ref, cos_ref, sin_ref,
                   ab_ref, u_ref, fu_ref, qd_ref, kd_ref, vd_ref, qm_ref, km_ref, vm_ref,
                   *, ts, tk, slope_parts):
    sub = range(ts // tk)
    projected = [_inproj_project(si, x_ref, m_ref, g_ref, wnat_ref, wT_ref, ab_ref, u_ref, fu_ref, tk)
                 for si in sub]
    for si in sub:
        _inproj_operands(si, projected[si], wuqT_ref, wukvT_ref,
                         gdq_ref, gdk_ref, gqa_ref, gkva_ref, gmq_ref, gmk_ref, cos_ref, sin_ref,
                         qd_ref, kd_ref, vd_ref, qm_ref, km_ref, vm_ref,
                         ts=ts, tk=tk, slope_parts=slope_parts)


def _inproj_project(si, x_ref, m_ref, g_ref, wnat_ref, wT_ref, ab_ref, u_ref, fu_ref, tk):
    rows = slice(si * tk, (si + 1) * tk)
    hb = _modulated_norm(x_ref[0, rows], m_ref, g_ref, 1).astype(BF16)

    nat = jnp.dot(hb, wnat_ref[...], preferred_element_type=F32)
    ab_ref[0, rows] = nat[:, 0:GROUP]
    u_ref[0, rows] = nat[:, GROUP:2 * GROUP] * nat[:, 2 * GROUP:3 * GROUP]
    fu_ref[0, rows] = nat[:, 3 * GROUP:4 * GROUP].astype(BF16)

    return lax.dot_general(wT_ref[...], hb, (((1,), (1,)), ((), ())), preferred_element_type=F32)


def _inproj_operands(si, pT, wuqT_ref, wukvT_ref,
                     gdq_ref, gdk_ref, gqa_ref, gkva_ref, gmq_ref, gmk_ref, cos_ref, sin_ref,
                     qd_ref, kd_ref, vd_ref, qm_ref, km_ref, vm_ref, *, ts, tk, slope_parts):
    t = pl.program_id(1)
    rows = slice(si * tk, (si + 1) * tk)
    ones_tile = _ones_row_tile(tk)
    pos = t * ts + si * tk + lax.broadcasted_iota(jnp.int32, (1, tk), 1)
    c = pos & (tk - 1)
    c_hi = ((c >> 8) << 8).astype(F32)
    c_lo = (c & 255).astype(F32)
    rid = lax.broadcasted_iota(jnp.int32, (AUG_ROWS, tk), 0)
    pad_d = jnp.zeros((KPAD - 2 * DIFF_QK - AUG_ROWS, tk), F32)
    pad_m = jnp.zeros((KPAD - MLA_QK, tk), F32)
    gdq = gdq_ref[...] * (DIFF_QK ** -0.5 * LOG2E)
    third = _mod3(rid)
    gdk = gdk_ref[...]

    for h in range(HEADS):
        base = 2 * DIFF_QK * h
        ks = []
        for j in range(2):
            r0 = base + DIFF_QK * j
            qd_ref[0, h, DIFF_QK * j:DIFF_QK * (j + 1), rows] = _rms_rows(pT[r0:r0 + DIFF_QK], gdq).astype(BF16)
            ks.append(_rms_rows(pT[GROUP + r0:GROUP + r0 + DIFF_QK], gdk))
        s0, s1, s2 = slope_parts[h]
        part = jnp.where(third == 0, s0, jnp.where(third == 1, s1, s2))
        aug = jnp.where(rid < 6, part, jnp.where(rid < 9, c_hi, jnp.where(rid < 12, c_lo, 0.0)))
        kext = jnp.concatenate(ks + [aug, pad_d], axis=0)
        kd_ref[0, h, rows] = kext.T.astype(BF16)
        vd_ref[0, h, si, 0:HEAD_V, :] = pT[2 * GROUP + HEAD_V * h:2 * GROUP + HEAD_V * (h + 1)].astype(BF16)
        vd_ref[0, h, si, HEAD_V:V_ROWS, :] = ones_tile

    o_cq = 3 * GROUP
    o_ckv = o_cq + MLA_Q_RANK
    o_kpe = o_ckv + MLA_KV_RANK
    cqn = _rms_rows(pT[o_cq:o_ckv], gqa_ref[...]).astype(BF16)
    qmT = jnp.dot(wuqT_ref[...], cqn, preferred_element_type=F32)
    ckvn = _rms_rows(pT[o_ckv:o_kpe], gkva_ref[...]).astype(BF16)
    kvT = jnp.dot(wukvT_ref[...], ckvn, preferred_element_type=F32)
    kpe = pT[o_kpe:o_kpe + MLA_ROPE]
    cos = cos_ref[:, rows]
    sin = sin_ref[:, rows]
    gmq = gmq_ref[...] * (MLA_QK ** -0.5 * LOG2E)
    gmk = gmk_ref[...]
    half = MLA_ROPE // 2
    for h in range(HEADS):
        qn = _rms_rows(qmT[MLA_QK * h:MLA_QK * (h + 1)], gmq)
        q1, q2 = _rope_rows(qn[MLA_NOPE:MLA_NOPE + half], qn[MLA_NOPE + half:MLA_QK], cos, sin)
        qm_ref[0, h, 0:MLA_NOPE, rows] = qn[0:MLA_NOPE].astype(BF16)
        qm_ref[0, h, MLA_NOPE:MLA_NOPE + half, rows] = q1.astype(BF16)
        qm_ref[0, h, MLA_NOPE + half:MLA_QK, rows] = q2.astype(BF16)
        kv0 = (MLA_NOPE + HEAD_V) * h
        kn = _rms_rows(jnp.concatenate([kvT[kv0:kv0 + MLA_NOPE], kpe], axis=0), gmk)
        k1, k2 = _rope_rows(kn[MLA_NOPE:MLA_NOPE + half], kn[MLA_NOPE + half:MLA_QK], cos, sin)
        kext = jnp.concatenate([kn[0:MLA_NOPE], k1, k2, pad_m], axis=0)
        km_ref[0, h, rows] = kext.T.astype(BF16)
        vm_ref[0, h, si, 0:HEAD_V, :] = kvT[kv0 + MLA_NOPE:kv0 + MLA_NOPE + HEAD_V].astype(BF16)
        vm_ref[0, h, si, HEAD_V:V_ROWS, :] = ones_tile


def _inproj(x, m, g, w, rope_cos, rope_sin, slope_parts):
    B, S, D = x.shape
    _, _, tk = _tiles(S)
    ts = min(INPROJ_BLOCKS * tk, S)
    nkb = S // tk
    c2 = lambda b, t: (0, 0)
    tok = lambda b, t: (b, t, 0)
    head_T = lambda b, t: (b, 0, 0, t)
    head_N = lambda b, t: (b, 0, t, 0)
    head_V = lambda b, t: (b, 0, t, 0, 0)
    small = [w['gdq'], w['gdk'], w['gqa'], w['gkva'], w['gmq'], w['gmk']]
    in_specs = ([pl.BlockSpec((1, ts, D), tok),
                 pl.BlockSpec((1, N_MOD, D), lambda b, t: (b, 0, 0)),
                 pl.BlockSpec((1, D), c2),
                 pl.BlockSpec(w['w_nat'].shape, c2, pipeline_mode=pl.Buffered(1)),
                 pl.BlockSpec(w['w_T'].shape, c2, pipeline_mode=pl.Buffered(1)),
                 pl.BlockSpec(w['wuqT'].shape, c2, pipeline_mode=pl.Buffered(1)),
                 pl.BlockSpec(w['wukvT'].shape, c2, pipeline_mode=pl.Buffered(1))]
                + [pl.BlockSpec(a.shape, c2) for a in small]
                + [pl.BlockSpec((MLA_ROPE // 2, ts), lambda b, t: (0, t))] * 2)
    out_shape = [jax.ShapeDtypeStruct((B, S, GROUP), F32),
                 jax.ShapeDtypeStruct((B, S, GROUP), F32),
                 jax.ShapeDtypeStruct((B, S, GROUP), BF16),
                 jax.ShapeDtypeStruct((B, HEADS, 2 * DIFF_QK, S), BF16),
                 jax.ShapeDtypeStruct((B, HEADS, S, KPAD), BF16),
                 jax.ShapeDtypeStruct((B, HEADS, nkb, V_ROWS, tk), BF16),
                 jax.ShapeDtypeStruct((B, HEADS, MLA_QK, S), BF16),
                 jax.ShapeDtypeStruct((B, HEADS, S, KPAD), BF16),
                 jax.ShapeDtypeStruct((B, HEADS, nkb, V_ROWS, tk), BF16)]
    out_specs = [pl.BlockSpec((1, ts, GROUP), tok),
                 pl.BlockSpec((1, ts, GROUP), tok),
                 pl.BlockSpec((1, ts, GROUP), tok),
                 pl.BlockSpec((1, HEADS, 2 * DIFF_QK, ts), head_T),
                 pl.BlockSpec((1, HEADS, ts, KPAD), head_N),
                 pl.BlockSpec((1, HEADS, ts // tk, V_ROWS, tk), head_V),
                 pl.BlockSpec((1, HEADS, MLA_QK, ts), head_T),
                 pl.BlockSpec((1, HEADS, ts, KPAD), head_N),
                 pl.BlockSpec((1, HEADS, ts // tk, V_ROWS, tk), head_V)]
    return pl.pallas_call(
        functools.partial(_inproj_kernel, ts=ts, tk=tk, slope_parts=slope_parts),
        grid=(B, S // ts),
        in_specs=in_specs,
        out_specs=out_specs,
        out_shape=out_shape,
        compiler_params=_cparams(("arbitrary", "arbitrary")),
        name="inproj",
    )(x, m, g, w['w_nat'], w['w_T'], w['wuqT'], w['wukvT'], *small, rope_cos, rope_sin)


def _attn_query_tile(qi, sc_ref, qT_ref, k_ref, vT_ref, gsub_ref, lv_ref, o_ref, rhs_scr, p_scr,
                     *, n_maps, alibi, online, tq, tk, nkb, group_size, merge_maps, lambda_init):
    h = pl.program_id(1)
    q0 = pl.multiple_of(qi * tq, tq)
    q = qT_ref[0, 0, :, pl.ds(q0, tq)]
    shift = sc_ref[h, 4]

    def key_rows(kb, n):
        return k_ref[0, 0, pl.ds(pl.multiple_of(kb * tk, tk), n * tk), :]

    if alibi:
        slope2 = sc_ref[h, 0]
        lane = lax.broadcasted_iota(jnp.int32, (1, tq), 1)
        r_hi = ((lane >> 8) << 8).astype(F32)
        r_lo = (lane & 255).astype(F32)
        rid = lax.broadcasted_iota(jnp.int32, (AUG_ROWS, tq), 0)
        third = _mod3(rid)
        part = jnp.where(third == 0, sc_ref[h, 1], jnp.where(third == 1, sc_ref[h, 2], sc_ref[h, 3]))
        aug = jnp.where(rid < 3, -r_hi, jnp.where(rid < 6, -r_lo, jnp.where(rid < 12, part, 0.0))).astype(BF16)
        zq = jnp.zeros((DIFF_QK, tq), BF16)
        zp = jnp.zeros((KPAD - 2 * DIFF_QK - AUG_ROWS, tq), BF16)
        rhs_scr[:, 0:tq] = jnp.concatenate([q[0:DIFF_QK], zq, aug, zp], axis=0)
        rhs_scr[:, tq:2 * tq] = jnp.concatenate([zq, q[DIFF_QK:2 * DIFF_QK], aug, zp], axis=0)
        col = lax.broadcasted_iota(jnp.int32, (1, KPAD), 1)
        is_aug = (col >= 2 * DIFF_QK) & (col < 2 * DIFF_QK + AUG_ROWS)
        flip = jnp.where(is_aug, -1.0, 1.0).astype(BF16)
        keep = jnp.ones((1, KPAD), BF16)
        d_idx = lax.shift_right_logical(q0, int(math.log2(tk)))
    else:
        rhs_scr[...] = jnp.concatenate([q, jnp.zeros((KPAD - q.shape[0], tq), BF16)], axis=0)

    def accumulate(j, u, kb, s, cb, accs, ms):
        vblk = vT_ref[0, 0, kb]
        if online:
            m_new = jnp.maximum(ms[j], jnp.max(s, axis=0, keepdims=True) + cb)
            p = jnp.exp2(s + (cb - m_new)).astype(BF16)
            return (jnp.exp2(ms[j] - m_new) * accs[j] + jnp.dot(vblk, p, preferred_element_type=F32), m_new)
        p_scr[j, u] = jnp.exp2(s + (cb - shift)).astype(BF16)
        return accs[j] + jnp.dot(vblk, p_scr[j, u], preferred_element_type=F32), ms[j]

    def group(g, carry, first):
        accs, ms = carry
        accs, ms = list(accs), list(ms)
        base = g * group_size
        if alibi:
            kbs = [(d_idx + base + u) & (nkb - 1) for u in range(group_size)]
            parts = [key_rows(kb, 1) * jnp.where(kb > d_idx, flip, keep) for kb in kbs]
            if first:
                parts = [parts[0] * flip] + parts
            kgrp = jnp.concatenate(parts, axis=0)
        else:
            kbs = [base + u for u in range(group_size)]
            kgrp = key_rows(base, group_size)
        off = tk if (alibi and first) else 0
        if merge_maps:
            s_all = jnp.dot(kgrp, rhs_scr[...], preferred_element_type=F32)
        for j in range(n_maps):
            if merge_maps:
                s = s_all[:, j * tq:(j + 1) * tq]
            else:
                s = jnp.dot(kgrp, rhs_scr[:, j * tq:(j + 1) * tq], preferred_element_type=F32)
            for u, kb in enumerate(kbs):
                s_u = s[off + u * tk:off + (u + 1) * tk]
                if alibi and first and u == 0:
                    s_u = jnp.minimum(s_u, s[0:tk])
                cb = -slope2 * jnp.abs(q0 - kb * tk).astype(F32) if alibi else 0.0
                accs[j], ms[j] = accumulate(j, u, kb, s_u, cb, accs, ms)
        return tuple(accs), tuple(ms)

    carry = (tuple(jnp.zeros((V_ROWS, tq), F32) for _ in range(n_maps)),
             tuple(jnp.full((1, tq), NEG_BIG, F32) for _ in range(n_maps)))
    first_rest = 0
    if alibi:
        carry = group(0, carry, True)
        first_rest = 1
    carry = lax.fori_loop(first_rest, nkb // group_size, lambda g, c: group(g, c, False), carry)
    accs, _ = carry

    a0 = accs[0]
    o = a0[0:HEAD_V] / a0[HEAD_V:HEAD_V + 1]
    if n_maps == 2:
        a1 = accs[1]
        lv = lv_ref[...]
        lam = (jnp.exp(jnp.sum(lv[0:1] * lv[1:2], axis=1, keepdims=True))
               - jnp.exp(jnp.sum(lv[2:3] * lv[3:4], axis=1, keepdims=True)) + lambda_init)
        o = o - lam * (a1[0:HEAD_V] / a1[HEAD_V:HEAD_V + 1])
        o = _rms_rows(o, gsub_ref[...]) * (1.0 - lambda_init)
    o_ref[0, :, pl.ds(q0, tq)] = o.astype(BF16)


def _attn_kernel(*refs, nq, **kw):
    def tile(qi, carry):
        _attn_query_tile(qi, *refs, **kw)
        return carry
    lax.fori_loop(0, nq, tile, 0)


def _attention_call(sc, qT, k, vT, gsub, lv, *, n_maps, alibi, online, lambda_init):
    B, H, kq, S = qT.shape
    _, tq, tk = _tiles(S)
    nkb = S // tk
    group_size = 1 if online else min(ATTN_GROUP, nkb)
    assert nkb % group_size == 0 and nkb & (nkb - 1) == 0, "rotated block order wraps with a mask"
    single = B * H <= 8
    in_mode = dict(pipeline_mode=pl.Buffered(1)) if single else {}
    resident = 2 * S * ((1 if single else 2) * (kq + KPAD + V_ROWS) + 2 * HEAD_V)
    tiles = (group_size + 1) * tk * 2 * tq * (4 + 2)
    widen = not online and resident + tiles <= ATTN_VMEM_BUDGET
    merge_maps = widen and n_maps > 1
    if widen and not alibi and 2 * tq <= S:
        tq = 2 * tq
    assert not alibi or tq == tk, "one key block per query tile sits on the diagonal"
    kern = functools.partial(_attn_kernel, n_maps=n_maps, alibi=alibi, online=online, tq=tq, tk=tk,
                             nkb=nkb, group_size=group_size, merge_maps=merge_maps,
                             lambda_init=lambda_init, nq=S // tq)
    n_rhs = n_maps
    name = ("attn_diff" if alibi else "attn_mla") + ("_online" if online else "")
    return pl.pallas_call(
        kern,
        grid=(B, H),
        in_specs=[pl.BlockSpec(memory_space=pltpu.SMEM),
                  pl.BlockSpec((1, 1, kq, S), lambda b, h: (b, h, 0, 0), **in_mode),
                  pl.BlockSpec((1, 1, S, KPAD), lambda b, h: (b, h, 0, 0), **in_mode),
                  pl.BlockSpec((1, 1, nkb, V_ROWS, tk), lambda b, h: (b, h, 0, 0, 0), **in_mode),
                  pl.BlockSpec(gsub.shape, lambda b, h: (0, 0)),
                  pl.BlockSpec(lv.shape, lambda b, h: (0, 0))],
        out_specs=pl.BlockSpec((1, HEAD_V, S), lambda b, h: (b, h, 0)),
        out_shape=jax.ShapeDtypeStruct((B, H * HEAD_V, S), BF16),
        scratch_shapes=[pltpu.VMEM((KPAD, n_rhs * tq), BF16),
                        pltpu.VMEM((n_maps, group_size, tk, tq), BF16)],
        compiler_params=_cparams(("arbitrary", "arbitrary")),
        name=name,
    )(sc, qT, k, vT, gsub, lv)


def _attention(qT, k, vT, gq, gk, gsub, lv, *, width, n_maps, alibi, lambda_init):
    bound = math.sqrt(width) * BOUND_MARGIN * jnp.max(jnp.abs(gq)) * jnp.max(jnp.abs(gk))
    slopes2 = [np.float32(s * LOG2E) for s in _alibi_slopes()]
    static = jnp.asarray([[s, *_bf16_parts(s)] for s in slopes2], F32)
    sc = jnp.concatenate([static, jnp.full((HEADS, 1), LOG2E, F32) * bound,
                          jnp.zeros((HEADS, 3), F32)], axis=1)
    call = functools.partial(_attention_call, n_maps=n_maps, alibi=alibi, lambda_init=lambda_init)
    return lax.cond(bound <= FAST_SOFTMAX_BOUND,
                    functools.partial(call, online=False),
                    functools.partial(call, online=True),
                    sc, qT, k, vT, gsub, lv)


def _f1_kernel(t_ref, x_ref, o_ref):
    o_ref[0] = jnp.dot(t_ref[...], x_ref[0], preferred_element_type=F32).astype(BF16)


def _f2_kernel(g_ref, a_ref, cs_ref, o_ref, *, tk1, n2, ch, norm):
    xr, xi = [], []
    for j in range(tk1):
        z = jnp.concatenate([a_ref[0, 0, j], a_ref[0, 1, j]], axis=0)
        x = jnp.dot(g_ref[j], z, preferred_element_type=F32)
        xr.append(x[:n2].astype(BF16))
        xi.append(x[n2:].astype(BF16))
    y = (jnp.dot(jnp.concatenate(xr, axis=0), cs_ref[0:ch], preferred_element_type=F32)
         + jnp.dot(jnp.concatenate(xi, axis=0), cs_ref[ch:2 * ch], preferred_element_type=F32))
    for j in range(tk1):
        o_ref[0, :, j, :] = y[j * n2:(j + 1) * n2] * norm


def _fourier_tables(S):
    n1, n2 = _fourier_split(S)
    a = jnp.arange(n1, dtype=jnp.int32)
    ang1 = (2.0 * math.pi / n1) * ((a[:, None] * a[None, :]) % n1).astype(F32)
    t1 = jnp.concatenate([jnp.cos(ang1), -jnp.sin(ang1)], axis=0).astype(BF16)
    k = a[:, None, None] + n1 * jnp.arange(n2, dtype=jnp.int32)[None, :, None]
    b = jnp.arange(n2, dtype=jnp.int32)[None, None, :]
    ang = (2.0 * math.pi / S) * ((k * b) % S).astype(F32)
    cg, sg = jnp.cos(ang), jnp.sin(ang)
    g = jnp.concatenate([jnp.concatenate([cg, sg], axis=2),
                         jnp.concatenate([-sg, cg], axis=2)], axis=1).astype(BF16)
    gw = GROUP // FOURIER_GROUPS
    c = jnp.arange(GROUP, dtype=jnp.int32)
    same = (c[:, None] // gw) == (c[None, :] // gw)
    angc = (2.0 * math.pi / gw) * (((c[:, None] % gw) * (c[None, :] % gw)) % gw).astype(F32)
    cs = jnp.concatenate([jnp.where(same, jnp.cos(angc), 0.0),
                          jnp.where(same, jnp.sin(angc), 0.0)], axis=0).astype(BF16)
    return t1, g, cs


def _fourier(fu, tables):
    B, S, C = fu.shape
    n1, n2 = _fourier_split(S)
    t1, g, cs = tables
    tn = min(8192, n2 * C)
    a = pl.pallas_call(
        _f1_kernel,
        grid=(B, n2 * C // tn),
        in_specs=[pl.BlockSpec(t1.shape, lambda b, j: (0, 0)),
                  pl.BlockSpec((1, n1, tn), lambda b, j: (b, 0, j))],
        out_specs=pl.BlockSpec((1, 2 * n1, tn), lambda b, j: (b, 0, j)),
        out_shape=jax.ShapeDtypeStruct((B, 2 * n1, n2 * C), BF16),
        compiler_params=_cparams(("arbitrary", "arbitrary")),
        name="fourier_stage1",
    )(t1, fu.reshape(B, n1, n2 * C))
    tk1 = min(16, n1)
    norm = 1.0 / math.sqrt(S * (GROUP // FOURIER_GROUPS))
    y = pl.pallas_call(
        functools.partial(_f2_kernel, tk1=tk1, n2=n2, ch=C, norm=norm),
        grid=(B, n1 // tk1),
        in_specs=[pl.BlockSpec((tk1, 2 * n2, 2 * n2), lambda b, j: (j, 0, 0)),
                  pl.BlockSpec((1, 2, tk1, n2, C), lambda b, j: (b, 0, j, 0, 0)),
                  pl.BlockSpec(cs.shape, lambda b, j: (0, 0))],
        out_specs=pl.BlockSpec((1, n2, tk1, C), lambda b, j: (b, 0, j, 0)),
        out_shape=jax.ShapeDtypeStruct((B, n2, n1, C), F32),
        compiler_params=_cparams(("arbitrary", "arbitrary")),
        name="fourier_stage2",
    )(g, a.reshape(B, 2, n1, n2, C), cs)
    return y.reshape(B, S, C)


def _outproj_ffn_kernel(x_ref, m_ref, ab_ref, u_ref, up_ref, un_ref, cw_ref, yb_ref, yc_ref, yd_ref,
                        w_ref, g_ref, wgu_ref, wd_ref, o_ref, h_scr, acc_scr, *, ts, nt, nch):
    t = pl.program_id(1)
    u = u_ref[0]
    prev_row = jnp.where(t == 0, 0.0, up_ref[0, 7:8, :])
    next_row = jnp.where(t == nt - 1, 0.0, un_ref[0, 0:1, :])
    rid = lax.broadcasted_iota(jnp.int32, u.shape, 0)
    u_m1 = jnp.where(rid == 0, prev_row, pltpu.roll(u, 1, 0))
    u_p1 = jnp.where(rid == ts - 1, next_row, pltpu.roll(u, ts - 1, 0))
    conv = u_m1 * cw_ref[0:1, :] + u * cw_ref[1:2, :] + u_p1 * cw_ref[2:3, :]
    ya = (ab_ref[0] * conv).astype(BF16)
    tn_dims = (((0,), (0,)), ((), ()))
    acc = jnp.dot(ya, w_ref[0:GROUP], preferred_element_type=F32)
    acc += lax.dot_general(yb_ref[0], w_ref[GROUP:2 * GROUP], tn_dims, preferred_element_type=F32)
    acc += lax.dot_general(yc_ref[0], w_ref[2 * GROUP:3 * GROUP], tn_dims, preferred_element_type=F32)
    acc += jnp.dot(yd_ref[0].astype(BF16), w_ref[3 * GROUP:4 * GROUP], preferred_element_type=F32)
    x_mid = x_ref[0] + m_ref[0, 5:6, :] * acc
    o_ref[0] = _ffn_tile(x_mid, m_ref, g_ref, wgu_ref, wd_ref, h_scr, acc_scr, 2, nch)


def _outproj_ffn(x, m, ab, u, conv_w, ybT, ycT, yd, w_out, g, wgu, wd):
    B, S, D = x.shape
    ts, _, _ = _tiles(S)
    nt = S // ts
    r8 = ts // 8
    tok = lambda b, t: (b, t, 0)
    whole = lambda b, t: (0, 0)
    return pl.pallas_call(
        functools.partial(_outproj_ffn_kernel, ts=ts, nt=nt, nch=wd.shape[0] // FF_CHUNK),
        grid=(B, nt),
        in_specs=[pl.BlockSpec((1, ts, D), tok),
                  pl.BlockSpec((1, N_MOD, D), lambda b, t: (b, 0, 0)),
                  pl.BlockSpec((1, ts, GROUP), tok),
                  pl.BlockSpec((1, ts, GROUP), tok),
                  pl.BlockSpec((1, 8, GROUP), lambda b, t: (b, jnp.maximum(t * r8 - 1, 0), 0)),
                  pl.BlockSpec((1, 8, GROUP), lambda b, t: (b, jnp.minimum((t + 1) * r8, S // 8 - 1), 0)),
                  pl.BlockSpec(conv_w.shape, lambda b, t: (0, 0)),
                  pl.BlockSpec((1, GROUP, ts), lambda b, t: (b, 0, t)),
                  pl.BlockSpec((1, GROUP, ts), lambda b, t: (b, 0, t)),
                  pl.BlockSpec((1, ts, GROUP), tok),
                  pl.BlockSpec(w_out.shape, lambda b, t: (0, 0), pipeline_mode=pl.Buffered(1)),
                  pl.BlockSpec((1, D), lambda b, t: (0, 0)),
                  pl.BlockSpec(wgu.shape, whole, pipeline_mode=pl.Buffered(1)),
                  pl.BlockSpec(wd.shape, whole, pipeline_mode=pl.Buffered(1))],
        out_specs=pl.BlockSpec((1, ts, D), tok),
        out_shape=jax.ShapeDtypeStruct((B, S, D), F32),
        scratch_shapes=[pltpu.VMEM((ts, D), BF16), pltpu.VMEM((ts, D), F32)],
        compiler_params=_cparams(("arbitrary", "arbitrary")),
        name="outproj_ffn",
    )(x, m, ab, u, u, u, conv_w, ybT, ycT, yd, w_out, g, wgu, wd)


def _rope_tables(S):
    inv = ROPE_THETA ** (-jnp.arange(0, MLA_ROPE, 2, dtype=F32) / MLA_ROPE)
    ang = inv[:, None] * jnp.arange(S, dtype=F32)[None, :]
    return jnp.cos(ang), jnp.sin(ang)


def _alibi_slopes():
    return tuple(float(2.0 ** (-8.0 * (i + 1) / HEADS)) for i in range(HEADS))


def _prep_layer(p, l):
    col = lambda v: v.astype(F32).reshape(-1, 1)
    w_in = p['w_in'][l].astype(BF16)
    return dict(
        ffn1=(p['ffn1_w_gu'][l].astype(BF16), p['ffn1_w_down'][l].astype(BF16)),
        ffn2=(p['ffn2_w_gu'][l].astype(BF16), p['ffn2_w_down'][l].astype(BF16)),
        w_nat=jnp.concatenate([w_in[:, :NAT_LO], w_in[:, ATT_HI:]], axis=1),
        w_T=w_in[:, NAT_LO:ATT_HI].T,
        wuqT=p['mla_w_uq'][l].astype(BF16).T,
        wukvT=p['mla_w_ukv'][l].astype(BF16).T,
        gdq=col(p['diff_q_g'][l]), gdk=col(p['diff_k_g'][l]),
        gqa=col(p['mla_q_a_g'][l]), gkva=col(p['mla_kv_a_g'][l]),
        gmq=col(p['mla_q_g'][l]), gmk=col(p['mla_k_g'][l]),
        gsub=col(p['diff_subln_g'][l]),
        lv=p['diff_lambda'][l].astype(F32),
        conv_w=p['conv_w'][l].astype(F32),
        w_out=p['w_out'][l].astype(BF16),
        norm_g=p['norm_g'][l].astype(F32),
    )


def _layer(x, m, w, l, consts):
    rope_cos, rope_sin, ftables = consts
    ng = w['norm_g']
    x = _ffn(x, m, ng[0:1], *w['ffn1'], 0)
    slope_parts = tuple(_bf16_parts(np.float32(s * LOG2E)) for s in _alibi_slopes())
    ab, u, fu, qd, kd, vd, qm, km, vm = _inproj(x, m, ng[1:2], w, rope_cos, rope_sin, slope_parts)
    lambda_init = 0.8 - 0.6 * math.exp(-0.3 * l)
    ybT = _attention(qd, kd, vd, w['gdq'], w['gdk'], w['gsub'], w['lv'],
                     width=DIFF_QK, n_maps=2, alibi=True, lambda_init=lambda_init)
    ycT = _attention(qm, km, vm, w['gmq'], w['gmk'], w['gsub'], w['lv'],
                     width=MLA_QK, n_maps=1, alibi=False, lambda_init=0.0)
    yd = _fourier(fu, ftables)
    return _outproj_ffn(x, m, ab, u, w['conv_w'], ybT, ycT, yd, w['w_out'], ng[2:3], *w['ffn2'])


def _trunk(groups, c_all, p):
    depth = p['w_mod'].shape[0]
    mods = _modulation(c_all, p['w_mod'], p['b_mod'])
    consts = []
    for x, _ in groups:
        S = x.shape[1]
        consts.append(_rope_tables(S) + (_fourier_tables(S),))
    xs = [x for x, _ in groups]
    for l in range(depth):
        w = _prep_layer(p, l)
        for gi, (x0, off) in enumerate(groups):
            B, _, D = x0.shape
            m = mods[l, off:off + B].reshape(B, N_MOD, D)
            xs[gi] = _layer(xs[gi], m, w, l, consts[gi])
    return xs


def kernel(x_prompt, x_sample, c_prompt, c_sample, w_mod, b_mod, norm_g, ffn1_w_gu, ffn1_w_down,
           w_in, conv_w, diff_lambda, diff_q_g, diff_k_g, diff_subln_g, mla_q_a_g, mla_w_uq,
           mla_kv_a_g, mla_w_ukv, mla_q_g, mla_k_g, w_out, ffn2_w_gu, ffn2_w_down):
    p = dict(w_mod=w_mod, b_mod=b_mod, norm_g=norm_g, ffn1_w_gu=ffn1_w_gu, ffn1_w_down=ffn1_w_down,
             w_in=w_in, conv_w=conv_w, diff_lambda=diff_lambda, diff_q_g=diff_q_g, diff_k_g=diff_k_g,
             diff_subln_g=diff_subln_g, mla_q_a_g=mla_q_a_g, mla_w_uq=mla_w_uq, mla_kv_a_g=mla_kv_a_g,
             mla_w_ukv=mla_w_ukv, mla_q_g=mla_q_g, mla_k_g=mla_k_g, w_out=w_out,
             ffn2_w_gu=ffn2_w_gu, ffn2_w_down=ffn2_w_down)
    nb = c_prompt.shape[0] + c_sample.shape[0]
    pad = (-nb) % 8
    c_all = jnp.concatenate([c_prompt, c_sample, jnp.zeros((pad, c_prompt.shape[1]), c_prompt.dtype)])
    y_prompt, y_sample = _trunk([(x_prompt, 0), (x_sample, c_prompt.shape[0])], c_all, p)
    return (y_prompt, y_sample)
```
